```python
import math
import jax, jax.numpy as jnp
from jax import lax
import numpy as np

D_MODEL = 2048
BATCH = 4
SEQ = 2048
DEPTH = 2

MIX_W = 1024
N_BRANCH = 4
CONV_W = 3
CONV_CH = MIX_W
S5_CH = MIX_W
S5_GROUP = 16
S5_GROUPS = S5_CH // S5_GROUP
S5_STATE = 64
SB_HEADS = 8
SB_HEAD_DIM = MIX_W // SB_HEADS
SB_BLOCK = 128
GLA_HEADS = 4
GLA_DK = MIX_W // 2 // GLA_HEADS
GLA_DV = MIX_W // GLA_HEADS
GLA_RANK = 16
GLA_TAU = 16.0
GLA_CHUNK = 16
D_FF = ((-(-8 * D_MODEL // 3)) + 255) // 256 * 256
DEEPNORM_ALPHA = (2.0 * DEPTH) ** 0.25
DEEPNORM_BETA = (8.0 * DEPTH) ** -0.25
IN_SPLITS = (CONV_CH, CONV_CH, CONV_CH,
             S5_CH,
             MIX_W, MIX_W, MIX_W,
             GLA_HEADS * GLA_DK, GLA_HEADS * GLA_DK,
             GLA_HEADS * GLA_DV, GLA_HEADS * GLA_DV,
             GLA_RANK,
             D_MODEL, D_MODEL, D_MODEL, D_MODEL)
IN_WIDTH = sum(IN_SPLITS)

kernel_name = 'hybrid_conv_s5_stickbreak_gla_deepnorm_adaln'


def _layer_norm(x, eps=1e-5):
    x32 = x.astype(jnp.float32)
    mu = jnp.mean(x32, axis=-1, keepdims=True)
    var = jnp.mean(jnp.square(x32 - mu), axis=-1, keepdims=True)
    return ((x32 - mu) * lax.rsqrt(var + eps)).astype(x.dtype)


def _short_conv(b_gate, c_gate, x_in, conv_w):
    h = c_gate * x_in
    y = lax.conv_general_dilated(h, conv_w[:, None, :].astype(h.dtype), (1,), [(CONV_W - 1, 0)],
                                 dimension_numbers=('NWC', 'WIO', 'NWC'),
                                 feature_group_count=h.shape[-1])
    return b_gate * y


def _s5(u, lam_re, lam_im, b_re, b_im, c_re, c_im, d_skip, log_dt):
    f32 = jnp.float32
    bsz, seq, _ = u.shape
    u32 = u.astype(f32)
    ug = u32.reshape(bsz, seq, S5_GROUPS, S5_GROUP)
    lr, li = lam_re.astype(f32), lam_im.astype(f32)
    dt = jnp.exp(log_dt.astype(f32))[:, None]
    mag = jnp.exp(lr * dt)
    ang = li * dt
    ab_re, ab_im = mag * jnp.cos(ang), mag * jnp.sin(ang)
    den = lr * lr + li * li
    nr, ni = ab_re - 1.0, ab_im
    f_re = (nr * lr + ni * li) / den
    f_im = (ni * lr - nr * li) / den
    br, bi = b_re.astype(f32), b_im.astype(f32)
    bb_re = f_re[..., None] * br - f_im[..., None] * bi
    bb_im = f_re[..., None] * bi + f_im[..., None] * br
    xr = jnp.einsum('blgh,gph->blgp', ug, bb_re)
    xi = jnp.einsum('blgh,gph->blgp', ug, bb_im)
    a_re = jnp.broadcast_to(ab_re, xr.shape)
    a_im = jnp.broadcast_to(ab_im, xr.shape)

    def combine(e1, e2):
        a1r, a1i, b1r, b1i = e1
        a2r, a2i, b2r, b2i = e2
        return (a1r * a2r - a1i * a2i,
                a1r * a2i + a1i * a2r,
                a2r * b1r - a2i * b1i + b2r,
                a2r * b1i + a2i * b1r + b2i)

    _, _, sr, si = lax.associative_scan(combine, (a_re, a_im, xr, xi), axis=1)
    y = (jnp.einsum('blgp,ghp->blgh', sr, c_re.astype(f32))
         - jnp.einsum('blgp,ghp->blgh', si, c_im.astype(f32)))
    y = y.reshape(bsz, seq, S5_CH) + d_skip.astype(f32) * u32
    return y.astype(u.dtype)


def _stick_breaking(q, k, v):
    bsz, seq, nh, dh = q.shape
    q = q * dh ** -0.5
    outs = []
    for i in range(seq // SB_BLOCK):
        t0 = i * SB_BLOCK
        kv_len = t0 + SB_BLOCK
        q_blk = q[:, t0:kv_len]
        k_blk, v_blk = k[:, :kv_len], v[:, :kv_len]
        z = jnp.einsum('bthd,bshd->bhts', q_blk, k_blk).astype(jnp.float32)
        t_pos = t0 + jnp.arange(SB_BLOCK)[:, None]
        s_pos = jnp.arange(kv_len)[None, :]
        mask = s_pos < t_pos
        log_beta = jax.nn.log_sigmoid(z)
        log_1m = jnp.where(mask, jax.nn.log_sigmoid(-z), 0.0)
        tail = lax.cumsum(log_1m, axis=3, reverse=True) - log_1m
        w = jnp.where(mask, jnp.exp(log_beta + tail), 0.0)
        outs.append(jnp.einsum('bhts,bshd->bthd', w.astype(v.dtype), v_blk))
    return jnp.concatenate(outs, axis=1)


def _gla_chunked(q, k, v, log_a):
    f32 = jnp.float32
    bsz, seq, nh, dk = q.shape
    dv = v.shape[-1]
    n = seq // GLA_CHUNK

    def chunks(t):
        return t.astype(f32).reshape(bsz, n, GLA_CHUNK, nh, t.shape[-1]).transpose(0, 3, 1, 2, 4)

    qc = chunks(q) * dk ** -0.5
    kc, vc, la = chunks(k), chunks(v), chunks(log_a)
    b = jnp.cumsum(la, axis=3)
    causal = jnp.tril(jnp.ones((GLA_CHUNK, GLA_CHUNK), bool))
    diff = b[:, :, :, :, None, :] - b[:, :, :, None, :, :]
    decay = jnp.exp(jnp.where(causal[:, :, None], diff, -jnp.inf))
    scores = jnp.einsum('bhntk,bhnsk,bhntsk->bhnts', qc, kc, decay)
    o_intra = jnp.einsum('bhnts,bhnsv->bhntv', scores, vc)
    b_end = b[:, :, :, -1:, :]
    q_dec = qc * jnp.exp(b)
    k_dec = kc * jnp.exp(b_end - b)
    a_end = jnp.exp(b_end[:, :, :, 0, :])

    def step(state, inp):
        q_n, k_n, v_n, a_n = inp
        o_n = jnp.einsum('bhtk,bhkv->bhtv', q_n, state)
        state = a_n[..., None] * state + jnp.einsum('bhsk,bhsv->bhkv', k_n, v_n)
        return state, o_n

    xs = (jnp.moveaxis(q_dec, 2, 0), jnp.moveaxis(k_dec, 2, 0),
          jnp.moveaxis(vc, 2, 0), jnp.moveaxis(a_end, 2, 0))
    s0 = jnp.zeros((bsz, nh, dk, dv), f32)
    _, o_inter = lax.scan(step, s0, xs)
    o = o_intra + jnp.moveaxis(o_inter, 0, 2)
    return o.transpose(0, 2, 3, 1, 4).reshape(bsz, seq, nh, dv).astype(q.dtype)


def _hybrid_mixer(h, w_in, conv_w, w_conv_out, lam_re, lam_im, b_re, b_im, c_re, c_im, d_skip,
                  log_dt, w_s5_val, w_s5_gate, w_sb_out, gla_w_gate, gla_b_gate, gla_norm_g,
                  w_gla_out, w_o):
    bsz, seq, _ = h.shape
    proj = h @ w_in
    split_at = np.cumsum(IN_SPLITS)[:-1].tolist()
    (cv_b, cv_c, cv_x, s5_u, sb_q, sb_k, sb_v, gl_q, gl_k, gl_v, gl_r, gl_lr,
     g_conv, g_s5, g_sb, g_gla) = jnp.split(proj, split_at, axis=-1)

    def heads(t, nh):
        return t.reshape(bsz, seq, nh, -1)

    y_conv = _short_conv(cv_b, cv_c, cv_x, conv_w) @ w_conv_out
    s = jax.nn.gelu(_s5(s5_u, lam_re, lam_im, b_re, b_im, c_re, c_im, d_skip, log_dt))
    y_s5 = (s @ w_s5_val) * jax.nn.sigmoid(s @ w_s5_gate)
    o_sb = _stick_breaking(heads(sb_q, SB_HEADS), heads(sb_k, SB_HEADS), heads(sb_v, SB_HEADS))
    y_sb = o_sb.reshape(bsz, seq, MIX_W) @ w_sb_out
    log_a = jax.nn.log_sigmoid((gl_lr @ gla_w_gate + gla_b_gate).astype(jnp.float32)) / GLA_TAU
    o_gla = _gla_chunked(heads(gl_q, GLA_HEADS), heads(gl_k, GLA_HEADS), heads(gl_v, GLA_HEADS),
                         log_a.reshape(bsz, seq, GLA_HEADS, GLA_DK))
    o_gla = _layer_norm(o_gla) * gla_norm_g.reshape(GLA_HEADS, GLA_DV)
    y_gla = (o_gla.reshape(bsz, seq, MIX_W) * jax.nn.silu(gl_r)) @ w_gla_out
    merged = (jax.nn.sigmoid(g_conv) * y_conv + jax.nn.sigmoid(g_s5) * y_s5
              + jax.nn.sigmoid(g_sb) * y_sb + jax.nn.sigmoid(g_gla) * y_gla)
    return merged @ w_o


def setup_inputs(seed: int = 0) -> dict:
    key = jax.random.key(seed)
    ks = jax.random.split(key, 32)
    f32 = jnp.float32

    def nrm(i, shape, scale):
        return jax.random.normal(ks[i], shape, f32) * scale

    L, D = DEPTH, D_MODEL
    G, P, H = S5_GROUPS, S5_STATE, S5_GROUP
    return {
        'x': nrm(0, (BATCH, SEQ, D), 1.0),
        'c': nrm(1, (BATCH, D), 1.0),
        'ada_w': nrm(2, (L, D, 6 * D), 0.5 * D ** -0.5),
        'ada_b': nrm(3, (L, 6 * D), 0.02),
        'w_in': nrm(4, (L, D, IN_WIDTH), D ** -0.5),
        'conv_w': nrm(5, (L, CONV_W, CONV_CH), CONV_W ** -0.5),
        'w_conv_out': nrm(6, (L, CONV_CH, D), CONV_CH ** -0.5),
        's5_lam_re': -0.5 + nrm(7, (L, G, P), 0.01),
        's5_lam_im': jnp.pi * jnp.arange(P, dtype=f32) + nrm(8, (L, G, P), 0.01),
        's5_b_re': nrm(9, (L, G, P, H), (2 * H) ** -0.5),
        's5_b_im': nrm(10, (L, G, P, H), (2 * H) ** -0.5),
        's5_c_re': nrm(11, (L, G, H, P), P ** -0.5),
        's5_c_im': nrm(12, (L, G, H, P), P ** -0.5),
        's5_d': nrm(13, (L, S5_CH), 1.0),
        's5_log_dt': jax.random.uniform(ks[14], (L, G), f32, math.log(1e-3), math.log(1e-1)),
        'w_s5_val': nrm(15, (L, S5_CH, D), S5_CH ** -0.5),
        'w_s5_gate': nrm(16, (L, S5_CH, D), S5_CH ** -0.5),
        'w_sb_out': nrm(17, (L, MIX_W, D), MIX_W ** -0.5),
        'gla_w_gate': nrm(18, (L, GLA_RANK, GLA_HEADS * GLA_DK), GLA_RANK ** -0.5),
        'gla_b_gate': nrm(19, (L, GLA_HEADS * GLA_DK), 0.1),
        'gla_norm_g': 1.0 + nrm(20, (L, GLA_HEADS * GLA_DV), 0.02),
        'w_gla_out': nrm(21, (L, MIX_W, D), MIX_W ** -0.5),
        'w_o': nrm(22, (L, D, D), DEEPNORM_BETA * D ** -0.5),
        'ln1_g': 1.0 + nrm(23, (L, D), 0.02),
        'ln1_b': nrm(24, (L, D), 0.02),
        'ffn_w_gate': nrm(25, (L, D, D_FF), D ** -0.5),
        'ffn_w_up': nrm(26, (L, D, D_FF), D ** -0.5),
        'ffn_w_down': nrm(27, (L, D_FF, D), DEEPNORM_BETA * D_FF ** -0.5),
        'ln2_g': 1.0 + nrm(28, (L, D), 0.02),
        'ln2_b': nrm(29, (L, D), 0.02),
    }


def reference(x, c, ada_w, ada_b, w_in, conv_w, w_conv_out, s5_lam_re, s5_lam_im, s5_b_re, s5_b_im,
              s5_c_re, s5_c_im, s5_d, s5_log_dt, w_s5_val, w_s5_gate, w_sb_out, gla_w_gate,
              gla_b_gate, gla_norm_g, w_gla_out, w_o, ln1_g, ln1_b, ffn_w_gate, ffn_w_up,
              ffn_w_down, ln2_g, ln2_b):
    for l in range(DEPTH):
        mod = jax.nn.silu(c) @ ada_w[l] + ada_b[l]
        sh1, sc1, gt1, sh2, sc2, gt2 = [m[:, None, :] for m in jnp.split(mod, 6, axis=-1)]
        h = _layer_norm(x) * (1.0 + sc1) + sh1
        mix = _hybrid_mixer(h, w_in[l], conv_w[l], w_conv_out[l], s5_lam_re[l], s5_lam_im[l],
                            s5_b_re[l], s5_b_im[l], s5_c_re[l], s5_c_im[l], s5_d[l], s5_log_dt[l],
                            w_s5_val[l], w_s5_gate[l], w_sb_out[l], gla_w_gate[l], gla_b_gate[l],
                            gla_norm_g[l], w_gla_out[l], w_o[l])
        x = _layer_norm(DEEPNORM_ALPHA * x + gt1 * mix) * ln1_g[l] + ln1_b[l]
        h = _layer_norm(x) * (1.0 + sc2) + sh2
        ff = (jax.nn.silu(h @ ffn_w_gate[l]) * (h @ ffn_w_up[l])) @ ffn_w_down[l]
        x = _layer_norm(DEEPNORM_ALPHA * x + gt2 * ff) * ln2_g[l] + ln2_b[l]
    return x
```

```python
import functools
import math

import jax
import jax.numpy as jnp
from jax import lax
from jax.experimental import pallas as pl
from jax.experimental.pallas import tpu as pltpu

F32 = jnp.float32
BF16 = jnp.bfloat16

LN_EPS = 1e-5
LANES = 128
MIX_W = 1024
S5_GROUP = 16
S5_STATE = 64
S5_CHUNK = 16
SB_HEADS = 8
SB_DH = 128
GLA_HEADS = 4
GLA_DK = 128
GLA_DV = 256
GLA_RANK = 16
GLA_TAU = 16.0
GLA_CHUNK = 128
GLA_SUB = 16
VMEM_LIMIT = 52 * 1024 * 1024

OFF_CV_B, OFF_CV_C, OFF_CV_X = 0, 1024, 2048
OFF_S5_U = 3072
OFF_SB_Q, OFF_SB_K, OFF_SB_V = 4096, 5120, 6144
OFF_GL_Q, OFF_GL_K, OFF_GL_V, OFF_GL_R = 7168, 7680, 8192, 9216
OFF_G_CONV, OFF_G_S5, OFF_G_SB, OFF_G_GLA = 10240, 12288, 14336, 16384
OFF_GL_LR = 18432
PROJ_W = 18560
SRC_GL_LR = 10240
SRC_GATES = 10256


def _cparams(sem):
    return pltpu.CompilerParams(dimension_semantics=sem, vmem_limit_bytes=VMEM_LIMIT)


def _layer_norm_rows(x):
    mu = jnp.mean(x, axis=-1, keepdims=True)
    xc = x - mu
    var = jnp.mean(xc * xc, axis=-1, keepdims=True)
    return xc * lax.rsqrt(var + LN_EPS)


def _sigmoid(x):
    return 1.0 / (1.0 + jnp.exp(-x))


def _log_sigmoid(x):
    return jnp.minimum(x, 0.0) - jnp.log(1.0 + jnp.exp(-jnp.abs(x)))


def _ada_kernel(c_ref, w_ref, b_ref, o_ref):
    c = c_ref[...]
    s = (c * _sigmoid(c)).astype(BF16)
    o_ref[0] = jnp.dot(s, w_ref[0].astype(BF16), preferred_element_type=F32) + b_ref[0]


def _ada_mod(c, ada_w, ada_b):
    depth, d, n = ada_w.shape
    bsz = c.shape[0]
    rows = 8
    c_pad = jnp.zeros((rows, d), F32).at[:bsz].set(c)
    tn = 1024
    out = pl.pallas_call(
        _ada_kernel,
        out_shape=jax.ShapeDtypeStruct((depth, rows, n), F32),
        grid=(depth, n // tn),
        in_specs=[
            pl.BlockSpec((rows, d), lambda l, j: (0, 0)),
            pl.BlockSpec((1, d, tn), lambda l, j: (l, 0, j)),
            pl.BlockSpec((1, 1, tn), lambda l, j: (l, 0, j)),
        ],
        out_specs=pl.BlockSpec((1, rows, tn), lambda l, j: (l, 0, j)),
        compiler_params=_cparams(("parallel", "parallel")),
        name="ada_mod",
    )(c_pad, ada_w, ada_b.reshape(depth, 1, n))
    return out[:, :bsz]


def _inproj_kernel(x_ref, sc_ref, sh_ref, w_ref, o_ref, h_ref):
    @pl.when(pl.program_id(1) == 0)
    def _():
        h = _layer_norm_rows(x_ref[...]) * (1.0 + sc_ref[0]) + sh_ref[0]
        h_ref[...] = h.astype(BF16)

    o_ref[...] = jnp.dot(h_ref[...], w_ref[...], preferred_element_type=F32).astype(BF16)


def _inproj(x, sc, sh, w, seq, tm, tn):
    n_tok, d = x.shape
    width = w.shape[1]
    per_seq = seq // tm
    return pl.pallas_call(
        _inproj_kernel,
        out_shape=jax.ShapeDtypeStruct((n_tok, width), BF16),
        grid=(n_tok // tm, width // tn),
        in_specs=[
            pl.BlockSpec((tm, d), lambda i, j: (i, 0)),
            pl.BlockSpec((1, 1, d), lambda i, j: (i // per_seq, 0, 0)),
            pl.BlockSpec((1, 1, d), lambda i, j: (i // per_seq, 0, 0)),
            pl.BlockSpec((d, tn), lambda i, j: (0, j)),
        ],
        out_specs=pl.BlockSpec((tm, tn), lambda i, j: (i, j)),
        scratch_shapes=[pltpu.VMEM((tm, d), BF16)],
        compiler_params=_cparams(("parallel", "arbitrary")),
        name="ln_mod_inproj",
    )(x, sc, sh, w)


def _conv_kernel(b_ref, c_ref, x_ref, cp_ref, xp_ref, w_ref, o_ref, *, per_seq):
    h = c_ref[...].astype(F32) * x_ref[...].astype(F32)
    hp = cp_ref[...].astype(F32) * xp_ref[...].astype(F32)
    first = (pl.program_id(0) % per_seq) == 0
    hp = jnp.where(first, 0.0, hp)
    row = lax.broadcasted_iota(jnp.int32, h.shape, 0)
    rows_p = hp.shape[0]
    h1 = jnp.where(row == 0, hp[rows_p - 1:rows_p], pltpu.roll(h, 1, 0))
    h2 = pltpu.roll(h, 2, 0)
    h2 = jnp.where(row == 0, hp[rows_p - 2:rows_p - 1], h2)
    h2 = jnp.where(row == 1, hp[rows_p - 1:rows_p], h2)
    w = w_ref[...]
    y = w[0:1] * h2 + w[1:2] * h1 + w[2:3] * h
    o_ref[...] = (b_ref[...].astype(F32) * y).astype(BF16)


def _short_conv(proj, conv_w, seq, tc):
    n_tok = proj.shape[0]
    halo = 16
    per_seq = seq // tc
    ratio = tc // halo

    def prev(i):
        return jnp.maximum(i * ratio - 1, 0)

    return pl.pallas_call(
        functools.partial(_conv_kernel, per_seq=per_seq),
        out_shape=jax.ShapeDtypeStruct((n_tok, MIX_W), BF16),
        grid=(n_tok // tc,),
        in_specs=[
            pl.BlockSpec((tc, MIX_W), lambda i: (i, OFF_CV_B // MIX_W)),
            pl.BlockSpec((tc, MIX_W), lambda i: (i, OFF_CV_C // MIX_W)),
            pl.BlockSpec((tc, MIX_W), lambda i: (i, OFF_CV_X // MIX_W)),
            pl.BlockSpec((halo, MIX_W), lambda i: (prev(i), OFF_CV_C // MIX_W)),
            pl.BlockSpec((halo, MIX_W), lambda i: (prev(i), OFF_CV_X // MIX_W)),
            pl.BlockSpec((8, MIX_W), lambda i: (0, 0)),
        ],
        out_specs=pl.BlockSpec((tc, MIX_W), lambda i: (i, 0)),
        compiler_params=_cparams(("parallel",)),
        name="short_conv",
    )(proj, proj, proj, proj, proj, jnp.zeros((8, MIX_W), F32).at[:conv_w.shape[0]].set(conv_w))


def _s5_kernel(u_ref, mi_ref, msr_ref, msi_ref, mor_ref, moi_ref, ar_ref, ai_ref, d_ref, o_ref,
               *, n_chunks, bsz):
    u = u_ref[0]
    s_re = jnp.dot(u, msr_ref[0], preferred_element_type=F32)
    s_im = jnp.dot(u, msi_ref[0], preferred_element_type=F32)
    ar = ar_ref[0]
    ai = ai_ref[0]
    row = lax.broadcasted_iota(jnp.int32, s_re.shape, 0)

    def shifted(a, rows):
        return jnp.where(row >= rows, pltpu.roll(a, rows, 0), 0.0)

    step = 1
    while step < n_chunks:
        p_re, p_im = shifted(s_re, step * bsz), shifted(s_im, step * bsz)
        s_re, s_im = s_re + ar * p_re - ai * p_im, s_im + ar * p_im + ai * p_re
        ar, ai = ar * ar - ai * ai, 2.0 * ar * ai
        step *= 2
    prev_re, prev_im = shifted(s_re, bsz), shifted(s_im, bsz)
    y = jnp.dot(u, mi_ref[0], preferred_element_type=F32)
    y += jnp.dot(prev_re.astype(BF16), mor_ref[0], preferred_element_type=F32)
    y += jnp.dot(prev_im.astype(BF16), moi_ref[0], preferred_element_type=F32)
    y += d_ref[0] * u.astype(F32)
    inner = math.sqrt(2.0 / math.pi) * (y + 0.044715 * (y * y * y))
    o_ref[0] = (0.5 * y * (1.0 + jnp.tanh(inner))).astype(BF16)


def _s5_matrices(lam_re, lam_im, b_re, b_im, c_re, c_im, d_skip, log_dt):
    g, p = lam_re.shape
    hh = S5_GROUP
    t = S5_CHUNK
    hi = lax.Precision.HIGHEST
    dt = jnp.exp(log_dt)[:, None]
    mag = jnp.exp(lam_re * dt)
    ang = lam_im * dt
    ab_re, ab_im = mag * jnp.cos(ang), mag * jnp.sin(ang)
    den = lam_re * lam_re + lam_im * lam_im
    nr, ni = ab_re - 1.0, ab_im
    f_re = (nr * lam_re + ni * lam_im) / den
    f_im = (ni * lam_re - nr * lam_im) / den
    bb_re = f_re[..., None] * b_re - f_im[..., None] * b_im
    bb_im = f_re[..., None] * b_im + f_im[..., None] * b_re
    tau = jnp.arange(t + 1, dtype=F32)[:, None, None]
    pw_mag = jnp.exp(tau * (lam_re * dt)[None])
    pw_re = pw_mag * jnp.cos(tau * ang[None])
    pw_im = pw_mag * jnp.sin(tau * ang[None])
    cw_re = c_re[None] * pw_re[:, :, None, :] - c_im[None] * pw_im[:, :, None, :]
    cw_im = c_re[None] * pw_im[:, :, None, :] + c_im[None] * pw_re[:, :, None, :]
    kern = (jnp.einsum('tghp,gpk->gthk', cw_re[:t], bb_re, precision=hi)
            - jnp.einsum('tghp,gpk->gthk', cw_im[:t], bb_im, precision=hi))
    lag = jnp.arange(t)[None, :] - jnp.arange(t)[:, None]
    blocks = jnp.where((lag >= 0)[None, :, :, None, None], kern[:, jnp.clip(lag, 0, t - 1)], 0.0)
    m_intra = blocks.transpose(0, 1, 4, 2, 3).reshape(g, t * hh, t * hh)
    rev_re = pw_re[:t][::-1]
    rev_im = pw_im[:t][::-1]
    ms_re = (rev_re[:, :, :, None] * bb_re[None] - rev_im[:, :, :, None] * bb_im[None])
    ms_im = (rev_re[:, :, :, None] * bb_im[None] + rev_im[:, :, :, None] * bb_re[None])
    ms_re = ms_re.transpose(1, 0, 3, 2).reshape(g, t * hh, p)
    ms_im = ms_im.transpose(1, 0, 3, 2).reshape(g, t * hh, p)
    mo_re = cw_re[1:].transpose(1, 3, 0, 2).reshape(g, p, t * hh)
    mo_im = (-cw_im[1:]).transpose(1, 3, 0, 2).reshape(g, p, t * hh)
    a_re = pw_re[t].reshape(g, 1, p)
    a_im = pw_im[t].reshape(g, 1, p)
    d_row = jnp.broadcast_to(d_skip.reshape(g, 1, hh), (g, t, hh)).reshape(g, 1, t * hh)
    return (m_intra.astype(BF16), ms_re.astype(BF16), ms_im.astype(BF16), mo_re.astype(BF16),
            mo_im.astype(BF16), a_re, a_im, d_row)


def _s5_gelu(proj, mats, bsz, seq):
    m_intra, ms_re, ms_im, mo_re, mo_im, a_re, a_im, d_row = mats
    g = m_intra.shape[0]
    t, hh, p = S5_CHUNK, S5_GROUP, S5_STATE
    n_chunks = seq // t
    rows = n_chunks * bsz
    u = proj[:, OFF_S5_U:OFF_S5_U + MIX_W].reshape(bsz, n_chunks, t, g, hh)
    u = u.transpose(3, 1, 0, 2, 4).reshape(g, rows, t * hh)
    grp = lambda shape: pl.BlockSpec((1,) + shape, lambda i: (i, 0, 0))
    y = pl.pallas_call(
        functools.partial(_s5_kernel, n_chunks=n_chunks, bsz=bsz),
        out_shape=jax.ShapeDtypeStruct((g, rows, t * hh), BF16),
        grid=(g,),
        in_specs=[grp((rows, t * hh)), grp((t * hh, t * hh)), grp((t * hh, p)), grp((t * hh, p)),
                  grp((p, t * hh)), grp((p, t * hh)), grp((1, p)), grp((1, p)), grp((1, t * hh))],
        out_specs=grp((rows, t * hh)),
        compiler_params=_cparams(("parallel",)),
        name="s5_chunked",
    )(u, m_intra, ms_re, ms_im, mo_re, mo_im, a_re, a_im, d_row)
    y = y.reshape(g, n_chunks, bsz, t, hh).transpose(2, 1, 3, 0, 4)
    return y.reshape(bsz * seq, g * hh)


def _sb_kernel(q_ref, k_ref, v_ref, o_ref, *, tq):
    i = pl.program_id(2)
    q = q_ref[...]
    scale = SB_DH ** -0.5
    row = lax.broadcasted_iota(jnp.int32, (tq, tq), 0)
    col = lax.broadcasted_iota(jnp.int32, (tq, tq), 1)
    below = row > col
    tri = jnp.where(below, 1.0, 0.0).astype(BF16)

    def block(j, carry, acc, diag):
        rows = pl.ds(pl.multiple_of(j * tq, tq), tq)
        kj = k_ref[rows, :]
        vj = v_ref[rows, :]
        z = lax.dot_general(q, kj, (((1,), (1,)), ((), ())), preferred_element_type=F32) * scale
        lp = jnp.log(1.0 + jnp.exp(-jnp.abs(z)))
        log_beta = jnp.minimum(z, 0.0) - lp
        log_1m = log_beta - z
        if diag:
            log_1m = jnp.where(below, log_1m, 0.0)
        tail = jnp.dot(log_1m.astype(BF16), tri, preferred_element_type=F32) + carry
        w = jnp.exp(log_beta + tail)
        if diag:
            w = jnp.where(below, w, 0.0)
        acc = acc + jnp.dot(w.astype(BF16), vj, preferred_element_type=F32)
        carry = carry + jnp.sum(log_1m, axis=-1, keepdims=True)
        return carry, acc

    carry, acc = block(i, jnp.zeros((tq, 1), F32), jnp.zeros((tq, SB_DH), F32), True)

    def body(jj, state):
        return block(i - 1 - jj, state[0], state[1], False)

    carry, acc = lax.fori_loop(0, i, body, (carry, acc))
    o_ref[...] = acc.astype(BF16)


def _stick_breaking(proj, bsz, seq, tq):
    n_tok = proj.shape[0]
    nq = seq // tq
    return pl.pallas_call(
        functools.partial(_sb_kernel, tq=tq),
        out_shape=jax.ShapeDtypeStruct((n_tok, MIX_W), BF16),
        grid=(bsz, SB_HEADS, nq),
        in_specs=[
            pl.BlockSpec((tq, SB_DH), lambda b, h, i: (b * nq + i, OFF_SB_Q // SB_DH + h)),
            pl.BlockSpec((seq, SB_DH), lambda b, h, i: (b, OFF_SB_K // SB_DH + h)),
            pl.BlockSpec((seq, SB_DH), lambda b, h, i: (b, OFF_SB_V // SB_DH + h)),
        ],
        out_specs=pl.BlockSpec((tq, SB_DH), lambda b, h, i: (b * nq + i, h)),
        compiler_params=_cparams(("parallel", "parallel", "arbitrary")),
        name="stick_breaking",
    )(proj, proj, proj)


def _split3(x):
    hi = x.astype(BF16)
    r1 = x - hi.astype(F32)
    mid = r1.astype(BF16)
    lo = (r1 - mid.astype(F32)).astype(BF16)
    return hi, mid, lo


def _gla_kernel(q_ref, k_ref, v_ref, r_ref, lr_ref, wg_ref, bg_ref, g_ref, o_ref, st_ref, sc_ref,
                *, n_chunks):
    c_len, sub = GLA_CHUNK, GLA_SUB
    st_ref[...] = jnp.zeros_like(st_ref)
    row = lax.broadcasted_iota(jnp.int32, (c_len, c_len), 0)
    col = lax.broadcasted_iota(jnp.int32, (c_len, c_len), 1)
    tri_incl = jnp.where(col <= row, 1.0, 0.0).astype(BF16)
    col_s = lax.broadcasted_iota(jnp.int32, (sub, c_len), 1)
    row_s = lax.broadcasted_iota(jnp.int32, (sub, GLA_DK), 0)
    nt = (((1,), (1,)), ((), ()))

    def chunk(c, _):
        rows = pl.ds(pl.multiple_of(c * c_len, c_len), c_len)
        q = q_ref[rows, :].astype(F32) * (GLA_DK ** -0.5)
        k = k_ref[rows, :].astype(F32)
        v = v_ref[rows, :]
        pre = jnp.dot(lr_ref[rows, :], wg_ref[...], preferred_element_type=F32) + bg_ref[...]
        la = _log_sigmoid(pre) * (1.0 / GLA_TAU)
        b = sum(jnp.dot(tri_incl, part, preferred_element_type=F32) for part in _split3(la))
        bx = b - la
        b_end = b[c_len - 1:c_len]
        st = st_ref[...]
        qe = (q * jnp.exp(b)).astype(BF16)
        o = lax.dot_general(qe, st.astype(BF16), nt, preferred_element_type=F32)
        for i in range(c_len // sub):
            lo_row = i * sub
            sl = slice(lo_row, lo_row + sub)
            bi, qi, ki = b[sl], q[sl], k[sl]
            ref_i = bx[lo_row:lo_row + 1]
            qd = (qi * jnp.exp(bi - ref_i)).astype(BF16)
            kd = (k * jnp.exp(jnp.minimum(ref_i - b, 0.0))).astype(BF16)
            sc = lax.dot_general(qd, kd, nt, preferred_element_type=F32)
            sc = jnp.where(col_s < lo_row, sc, 0.0)
            for s in range(sub):
                dec = jnp.exp(jnp.where(row_s >= s, bi - bi[s:s + 1], -1e30))
                cs = jnp.sum(qi * (ki[s:s + 1] * dec), axis=-1, keepdims=True)
                sc = jnp.where(col_s == lo_row + s, cs, sc)
            sc_ref[sl, :] = sc
        o = o + jnp.dot(sc_ref[...].astype(BF16), v, preferred_element_type=F32)
        ke = (k * jnp.exp(b_end - b)).astype(BF16)
        upd = lax.dot_general(v, ke, (((0,), (0,)), ((), ())), preferred_element_type=F32)
        st_ref[...] = st * jnp.exp(b_end) + upd
        on = _layer_norm_rows(o) * g_ref[...]
        r = r_ref[rows, :].astype(F32)
        o_ref[rows, :] = (on * (r * _sigmoid(r))).astype(BF16)
        return 0

    lax.fori_loop(0, n_chunks, chunk, 0)


def _gla(proj, w_gate, b_gate, norm_g, bsz, seq):
    n_tok = proj.shape[0]
    n_chunks = seq // GLA_CHUNK
    wg = jnp.zeros((LANES, GLA_HEADS * GLA_DK), BF16).at[:GLA_RANK].set(w_gate.astype(BF16))
    return pl.pallas_call(
        functools.partial(_gla_kernel, n_chunks=n_chunks),
        out_shape=jax.ShapeDtypeStruct((n_tok, MIX_W), BF16),
        grid=(bsz, GLA_HEADS),
        in_specs=[
            pl.BlockSpec((seq, GLA_DK), lambda b, h: (b, OFF_GL_Q // GLA_DK + h)),
            pl.BlockSpec((seq, GLA_DK), lambda b, h: (b, OFF_GL_K // GLA_DK + h)),
            pl.BlockSpec((seq, GLA_DV), lambda b, h: (b, OFF_GL_V // GLA_DV + h)),
            pl.BlockSpec((seq, GLA_DV), lambda b, h: (b, OFF_GL_R // GLA_DV + h)),
            pl.BlockSpec((seq, LANES), lambda b, h: (b, OFF_GL_LR // LANES)),
            pl.BlockSpec((LANES, GLA_DK), lambda b, h: (0, h)),
            pl.BlockSpec((1, GLA_DK), lambda b, h: (0, h)),
            pl.BlockSpec((1, GLA_DV), lambda b, h: (0, h)),
        ],
        out_specs=pl.BlockSpec((seq, GLA_DV), lambda b, h: (b, h)),
        scratch_shapes=[pltpu.VMEM((GLA_DV, GLA_DK), F32), pltpu.VMEM((GLA_CHUNK, GLA_CHUNK), F32)],
        compiler_params=_cparams(("parallel", "parallel")),
        name="gla",
    )(proj, proj, proj, proj, proj, wg, b_gate.reshape(1, -1), norm_g.reshape(1, -1))


def _residual_ln(x, y, gt, g, b, alpha):
    return _layer_norm_rows(alpha * x + gt * y) * g + b


def _merge_kernel(pc_ref, ps_ref, pb_ref, pg_ref, gc_ref, gs_ref, gb_ref, gg_ref, x_ref, gt_ref,
                  wc_ref, wv_ref, wt_ref, wb_ref, wl_ref, wo_ref, lg_ref, lb_ref, o_ref, acc_ref,
                  *, alpha):
    j = pl.program_id(1)

    @pl.when(j == 0)
    def _():
        acc_ref[...] = jnp.zeros_like(acc_ref)

    def mm(a_ref, w_ref):
        return jnp.dot(a_ref[...], w_ref[...], preferred_element_type=F32)

    def gate(ref):
        return _sigmoid(ref[...].astype(F32))

    s5 = ps_ref[...]
    y_s5 = jnp.dot(s5, wv_ref[...], preferred_element_type=F32) * _sigmoid(
        jnp.dot(s5, wt_ref[...], preferred_element_type=F32))
    merged = (gate(gc_ref) * mm(pc_ref, wc_ref) + gate(gs_ref) * y_s5
              + gate(gb_ref) * mm(pb_ref, wb_ref) + gate(gg_ref) * mm(pg_ref, wl_ref))
    acc_ref[...] += jnp.dot(merged.astype(BF16), wo_ref[...], preferred_element_type=F32)

    @pl.when(j == pl.num_programs(1) - 1)
    def _():
        o_ref[...] = _residual_ln(x_ref[...], acc_ref[...], gt_ref[0], lg_ref[...], lb_ref[...], alpha)


def _merge(pre_conv, pre_s5, pre_sb, pre_gla, proj, x, gt, w_conv, w_val, w_gate, w_sb, w_gla, w_o,
           ln_g, ln_b, seq, alpha, tm, tn):
    n_tok, d = x.shape
    per_seq = seq // tm
    pre = pl.BlockSpec((tm, MIX_W), lambda i, j: (i, 0))
    gcol = lambda off: pl.BlockSpec((tm, tn), lambda i, j: (i, off // tn + j))
    wcol = pl.BlockSpec((MIX_W, tn), lambda i, j: (0, j))
    vec = pl.BlockSpec((1, d), lambda i, j: (0, 0))
    return pl.pallas_call(
        functools.partial(_merge_kernel, alpha=alpha),
        out_shape=jax.ShapeDtypeStruct((n_tok, d), F32),
        grid=(n_tok // tm, d // tn),
        in_specs=[pre, pre, pre, pre,
                  gcol(OFF_G_CONV), gcol(OFF_G_S5), gcol(OFF_G_SB), gcol(OFF_G_GLA),
                  pl.BlockSpec((tm, d), lambda i, j: (i, 0)),
                  pl.BlockSpec((1, 1, d), lambda i, j: (i // per_seq, 0, 0)),
                  wcol, wcol, wcol, wcol, wcol,
                  pl.BlockSpec((tn, d), lambda i, j: (j, 0)),
                  vec, vec],
        out_specs=pl.BlockSpec((tm, d), lambda i, j: (i, 0)),
        scratch_shapes=[pltpu.VMEM((tm, d), F32)],
        compiler_params=_cparams(("parallel", "arbitrary")),
        name="merge_out_ln",
    )(pre_conv, pre_s5, pre_sb, pre_gla, proj, proj, proj, proj, x, gt,
      w_conv, w_val, w_gate, w_sb, w_gla, w_o, ln_g.reshape(1, d), ln_b.reshape(1, d))


def _ffn_kernel(x_ref, sc_ref, sh_ref, gt_ref, wg_ref, wu_ref, wd_ref, lg_ref, lb_ref, o_ref,
                h_ref, acc_ref, *, alpha):
    j = pl.program_id(1)

    @pl.when(j == 0)
    def _():
        h = _layer_norm_rows(x_ref[...]) * (1.0 + sc_ref[0]) + sh_ref[0]
        h_ref[...] = h.astype(BF16)
        acc_ref[...] = jnp.zeros_like(acc_ref)

    h = h_ref[...]
    gate = jnp.dot(h, wg_ref[...], preferred_element_type=F32)
    up = jnp.dot(h, wu_ref[...], preferred_element_type=F32)
    act = (gate * _sigmoid(gate) * up).astype(BF16)
    acc_ref[...] += jnp.dot(act, wd_ref[...], preferred_element_type=F32)

    @pl.when(j == pl.num_programs(1) - 1)
    def _():
        o_ref[...] = _residual_ln(x_ref[...], acc_ref[...], gt_ref[0], lg_ref[...], lb_ref[...], alpha)


def _ffn(x, sc, sh, gt, w_gate, w_up, w_down, ln_g, ln_b, seq, alpha, tm, tf):
    n_tok, d = x.shape
    d_ff = w_gate.shape[1]
    per_seq = seq // tm
    mod = pl.BlockSpec((1, 1, d), lambda i, j: (i // per_seq, 0, 0))
    vec = pl.BlockSpec((1, d), lambda i, j: (0, 0))
    return pl.pallas_call(
        functools.partial(_ffn_kernel, alpha=alpha),
        out_shape=jax.ShapeDtypeStruct((n_tok, d), F32),
        grid=(n_tok // tm, d_ff // tf),
        in_specs=[pl.BlockSpec((tm, d), lambda i, j: (i, 0)), mod, mod, mod,
                  pl.BlockSpec((d, tf), lambda i, j: (0, j)),
                  pl.BlockSpec((d, tf), lambda i, j: (0, j)),
                  pl.BlockSpec((tf, d), lambda i, j: (j, 0)),
                  vec, vec],
        out_specs=pl.BlockSpec((tm, d), lambda i, j: (i, 0)),
        scratch_shapes=[pltpu.VMEM((tm, d), BF16), pltpu.VMEM((tm, d), F32)],
        compiler_params=_cparams(("parallel", "arbitrary")),
        name="ffn_swiglu_ln",
    )(x, sc, sh, gt, w_gate, w_up, w_down, ln_g.reshape(1, d), ln_b.reshape(1, d))


def _rearrange_w_in(w):
    d = w.shape[0]
    pad = jnp.zeros((d, PROJ_W - OFF_GL_LR - GLA_RANK), w.dtype)
    return jnp.concatenate([w[:, :SRC_GL_LR], w[:, SRC_GATES:], w[:, SRC_GL_LR:SRC_GATES], pad],
                           axis=1).astype(BF16)


def kernel(x, c, ada_w, ada_b, w_in, conv_w, w_conv_out, s5_lam_re, s5_lam_im, s5_b_re, s5_b_im, s5_c_re, s5_c_im, s5_d, s5_log_dt, w_s5_val, w_s5_gate, w_sb_out, gla_w_gate, gla_b_gate, gla_norm_g, w_gla_out, w_o, ln1_g, ln1_b, ffn_w_gate, ffn_w_up, ffn_w_down, ln2_g, ln2_b):
    bsz, seq, d = x.shape
    depth = ada_w.shape[0]
    alpha = (2.0 * depth) ** 0.25
    mod = _ada_mod(c, ada_w, ada_b)
    xf = x.reshape(bsz * seq, d)
    bf = lambda a: a.astype(BF16)
    for l in range(depth):
        sh1, sc1, gt1, sh2, sc2, gt2 = [m.reshape(bsz, 1, d) for m in jnp.split(mod[l], 6, axis=-1)]
        proj = _inproj(xf, sc1, sh1, _rearrange_w_in(w_in[l]), seq, tm=512, tn=640)
        pre_conv = _short_conv(proj, conv_w[l], seq, tc=512)
        mats = _s5_matrices(s5_lam_re[l], s5_lam_im[l], s5_b_re[l], s5_b_im[l], s5_c_re[l],
                            s5_c_im[l], s5_d[l], s5_log_dt[l])
        pre_s5 = _s5_gelu(proj, mats, bsz, seq)
        pre_sb = _stick_breaking(proj, bsz, seq, tq=256)
        pre_gla = _gla(proj, gla_w_gate[l], gla_b_gate[l], gla_norm_g[l], bsz, seq)
        xf = _merge(pre_conv, pre_s5, pre_sb, pre_gla, proj, xf, gt1, bf(w_conv_out[l]),
                    bf(w_s5_val[l]), bf(w_s5_gate[l]), bf(w_sb_out[l]), bf(w_gla_out[l]), bf(w_o[l]),
                    ln1_g[l], ln1_b[l], seq, alpha, tm=512, tn=256)
        xf = _ffn(xf, sc2, sh2, gt2, bf(ffn_w_gate[l]), bf(ffn_w_up[l]), bf(ffn_w_down[l]),
                  ln2_g[l], ln2_b[l], seq, alpha, tm=512, tf=512)
    return xf.reshape(bsz, seq, d)
```

```python
import functools
import math

import jax
import jax.numpy as jnp
from jax import lax
from jax.experimental import pallas as pl
from jax.experimental.pallas import tpu as pltpu

F32 = jnp.float32
BF16 = jnp.bfloat16

LN_EPS = 1e-5
LANES = 128
MIX_W = 1024
S5_GROUP = 16
S5_STATE = 64
S5_CHUNK = 16
SB_HEADS = 8
SB_DH = 128
GLA_HEADS = 4
GLA_DK = 128
GLA_DV = 256
GLA_RANK = 16
GLA_TAU = 16.0
GLA_CHUNK = 128
GLA_SUB = 16
GLA_HEADS_PER_STEP = 2
VMEM_LIMIT = 52 * 1024 * 1024
SB_SKIP = 90.0

OFF_CV_B, OFF_CV_C, OFF_CV_X = 0, 1024, 2048
OFF_S5_U = 3072
OFF_SB_Q, OFF_SB_K, OFF_SB_V = 4096, 5120, 6144
OFF_GL_Q, OFF_GL_K, OFF_GL_V, OFF_GL_R = 7168, 7680, 8192, 9216
MAIN_W = 10240
OFF_G_CONV, OFF_G_S5, OFF_G_SB, OFF_G_GLA = 10240, 12288, 14336, 16384
PROJ_W = 18432
SRC_GL_LR = 10240
SRC_GATES = 10256


def _tiles(seq):
    return dict(inproj_tm=min(1024, seq), inproj_tn=1024, conv_tc=min(512, seq), sb_tq=min(256, seq),
                merge_tm=min(512, seq), merge_tn=512, ffn_tm=min(512, seq), ffn_tf=512)


def _cparams(sem):
    return pltpu.CompilerParams(dimension_semantics=sem, vmem_limit_bytes=VMEM_LIMIT)


def _layer_norm_rows(x):
    mu = jnp.mean(x, axis=-1, keepdims=True)
    xc = x - mu
    var = jnp.mean(xc * xc, axis=-1, keepdims=True)
    return xc * lax.rsqrt(var + LN_EPS)


ROW_CHUNK = 256


def _for_row_chunks(n_rows, fn):
    chunk = min(ROW_CHUNK, n_rows)

    def body(r, _):
        fn(pl.ds(pl.multiple_of(r * chunk, chunk), chunk))
        return 0

    lax.fori_loop(0, n_rows // chunk, body, 0)


def _ln_modulate_into(h_ref, x_ref, sc_ref, sh_ref):
    def rows(sl):
        h = _layer_norm_rows(x_ref[sl, :]) * (1.0 + sc_ref[0]) + sh_ref[0]
        h_ref[sl, :] = h.astype(BF16)

    _for_row_chunks(x_ref.shape[0], rows)


def _residual_ln_into(o_ref, x_ref, y_ref, gt_ref, g_ref, b_ref, alpha):
    def rows(sl):
        z = alpha * x_ref[sl, :] + gt_ref[0] * y_ref[sl, :]
        o_ref[sl, :] = _layer_norm_rows(z) * g_ref[...] + b_ref[...]

    _for_row_chunks(x_ref.shape[0], rows)


def _sigmoid(x):
    return 1.0 / (1.0 + jnp.exp(-x))


def _log_sigmoid(x):
    return jnp.minimum(x, 0.0) - jnp.log(1.0 + jnp.exp(-jnp.abs(x)))


def _ada_kernel(c_ref, w_ref, b_ref, o_ref):
    c = c_ref[...]
    s = (c * _sigmoid(c)).astype(BF16)
    o_ref[0] = jnp.dot(s, w_ref[0].astype(BF16), preferred_element_type=F32) + b_ref[0]


def _ada_mod(c, ada_w, ada_b):
    depth, d, n = ada_w.shape
    bsz = c.shape[0]
    rows = 8
    c_pad = jnp.zeros((rows, d), F32).at[:bsz].set(c)
    tn = 1024
    out = pl.pallas_call(
        _ada_kernel,
        out_shape=jax.ShapeDtypeStruct((depth, rows, n), F32),
        grid=(depth, n // tn),
        in_specs=[
            pl.BlockSpec((rows, d), lambda l, j: (0, 0)),
            pl.BlockSpec((1, d, tn), lambda l, j: (l, 0, j)),
            pl.BlockSpec((1, 1, tn), lambda l, j: (l, 0, j)),
        ],
        out_specs=pl.BlockSpec((1, rows, tn), lambda l, j: (l, 0, j)),
        compiler_params=_cparams(("parallel", "parallel")),
        name="ada_mod",
    )(c_pad, ada_w, ada_b.reshape(depth, 1, n))
    return out[:, :bsz]


def _inproj_kernel(x_ref, sc_ref, sh_ref, wm_ref, wg_ref, wl_ref, o_ref, lr_ref, h_ref, *, n_main):
    j = pl.program_id(1)

    @pl.when(j == 0)
    def _():
        _ln_modulate_into(h_ref, x_ref, sc_ref, sh_ref)
        lr_ref[...] = jnp.dot(h_ref[...], wl_ref[...], preferred_element_type=F32).astype(BF16)

    @pl.when(j < n_main)
    def _():
        o_ref[...] = jnp.dot(h_ref[...], wm_ref[...], preferred_element_type=F32).astype(BF16)

    @pl.when(j >= n_main)
    def _():
        o_ref[...] = jnp.dot(h_ref[...], wg_ref[...], preferred_element_type=F32).astype(BF16)


def _inproj(x, sc, sh, w_main, w_gates, w_lr, seq, tm, tn):
    n_tok, d = x.shape
    n_main = w_main.shape[1] // tn
    n_gate = w_gates.shape[1] // tn
    per_seq = seq // tm
    mod = pl.BlockSpec((1, 1, d), lambda i, j: (i // per_seq, 0, 0))
    return pl.pallas_call(
        functools.partial(_inproj_kernel, n_main=n_main),
        out_shape=(jax.ShapeDtypeStruct((n_tok, (n_main + n_gate) * tn), BF16),
                   jax.ShapeDtypeStruct((n_tok, LANES), BF16)),
        grid=(n_tok // tm, n_main + n_gate),
        in_specs=[
            pl.BlockSpec((tm, d), lambda i, j: (i, 0)),
            mod, mod,
            pl.BlockSpec((d, tn), lambda i, j: (0, jnp.minimum(j, n_main - 1))),
            pl.BlockSpec((d, tn), lambda i, j: (0, jnp.maximum(j - n_main, 0))),
            pl.BlockSpec((d, LANES), lambda i, j: (0, 0)),
        ],
        out_specs=(pl.BlockSpec((tm, tn), lambda i, j: (i, j)),
                   pl.BlockSpec((tm, LANES), lambda i, j: (i, 0))),
        scratch_shapes=[pltpu.VMEM((tm, d), BF16)],
        compiler_params=_cparams(("parallel", "arbitrary")),
        name="ln_mod_inproj",
    )(x, sc, sh, w_main, w_gates, w_lr)


def _conv_kernel(b_ref, c_ref, x_ref, cp_ref, xp_ref, w_ref, o_ref, *, per_seq):
    h = c_ref[...].astype(F32) * x_ref[...].astype(F32)
    hp = cp_ref[...].astype(F32) * xp_ref[...].astype(F32)
    first = (pl.program_id(0) % per_seq) == 0
    hp = jnp.where(first, 0.0, hp)
    row = lax.broadcasted_iota(jnp.int32, h.shape, 0)
    rows_p = hp.shape[0]
    h1 = jnp.where(row == 0, hp[rows_p - 1:rows_p], pltpu.roll(h, 1, 0))
    h2 = pltpu.roll(h, 2, 0)
    h2 = jnp.where(row == 0, hp[rows_p - 2:rows_p - 1], h2)
    h2 = jnp.where(row == 1, hp[rows_p - 1:rows_p], h2)
    w = w_ref[...]
    y = w[0:1] * h2 + w[1:2] * h1 + w[2:3] * h
    o_ref[...] = (b_ref[...].astype(F32) * y).astype(BF16)


def _short_conv(proj, conv_w, seq, tc):
    n_tok = proj.shape[0]
    halo = 16
    per_seq = seq // tc
    ratio = tc // halo

    def prev(i):
        return jnp.maximum(i * ratio - 1, 0)

    return pl.pallas_call(
        functools.partial(_conv_kernel, per_seq=per_seq),
        out_shape=jax.ShapeDtypeStruct((n_tok, MIX_W), BF16),
        grid=(n_tok // tc,),
        in_specs=[
            pl.BlockSpec((tc, MIX_W), lambda i: (i, OFF_CV_B // MIX_W)),
            pl.BlockSpec((tc, MIX_W), lambda i: (i, OFF_CV_C // MIX_W)),
            pl.BlockSpec((tc, MIX_W), lambda i: (i, OFF_CV_X // MIX_W)),
            pl.BlockSpec((halo, MIX_W), lambda i: (prev(i), OFF_CV_C // MIX_W)),
            pl.BlockSpec((halo, MIX_W), lambda i: (prev(i), OFF_CV_X // MIX_W)),
            pl.BlockSpec((8, MIX_W), lambda i: (0, 0)),
        ],
        out_specs=pl.BlockSpec((tc, MIX_W), lambda i: (i, 0)),
        compiler_params=_cparams(("parallel",)),
        name="short_conv",
    )(proj, proj, proj, proj, proj, jnp.zeros((8, MIX_W), F32).at[:conv_w.shape[0]].set(conv_w))


def _s5_kernel(u_ref, mi_ref, msr_ref, msi_ref, mor_ref, moi_ref, ar_ref, ai_ref, d_ref, o_ref,
               *, n_chunks):
    u = u_ref[0]
    s_re = jnp.dot(u, msr_ref[0], preferred_element_type=F32)
    s_im = jnp.dot(u, msi_ref[0], preferred_element_type=F32)
    ar = ar_ref[0]
    ai = ai_ref[0]
    chunk = lax.broadcasted_iota(jnp.int32, s_re.shape, 0) % n_chunks

    def shifted(a, rows):
        return jnp.where(chunk >= rows, pltpu.roll(a, rows, 0), 0.0)

    step = 1
    while step < n_chunks:
        p_re, p_im = shifted(s_re, step), shifted(s_im, step)
        s_re, s_im = s_re + ar * p_re - ai * p_im, s_im + ar * p_im + ai * p_re
        ar, ai = ar * ar - ai * ai, 2.0 * ar * ai
        step *= 2
    prev_re, prev_im = shifted(s_re, 1), shifted(s_im, 1)
    y = jnp.dot(u, mi_ref[0], preferred_element_type=F32)
    y += jnp.dot(prev_re.astype(BF16), mor_ref[0], preferred_element_type=F32)
    y += jnp.dot(prev_im.astype(BF16), moi_ref[0], preferred_element_type=F32)
    y += d_ref[0] * u.astype(F32)
    inner = math.sqrt(2.0 / math.pi) * (y + 0.044715 * (y * y * y))
    o_ref[0] = (0.5 * y * (1.0 + jnp.tanh(inner))).astype(BF16)


def _s5_matrices(lam_re, lam_im, b_re, b_im, c_re, c_im, d_skip, log_dt):
    g, p = lam_re.shape
    hh = S5_GROUP
    t = S5_CHUNK
    hi = lax.Precision.HIGHEST
    dt = jnp.exp(log_dt)[:, None]
    mag = jnp.exp(lam_re * dt)
    ang = lam_im * dt
    ab_re, ab_im = mag * jnp.cos(ang), mag * jnp.sin(ang)
    den = lam_re * lam_re + lam_im * lam_im
    nr, ni = ab_re - 1.0, ab_im
    f_re = (nr * lam_re + ni * lam_im) / den
    f_im = (ni * lam_re - nr * lam_im) / den
    bb_re = f_re[..., None] * b_re - f_im[..., None] * b_im
    bb_im = f_re[..., None] * b_im + f_im[..., None] * b_re
    tau = jnp.arange(t + 1, dtype=F32)[:, None, None]
    pw_mag = jnp.exp(tau * (lam_re * dt)[None])
    pw_re = pw_mag * jnp.cos(tau * ang[None])
    pw_im = pw_mag * jnp.sin(tau * ang[None])
    cw_re = c_re[None] * pw_re[:, :, None, :] - c_im[None] * pw_im[:, :, None, :]
    cw_im = c_re[None] * pw_im[:, :, None, :] + c_im[None] * pw_re[:, :, None, :]
    kern = (jnp.einsum('tghp,gpk->gthk', cw_re[:t], bb_re, precision=hi)
            - jnp.einsum('tghp,gpk->gthk', cw_im[:t], bb_im, precision=hi))
    lag = jnp.arange(t)[None, :] - jnp.arange(t)[:, None]
    blocks = jnp.where((lag >= 0)[None, :, :, None, None], kern[:, jnp.clip(lag, 0, t - 1)], 0.0)
    m_intra = blocks.transpose(0, 1, 4, 2, 3).reshape(g, t * hh, t * hh)
    rev_re = pw_re[:t][::-1]
    rev_im = pw_im[:t][::-1]
    ms_re = (rev_re[:, :, :, None] * bb_re[None] - rev_im[:, :, :, None] * bb_im[None])
    ms_im = (rev_re[:, :, :, None] * bb_im[None] + rev_im[:, :, :, None] * bb_re[None])
    ms_re = ms_re.transpose(1, 0, 3, 2).reshape(g, t * hh, p)
    ms_im = ms_im.transpose(1, 0, 3, 2).reshape(g, t * hh, p)
    mo_re = cw_re[1:].transpose(1, 3, 0, 2).reshape(g, p, t * hh)
    mo_im = (-cw_im[1:]).transpose(1, 3, 0, 2).reshape(g, p, t * hh)
    a_re = pw_re[t].reshape(g, 1, p)
    a_im = pw_im[t].reshape(g, 1, p)
    d_row = jnp.broadcast_to(d_skip.reshape(g, 1, hh), (g, t, hh)).reshape(g, 1, t * hh)
    return (m_intra.astype(BF16), ms_re.astype(BF16), ms_im.astype(BF16), mo_re.astype(BF16),
            mo_im.astype(BF16), a_re, a_im, d_row)


def _s5_gelu(proj, mats, bsz, seq):
    m_intra, ms_re, ms_im, mo_re, mo_im, a_re, a_im, d_row = mats
    g = m_intra.shape[0]
    t, hh, p = S5_CHUNK, S5_GROUP, S5_STATE
    n_chunks = seq // t
    rows = n_chunks * bsz
    u = proj[:, OFF_S5_U:OFF_S5_U + MIX_W].reshape(rows * t * hh, g)
    u = u.T.reshape(g, rows, t * hh)
    grp = lambda shape: pl.BlockSpec((1,) + shape, lambda i: (i, 0, 0))
    y = pl.pallas_call(
        functools.partial(_s5_kernel, n_chunks=n_chunks),
        out_shape=jax.ShapeDtypeStruct((g, rows, t * hh), BF16),
        grid=(g,),
        in_specs=[grp((rows, t * hh)), grp((t * hh, t * hh)), grp((t * hh, p)), grp((t * hh, p)),
                  grp((p, t * hh)), grp((p, t * hh)), grp((1, p)), grp((1, p)), grp((1, t * hh))],
        out_specs=grp((rows, t * hh)),
        compiler_params=_cparams(("parallel",)),
        name="s5_chunked",
    )(u, m_intra, ms_re, ms_im, mo_re, mo_im, a_re, a_im, d_row)
    return y.reshape(g, rows * t * hh).T.reshape(bsz * seq, hh * g)


def _channel_major(w, axis):
    g = w.shape[axis] // S5_GROUP
    shape = w.shape[:axis] + (g, S5_GROUP) + w.shape[axis + 1:]
    return jnp.swapaxes(w.reshape(shape), axis, axis + 1).reshape(w.shape)


def _sb_kernel(q_ref, k_ref, v_ref, o_ref, *, tq):
    i = pl.program_id(2)
    q = q_ref[...]
    scale = SB_DH ** -0.5
    row = lax.broadcasted_iota(jnp.int32, (tq, tq), 0)
    col = lax.broadcasted_iota(jnp.int32, (tq, tq), 1)
    below = row > col
    tri = jnp.where(below, 1.0, 0.0).astype(BF16)

    def logits(j, diag):
        rows = pl.ds(pl.multiple_of(j * tq, tq), tq)
        z = lax.dot_general(q, k_ref[rows, :], (((1,), (1,)), ((), ())),
                            preferred_element_type=F32) * scale
        lp = jnp.log(1.0 + jnp.exp(-jnp.abs(z)))
        log_beta = jnp.minimum(z, 0.0) - lp
        log_1m = log_beta - z
        if diag:
            log_1m = jnp.where(below, log_1m, 0.0)
        local = jnp.dot(log_1m.astype(BF16), tri, preferred_element_type=F32)
        return log_beta, local, jnp.sum(log_1m, axis=-1, keepdims=True), v_ref[rows, :]

    def weigh(parts, carry, diag):
        log_beta, local, total, vj = parts
        w = jnp.exp(log_beta + local + carry)
        if diag:
            w = jnp.where(below, w, 0.0)
        return jnp.dot(w.astype(BF16), vj, preferred_element_type=F32), carry + total

    has_prev = i > 0
    d_parts = logits(i, True)
    p_parts = logits(jnp.maximum(i - 1, 0), False)
    acc, carry = weigh(d_parts, jnp.zeros((tq, 1), F32), True)
    p_acc, p_carry = weigh(p_parts, carry, False)
    acc = jnp.where(has_prev, acc + p_acc, acc)
    carry = jnp.where(has_prev, p_carry, carry)

    def cond(state):
        j, top = state[0], state[1]
        return jnp.logical_and(j >= 0, top > -SB_SKIP)

    def body(state):
        j, _, carry, acc = state
        d_acc, carry = weigh(logits(j, False), carry, False)
        return j - 1, jnp.max(carry), carry, acc + d_acc

    state = lax.while_loop(cond, body, (i - 2, jnp.max(carry), carry, acc))
    o_ref[...] = state[3].astype(BF16)


def _stick_breaking(proj, bsz, seq, tq):
    n_tok = proj.shape[0]
    nq = seq // tq
    return pl.pallas_call(
        functools.partial(_sb_kernel, tq=tq),
        out_shape=jax.ShapeDtypeStruct((n_tok, MIX_W), BF16),
        grid=(bsz, SB_HEADS, nq),
        in_specs=[
            pl.BlockSpec((tq, SB_DH), lambda b, h, i: (b * nq + i, OFF_SB_Q // SB_DH + h)),
            pl.BlockSpec((seq, SB_DH), lambda b, h, i: (b, OFF_SB_K // SB_DH + h)),
            pl.BlockSpec((seq, SB_DH), lambda b, h, i: (b, OFF_SB_V // SB_DH + h)),
        ],
        out_specs=pl.BlockSpec((tq, SB_DH), lambda b, h, i: (b * nq + i, h)),
        compiler_params=_cparams(("parallel", "parallel", "arbitrary")),
        name="stick_breaking",
    )(proj, proj, proj)


def _split3(x):
    hi = x.astype(BF16)
    r1 = x - hi.astype(F32)
    mid = r1.astype(BF16)
    lo = (r1 - mid.astype(F32)).astype(BF16)
    return hi, mid, lo


def _gla_kernel(q_ref, k_ref, v_ref, r_ref, lr_ref, wg_ref, bg_ref, g_ref, o_ref, st_ref, sc_ref,
                *, n_chunks, n_heads):
    c_len, sub = GLA_CHUNK, GLA_SUB
    st_ref[...] = jnp.zeros_like(st_ref)
    row = lax.broadcasted_iota(jnp.int32, (c_len, c_len), 0)
    col = lax.broadcasted_iota(jnp.int32, (c_len, c_len), 1)
    tri_incl = jnp.where(col <= row, 1.0, 0.0).astype(BF16)
    col_s = lax.broadcasted_iota(jnp.int32, (sub, c_len), 1)
    row_s = lax.broadcasted_iota(jnp.int32, (sub, GLA_DK), 0)
    nt = (((1,), (1,)), ((), ()))

    def head_chunk(hd, rows):
        kcols = slice(hd * GLA_DK, (hd + 1) * GLA_DK)
        vcols = slice(hd * GLA_DV, (hd + 1) * GLA_DV)
        q = q_ref[rows, kcols].astype(F32) * (GLA_DK ** -0.5)
        k = k_ref[rows, kcols].astype(F32)
        v = v_ref[rows, vcols]
        pre = jnp.dot(lr_ref[rows, :], wg_ref[:, kcols], preferred_element_type=F32) + bg_ref[:, kcols]
        la = _log_sigmoid(pre) * (1.0 / GLA_TAU)
        b = sum(jnp.dot(tri_incl, part, preferred_element_type=F32) for part in _split3(la))
        bx = b - la
        b_end = b[c_len - 1:c_len]
        st = st_ref[hd]
        qe = (q * jnp.exp(b)).astype(BF16)
        o = lax.dot_general(qe, st.astype(BF16), nt, preferred_element_type=F32)
        for i in range(c_len // sub):
            lo_row = i * sub
            sl = slice(lo_row, lo_row + sub)
            bi, qi, ki = b[sl], q[sl], k[sl]
            ref_i = bx[lo_row:lo_row + 1]
            qd = (qi * jnp.exp(bi - ref_i)).astype(BF16)
            kd = (k * jnp.exp(jnp.minimum(ref_i - b, 0.0))).astype(BF16)
            sc = lax.dot_general(qd, kd, nt, preferred_element_type=F32)
            sc = jnp.where(col_s < lo_row, sc, 0.0)
            for s in range(sub):
                dec = jnp.exp(jnp.where(row_s >= s, bi - bi[s:s + 1], -1e30))
                cs = jnp.sum(qi * (ki[s:s + 1] * dec), axis=-1, keepdims=True)
                sc = jnp.where(col_s == lo_row + s, cs, sc)
            sc_ref[hd, sl, :] = sc
        o = o + jnp.dot(sc_ref[hd].astype(BF16), v, preferred_element_type=F32)
        ke = (k * jnp.exp(b_end - b)).astype(BF16)
        upd = lax.dot_general(v, ke, (((0,), (0,)), ((), ())), preferred_element_type=F32)
        st_ref[hd] = st * jnp.exp(b_end) + upd
        on = _layer_norm_rows(o) * g_ref[:, vcols]
        r = r_ref[rows, vcols].astype(F32)
        o_ref[rows, vcols] = (on * (r * _sigmoid(r))).astype(BF16)

    def chunk(c, _):
        rows = pl.ds(pl.multiple_of(c * c_len, c_len), c_len)
        for hd in range(n_heads):
            head_chunk(hd, rows)
        return 0

    lax.fori_loop(0, n_chunks, chunk, 0)


def _gla(proj, lr, w_gate, b_gate, norm_g, bsz, seq):
    n_tok = proj.shape[0]
    n_chunks = seq // GLA_CHUNK
    nh = GLA_HEADS_PER_STEP
    wk, wv = nh * GLA_DK, nh * GLA_DV
    wg = jnp.zeros((LANES, GLA_HEADS * GLA_DK), BF16).at[:GLA_RANK].set(w_gate.astype(BF16))
    return pl.pallas_call(
        functools.partial(_gla_kernel, n_chunks=n_chunks, n_heads=nh),
        out_shape=jax.ShapeDtypeStruct((n_tok, MIX_W), BF16),
        grid=(bsz, GLA_HEADS // nh),
        in_specs=[
            pl.BlockSpec((seq, wk), lambda b, h: (b, OFF_GL_Q // wk + h)),
            pl.BlockSpec((seq, wk), lambda b, h: (b, OFF_GL_K // wk + h)),
            pl.BlockSpec((seq, wv), lambda b, h: (b, OFF_GL_V // wv + h)),
            pl.BlockSpec((seq, wv), lambda b, h: (b, OFF_GL_R // wv + h)),
            pl.BlockSpec((seq, LANES), lambda b, h: (b, 0)),
            pl.BlockSpec((LANES, wk), lambda b, h: (0, h)),
            pl.BlockSpec((1, wk), lambda b, h: (0, h)),
            pl.BlockSpec((1, wv), lambda b, h: (0, h)),
        ],
        out_specs=pl.BlockSpec((seq, wv), lambda b, h: (b, h)),
        scratch_shapes=[pltpu.VMEM((nh, GLA_DV, GLA_DK), F32),
                        pltpu.VMEM((nh, GLA_CHUNK, GLA_CHUNK), F32)],
        compiler_params=_cparams(("parallel", "parallel")),
        name="gla",
    )(proj, proj, proj, proj, lr, wg, b_gate.reshape(1, -1), norm_g.reshape(1, -1))


def _merge_kernel(pc_ref, ps_ref, pb_ref, pg_ref, gc_ref, gs_ref, gb_ref, gg_ref, x_ref, gt_ref,
                  wc_ref, wv_ref, wt_ref, wb_ref, wl_ref, wo_ref, lg_ref, lb_ref, o_ref, acc_ref,
                  *, alpha):
    j = pl.program_id(1)

    @pl.when(j == 0)
    def _():
        acc_ref[...] = jnp.zeros_like(acc_ref)

    def mm(a_ref, w_ref):
        return jnp.dot(a_ref[...], w_ref[...], preferred_element_type=F32)

    def gate(ref):
        return _sigmoid(ref[...].astype(F32))

    s5 = ps_ref[...]
    y_s5 = jnp.dot(s5, wv_ref[...], preferred_element_type=F32) * _sigmoid(
        jnp.dot(s5, wt_ref[...], preferred_element_type=F32))
    merged = (gate(gc_ref) * mm(pc_ref, wc_ref) + gate(gs_ref) * y_s5
              + gate(gb_ref) * mm(pb_ref, wb_ref) + gate(gg_ref) * mm(pg_ref, wl_ref))
    acc_ref[...] += jnp.dot(merged.astype(BF16), wo_ref[...], preferred_element_type=F32)

    @pl.when(j == pl.num_programs(1) - 1)
    def _():
        _residual_ln_into(o_ref, x_ref, acc_ref, gt_ref, lg_ref, lb_ref, alpha)


def _merge(pre_conv, pre_s5, pre_sb, pre_gla, proj, x, gt, w_conv, w_val, w_gate, w_sb, w_gla, w_o,
           ln_g, ln_b, seq, alpha, tm, tn):
    n_tok, d = x.shape
    per_seq = seq // tm
    pre = pl.BlockSpec((tm, MIX_W), lambda i, j: (i, 0))
    gcol = lambda off: pl.BlockSpec((tm, tn), lambda i, j: (i, off // tn + j))
    wcol = pl.BlockSpec((MIX_W, tn), lambda i, j: (0, j))
    vec = pl.BlockSpec((1, d), lambda i, j: (0, 0))
    return pl.pallas_call(
        functools.partial(_merge_kernel, alpha=alpha),
        out_shape=jax.ShapeDtypeStruct((n_tok, d), F32),
        grid=(n_tok // tm, d // tn),
        in_specs=[pre, pre, pre, pre,
                  gcol(OFF_G_CONV), gcol(OFF_G_S5), gcol(OFF_G_SB), gcol(OFF_G_GLA),
                  pl.BlockSpec((tm, d), lambda i, j: (i, 0)),
                  pl.BlockSpec((1, 1, d), lambda i, j: (i // per_seq, 0, 0)),
                  wcol, wcol, wcol, wcol, wcol,
                  pl.BlockSpec((tn, d), lambda i, j: (j, 0)),
                  vec, vec],
        out_specs=pl.BlockSpec((tm, d), lambda i, j: (i, 0)),
        scratch_shapes=[pltpu.VMEM((tm, d), F32)],
        compiler_params=_cparams(("parallel", "arbitrary")),
        name="merge_out_ln",
    )(pre_conv, pre_s5, pre_sb, pre_gla, proj, proj, proj, proj, x, gt,
      w_conv, w_val, w_gate, w_sb, w_gla, w_o, ln_g.reshape(1, d), ln_b.reshape(1, d))


def _ffn_kernel(x_ref, sc_ref, sh_ref, gt_ref, wg_ref, wu_ref, wd_ref, lg_ref, lb_ref, o_ref,
                h_ref, acc_ref, *, alpha):
    j = pl.program_id(1)

    @pl.when(j == 0)
    def _():
        _ln_modulate_into(h_ref, x_ref, sc_ref, sh_ref)
        acc_ref[...] = jnp.zeros_like(acc_ref)

    h = h_ref[...]
    gate = jnp.dot(h, wg_ref[...], preferred_element_type=F32)
    up = jnp.dot(h, wu_ref[...], preferred_element_type=F32)
    act = (gate * _sigmoid(gate) * up).astype(BF16)
    acc_ref[...] += jnp.dot(act, wd_ref[...], preferred_element_type=F32)

    @pl.when(j == pl.num_programs(1) - 1)
    def _():
        _residual_ln_into(o_ref, x_ref, acc_ref, gt_ref, lg_ref, lb_ref, alpha)


def _ffn(x, sc, sh, gt, w_gate, w_up, w_down, ln_g, ln_b, seq, alpha, tm, tf):
    n_tok, d = x.shape
    d_ff = w_gate.shape[1]
    per_seq = seq // tm
    mod = pl.BlockSpec((1, 1, d), lambda i, j: (i // per_seq, 0, 0))
    vec = pl.BlockSpec((1, d), lambda i, j: (0, 0))
    return pl.pallas_call(
        functools.partial(_ffn_kernel, alpha=alpha),
        out_shape=jax.ShapeDtypeStruct((n_tok, d), F32),
        grid=(n_tok // tm, d_ff // tf),
        in_specs=[pl.BlockSpec((tm, d), lambda i, j: (i, 0)), mod, mod, mod,
                  pl.BlockSpec((d, tf), lambda i, j: (0, j)),
                  pl.BlockSpec((d, tf), lambda i, j: (0, j)),
                  pl.BlockSpec((tf, d), lambda i, j: (j, 0)),
                  vec, vec],
        out_specs=pl.BlockSpec((tm, d), lambda i, j: (i, 0)),
        scratch_shapes=[pltpu.VMEM((tm, d), BF16), pltpu.VMEM((tm, d), F32)],
        compiler_params=_cparams(("parallel", "arbitrary")),
        name="ffn_swiglu_ln",
    )(x, sc, sh, gt, w_gate, w_up, w_down, ln_g.reshape(1, d), ln_b.reshape(1, d))


def _split_w_in(w):
    d = w.shape[0]
    s5_end = OFF_S5_U + MIX_W
    w_main = jnp.concatenate([w[:, :OFF_S5_U].astype(BF16),
                              _channel_major(w[:, OFF_S5_U:s5_end], 1).astype(BF16),
                              w[:, s5_end:MAIN_W].astype(BF16)], axis=1)
    w_gates = w[:, SRC_GATES:].astype(BF16)
    w_lr = jnp.zeros((d, LANES), BF16).at[:, :GLA_RANK].set(w[:, SRC_GL_LR:SRC_GATES].astype(BF16))
    return w_main, w_gates, w_lr


def kernel(x, c, ada_w, ada_b, w_in, conv_w, w_conv_out, s5_lam_re, s5_lam_im, s5_b_re, s5_b_im, s5_c_re, s5_c_im, s5_d, s5_log_dt, w_s5_val, w_s5_gate, w_sb_out, gla_w_gate, gla_b_gate, gla_norm_g, w_gla_out, w_o, ln1_g, ln1_b, ffn_w_gate, ffn_w_up, ffn_w_down, ln2_g, ln2_b):
    bsz, seq, d = x.shape
    depth = ada_w.shape[0]
    alpha = (2.0 * depth) ** 0.25
    mod = _ada_mod(c, ada_w, ada_b)
    xf = x.reshape(bsz * seq, d)
    bf = lambda a: a.astype(BF16)
    t = _tiles(seq)
    for l in range(depth):
        sh1, sc1, gt1, sh2, sc2, gt2 = [m.reshape(bsz, 1, d) for m in jnp.split(mod[l], 6, axis=-1)]
        proj, lr = _inproj(xf, sc1, sh1, *_split_w_in(w_in[l]), seq, t["inproj_tm"], t["inproj_tn"])
        pre_conv = _short_conv(proj, conv_w[l], seq, t["conv_tc"])
        mats = _s5_matrices(s5_lam_re[l], s5_lam_im[l], s5_b_re[l], s5_b_im[l], s5_c_re[l],
                            s5_c_im[l], s5_d[l], s5_log_dt[l])
        pre_s5 = _s5_gelu(proj, mats, bsz, seq)
        pre_sb = _stick_breaking(proj, bsz, seq, t["sb_tq"])
        pre_gla = _gla(proj, lr, gla_w_gate[l], gla_b_gate[l], gla_norm_g[l], bsz, seq)
        xf = _merge(pre_conv, pre_s5, pre_sb, pre_gla, proj, xf, gt1, bf(w_conv_out[l]),
                    bf(_channel_major(w_s5_val[l], 0)), bf(_channel_major(w_s5_gate[l], 0)),
                    bf(w_sb_out[l]), bf(w_gla_out[l]), bf(w_o[l]),
                    ln1_g[l], ln1_b[l], seq, alpha, t["merge_tm"], t["merge_tn"])
        xf = _ffn(xf, sc2, sh2, gt2, bf(ffn_w_gate[l]), bf(ffn_w_up[l]), bf(ffn_w_down[l]),
                  ln2_g[l], ln2_b[l], seq, alpha, t["ffn_tm"], t["ffn_tf"])
    return xf.reshape(bsz, seq, d)
```

```python
import functools
import math

import jax
import jax.numpy as jnp
from jax import lax
from jax.experimental import pallas as pl
from jax.experimental.pallas import tpu as pltpu

F32 = jnp.float32
BF16 = jnp.bfloat16

LN_EPS = 1e-5
LANES = 128
MIX_W = 1024
S5_GROUP = 16
S5_STATE = 64
S5_CHUNK = 16
SB_HEADS = 8
SB_DH = 128
GLA_HEADS = 4
GLA_DK = 128
GLA_DV = 256
GLA_RANK = 16
GLA_TAU = 16.0
GLA_CHUNK = 128
GLA_SUB = 16
GLA_HEADS_PER_STEP = 4
VMEM_LIMIT = 52 * 1024 * 1024
SB_SKIP = 90.0

OFF_CV_B, OFF_CV_C, OFF_CV_X = 0, 1024, 2048
OFF_SB_Q, OFF_SB_K, OFF_SB_V = 3072, 4096, 5120
OFF_GL_Q, OFF_GL_K, OFF_GL_V, OFF_GL_R = 6144, 6656, 7168, 8192
OFF_S5_U = 9216
MAIN_W = 10240
OFF_G_CONV, OFF_G_S5, OFF_G_SB, OFF_G_GLA = 10240, 12288, 14336, 16384
PROJ_W = 18432
SRC_S5_U = 3072
SRC_GL_LR = 10240
SRC_GATES = 10256


def _tiles(seq):
    return dict(inproj_tm=min(1024, seq), inproj_tn=1024, conv_tc=min(512, seq), sb_tq=min(256, seq),
                merge_tm=min(512, seq), merge_tn=512, ffn_tm=min(512, seq), ffn_tf=512)


def _cparams(sem):
    return pltpu.CompilerParams(dimension_semantics=sem, vmem_limit_bytes=VMEM_LIMIT)


def _layer_norm_rows(x):
    mu = jnp.mean(x, axis=-1, keepdims=True)
    xc = x - mu
    var = jnp.mean(xc * xc, axis=-1, keepdims=True)
    return xc * lax.rsqrt(var + LN_EPS)


ROW_CHUNK = 256


def _for_row_chunks(n_rows, fn):
    chunk = min(ROW_CHUNK, n_rows)

    def body(r, _):
        fn(pl.ds(pl.multiple_of(r * chunk, chunk), chunk))
        return 0

    lax.fori_loop(0, n_rows // chunk, body, 0)


def _ln_modulate_into(h_ref, x_ref, sc_ref, sh_ref):
    def rows(sl):
        h = _layer_norm_rows(x_ref[sl, :]) * (1.0 + sc_ref[0]) + sh_ref[0]
        h_ref[sl, :] = h.astype(BF16)

    _for_row_chunks(x_ref.shape[0], rows)


def _residual_ln_into(o_ref, x_ref, y_ref, gt_ref, g_ref, b_ref, alpha):
    def rows(sl):
        z = alpha * x_ref[sl, :] + gt_ref[0] * y_ref[sl, :]
        o_ref[sl, :] = _layer_norm_rows(z) * g_ref[0] + b_ref[0]

    _for_row_chunks(x_ref.shape[0], rows)


def _sigmoid(x):
    return 1.0 / (1.0 + jnp.exp(-x))


def _log_sigmoid(x):
    return jnp.minimum(x, 0.0) - jnp.log(1.0 + jnp.exp(-jnp.abs(x)))


def _ada_kernel(c_ref, w_ref, b_ref, o_ref):
    c = c_ref[...]
    s = (c * _sigmoid(c)).astype(BF16)
    o_ref[0] = jnp.dot(s, w_ref[0].astype(BF16), preferred_element_type=F32) + b_ref[0]


def _ada_mod(c, ada_w, ada_b):
    depth, d, n = ada_w.shape
    bsz = c.shape[0]
    rows = 8
    c_pad = jnp.zeros((rows, d), F32).at[:bsz].set(c)
    tn = 1024
    out = pl.pallas_call(
        _ada_kernel,
        out_shape=jax.ShapeDtypeStruct((depth, rows, n), F32),
        grid=(depth, n // tn),
        in_specs=[
            pl.BlockSpec((rows, d), lambda l, j: (0, 0)),
            pl.BlockSpec((1, d, tn), lambda l, j: (l, 0, j)),
            pl.BlockSpec((1, 1, tn), lambda l, j: (l, 0, j)),
        ],
        out_specs=pl.BlockSpec((1, rows, tn), lambda l, j: (l, 0, j)),
        compiler_params=_cparams(("parallel", "parallel")),
        name="ada_mod",
    )(c_pad, ada_w, ada_b.reshape(depth, 1, n))
    return out[:, :bsz]


def _inproj_kernel(x_ref, sc_ref, sh_ref, wf_ref, wu_ref, wl_ref, o_ref, lr_ref, h_ref, *, n_direct):
    j = pl.program_id(1)

    @pl.when(j == 0)
    def _():
        _ln_modulate_into(h_ref, x_ref, sc_ref, sh_ref)
        lr_ref[...] = jnp.dot(h_ref[...], wl_ref[0], preferred_element_type=F32).astype(BF16)

    @pl.when(j < n_direct)
    def _():
        o_ref[...] = jnp.dot(h_ref[...], wf_ref[0], preferred_element_type=F32).astype(BF16)

    @pl.when(j >= n_direct)
    def _():
        o_ref[...] = jnp.dot(h_ref[...], wu_ref[0], preferred_element_type=F32).astype(BF16)


def _inproj(x, sc, sh, w_full, w_ug, w_lr, layer, seq, tm, tn):
    n_tok, d = x.shape
    n_a = SRC_S5_U // tn
    n_direct = (MAIN_W - MIX_W) // tn
    n_ug = w_ug.shape[2] // tn
    skip = MIX_W // tn
    per_seq = seq // tm
    mod = pl.BlockSpec((1, 1, d), lambda i, j: (i // per_seq, 0, 0))

    def direct_col(j):
        jj = jnp.minimum(j, n_direct - 1)
        return jnp.where(jj < n_a, jj, jj + skip)

    return pl.pallas_call(
        functools.partial(_inproj_kernel, n_direct=n_direct),
        out_shape=(jax.ShapeDtypeStruct((n_tok, (n_direct + n_ug) * tn), BF16),
                   jax.ShapeDtypeStruct((n_tok, LANES), BF16)),
        grid=(n_tok // tm, n_direct + n_ug),
        in_specs=[
            pl.BlockSpec((tm, d), lambda i, j: (i, 0)),
            mod, mod,
            pl.BlockSpec((1, d, tn), lambda i, j: (layer, 0, direct_col(j))),
            pl.BlockSpec((1, d, tn), lambda i, j: (layer, 0, jnp.maximum(j - n_direct, 0))),
            pl.BlockSpec((1, d, LANES), lambda i, j: (layer, 0, 0)),
        ],
        out_specs=(pl.BlockSpec((tm, tn), lambda i, j: (i, j)),
                   pl.BlockSpec((tm, LANES), lambda i, j: (i, 0))),
        scratch_shapes=[pltpu.VMEM((tm, d), BF16)],
        compiler_params=_cparams(("parallel", "arbitrary")),
        name="ln_mod_inproj",
    )(x, sc, sh, w_full, w_ug, w_lr)


def _conv_kernel(b_ref, c_ref, x_ref, cp_ref, xp_ref, w_ref, o_ref, *, per_seq):
    h = c_ref[...].astype(F32) * x_ref[...].astype(F32)
    hp = cp_ref[...].astype(F32) * xp_ref[...].astype(F32)
    first = (pl.program_id(0) % per_seq) == 0
    hp = jnp.where(first, 0.0, hp)
    row = lax.broadcasted_iota(jnp.int32, h.shape, 0)
    rows_p = hp.shape[0]
    h1 = jnp.where(row == 0, hp[rows_p - 1:rows_p], pltpu.roll(h, 1, 0))
    h2 = pltpu.roll(h, 2, 0)
    h2 = jnp.where(row == 0, hp[rows_p - 2:rows_p - 1], h2)
    h2 = jnp.where(row == 1, hp[rows_p - 1:rows_p], h2)
    w = w_ref[...]
    y = w[0:1] * h2 + w[1:2] * h1 + w[2:3] * h
    o_ref[...] = (b_ref[...].astype(F32) * y).astype(BF16)


def _short_conv(proj, conv_w, seq, tc):
    n_tok = proj.shape[0]
    halo = 16
    per_seq = seq // tc
    ratio = tc // halo

    def prev(i):
        return jnp.maximum(i * ratio - 1, 0)

    return pl.pallas_call(
        functools.partial(_conv_kernel, per_seq=per_seq),
        out_shape=jax.ShapeDtypeStruct((n_tok, MIX_W), BF16),
        grid=(n_tok // tc,),
        in_specs=[
            pl.BlockSpec((tc, MIX_W), lambda i: (i, OFF_CV_B // MIX_W)),
            pl.BlockSpec((tc, MIX_W), lambda i: (i, OFF_CV_C // MIX_W)),
            pl.BlockSpec((tc, MIX_W), lambda i: (i, OFF_CV_X // MIX_W)),
            pl.BlockSpec((halo, MIX_W), lambda i: (prev(i), OFF_CV_C // MIX_W)),
            pl.BlockSpec((halo, MIX_W), lambda i: (prev(i), OFF_CV_X // MIX_W)),
            pl.BlockSpec((8, MIX_W), lambda i: (0, 0)),
        ],
        out_specs=pl.BlockSpec((tc, MIX_W), lambda i: (i, 0)),
        compiler_params=_cparams(("parallel",)),
        name="short_conv",
    )(proj, proj, proj, proj, proj, jnp.zeros((8, MIX_W), F32).at[:conv_w.shape[0]].set(conv_w))


def _cmul(xr, xi, yr, yi):
    return xr * yr - xi * yi, xr * yi + xi * yr


def _complex_powers(ar, ai, exponent, shape):
    wr, wi = jnp.ones(shape, F32), jnp.zeros(shape, F32)
    for bit in range(4):
        on = ((exponent >> bit) & 1) == 1
        wr, wi = _cmul(wr, wi, jnp.where(on, ar, 1.0), jnp.where(on, ai, 0.0))
        ar, ai = _cmul(ar, ai, ar, ai)
    return wr, wi, ar, ai


def _s5_chunk_matrices(bt_re, bt_im, ct_re, ct_im, ac_re, ac_im, arow_re, arow_im):
    t, hh, p = S5_CHUNK, S5_GROUP, S5_STATE
    hi = lax.Precision.HIGHEST
    lane_tau = lax.broadcasted_iota(jnp.int32, (p, t * hh), 1) // hh
    wr, wi, _, _ = _complex_powers(ac_re, ac_im, lane_tau, (p, t * hh))
    cw_re, cw_im = _cmul(ct_re, ct_im, wr, wi)
    strip = (jnp.dot(bt_re, cw_re, precision=hi, preferred_element_type=F32)
             - jnp.dot(bt_im, cw_im, precision=hi, preferred_element_type=F32))
    lane = lax.broadcasted_iota(jnp.int32, (hh, t * hh), 1)
    m_intra = jnp.concatenate(
        [strip if tp == 0 else jnp.where(lane >= tp * hh, pltpu.roll(strip, tp * hh, 1), 0.0)
         for tp in range(t)], axis=0)
    mo_re, mo_im = _cmul(cw_re, cw_im, ac_re, ac_im)
    row_tau = (t - 1) - lax.broadcasted_iota(jnp.int32, (t, p), 0)
    rr, ri, a16_re, a16_im = _complex_powers(arow_re, arow_im, row_tau, (t, p))
    ms = [_cmul(bt_re, bt_im, rr[tp:tp + 1], ri[tp:tp + 1]) for tp in range(t)]
    ms_re = jnp.concatenate([m[0] for m in ms], axis=0)
    ms_im = jnp.concatenate([m[1] for m in ms], axis=0)
    return (m_intra.astype(BF16), ms_re.astype(BF16), ms_im.astype(BF16), mo_re.astype(BF16),
            (-mo_im).astype(BF16), a16_re, a16_im)


def _s5_kernel(u_ref, btr_ref, bti_ref, ctr_ref, cti_ref, acr_ref, aci_ref, arr_ref, ari_ref, d_ref,
               o_ref, *, n_chunks):
    u = u_ref[0]
    m_intra, ms_re, ms_im, mo_re, mo_im, ar, ai = _s5_chunk_matrices(
        btr_ref[0], bti_ref[0], ctr_ref[0], cti_ref[0], acr_ref[0], aci_ref[0], arr_ref[0], ari_ref[0])
    s_re = jnp.dot(u, ms_re, preferred_element_type=F32)
    s_im = jnp.dot(u, ms_im, preferred_element_type=F32)
    chunk = lax.broadcasted_iota(jnp.int32, s_re.shape, 0) % n_chunks

    def shifted(a, rows):
        return jnp.where(chunk >= rows, pltpu.roll(a, rows, 0), 0.0)

    step = 1
    while step < n_chunks:
        p_re, p_im = shifted(s_re, step), shifted(s_im, step)
        s_re, s_im = s_re + ar * p_re - ai * p_im, s_im + ar * p_im + ai * p_re
        ar, ai = ar * ar - ai * ai, 2.0 * ar * ai
        step *= 2
    prev_re, prev_im = shifted(s_re, 1), shifted(s_im, 1)
    y = jnp.dot(u, m_intra, preferred_element_type=F32)
    y += jnp.dot(prev_re.astype(BF16), mo_re, preferred_element_type=F32)
    y += jnp.dot(prev_im.astype(BF16), mo_im, preferred_element_type=F32)
    y += d_ref[0] * u.astype(F32)
    inner = math.sqrt(2.0 / math.pi) * (y + 0.044715 * (y * y * y))
    o_ref[0] = (0.5 * y * (1.0 + jnp.tanh(inner))).astype(BF16)


def _s5_params(lam_re, lam_im, b_re, b_im, c_re, c_im, d_skip, log_dt):
    g, p = lam_re.shape
    hh = S5_GROUP
    t = S5_CHUNK
    dt = jnp.exp(log_dt)[:, None]
    mag = jnp.exp(lam_re * dt)
    ang = lam_im * dt
    ab_re, ab_im = mag * jnp.cos(ang), mag * jnp.sin(ang)
    den = lam_re * lam_re + lam_im * lam_im
    nr, ni = ab_re - 1.0, ab_im
    f_re = (nr * lam_re + ni * lam_im) / den
    f_im = (ni * lam_re - nr * lam_im) / den
    bb_re = f_re[..., None] * b_re - f_im[..., None] * b_im
    bb_im = f_re[..., None] * b_im + f_im[..., None] * b_re
    bt_re, bt_im = bb_re.transpose(0, 2, 1), bb_im.transpose(0, 2, 1)
    tile = lambda c: jnp.tile(c.transpose(0, 2, 1), (1, 1, t))
    d_row = jnp.tile(d_skip.reshape(g, 1, hh), (1, 1, t))
    return (bt_re, bt_im, tile(c_re), tile(c_im), ab_re[:, :, None], ab_im[:, :, None],
            ab_re[:, None, :], ab_im[:, None, :], d_row)


def _s5_gelu(proj, params, bsz, seq):
    g = params[0].shape[0]
    t, hh, p = S5_CHUNK, S5_GROUP, S5_STATE
    n_chunks = seq // t
    rows = n_chunks * bsz
    u = proj[:, OFF_S5_U:OFF_S5_U + MIX_W].reshape(rows * t * hh, g)
    u = u.T.reshape(g, rows, t * hh)
    grp = lambda shape: pl.BlockSpec((1,) + shape, lambda i: (i, 0, 0))
    y = pl.pallas_call(
        functools.partial(_s5_kernel, n_chunks=n_chunks),
        out_shape=jax.ShapeDtypeStruct((g, rows, t * hh), BF16),
        grid=(g,),
        in_specs=[grp((rows, t * hh)), grp((hh, p)), grp((hh, p)), grp((p, t * hh)), grp((p, t * hh)),
                  grp((p, 1)), grp((p, 1)), grp((1, p)), grp((1, p)), grp((1, t * hh))],
        out_specs=grp((rows, t * hh)),
        compiler_params=_cparams(("parallel",)),
        name="s5_chunked",
    )(u, *params)
    return y.reshape(g, rows * t * hh).T.reshape(bsz * seq, hh * g)


def _channel_major(w, axis):
    g = w.shape[axis] // S5_GROUP
    shape = w.shape[:axis] + (g, S5_GROUP) + w.shape[axis + 1:]
    return jnp.swapaxes(w.reshape(shape), axis, axis + 1).reshape(w.shape)


def _sb_kernel(q_ref, k_ref, v_ref, o_ref, *, tq):
    i = pl.program_id(2)
    q = q_ref[...]
    scale = SB_DH ** -0.5
    row = lax.broadcasted_iota(jnp.int32, (tq, tq), 0)
    col = lax.broadcasted_iota(jnp.int32, (tq, tq), 1)
    below = row > col
    tri = jnp.where(below, 1.0, 0.0).astype(BF16)

    def logits(j, diag):
        rows = pl.ds(pl.multiple_of(j * tq, tq), tq)
        z = lax.dot_general(q, k_ref[rows, :], (((1,), (1,)), ((), ())),
                            preferred_element_type=F32) * scale
        lp = jnp.log(1.0 + jnp.exp(-jnp.abs(z)))
        log_beta = jnp.minimum(z, 0.0) - lp
        log_1m = log_beta - z
        if diag:
            log_1m = jnp.where(below, log_1m, 0.0)
        local = jnp.dot(log_1m.astype(BF16), tri, preferred_element_type=F32)
        return log_beta, local, jnp.sum(log_1m, axis=-1, keepdims=True), v_ref[rows, :]

    def weigh(parts, carry, diag):
        log_beta, local, total, vj = parts
        w = jnp.exp(log_beta + local + carry)
        if diag:
            w = jnp.where(below, w, 0.0)
        return jnp.dot(w.astype(BF16), vj, preferred_element_type=F32), carry + total

    has_prev = i > 0
    d_parts = logits(i, True)
    p_parts = logits(jnp.maximum(i - 1, 0), False)
    acc, carry = weigh(d_parts, jnp.zeros((tq, 1), F32), True)
    p_acc, p_carry = weigh(p_parts, carry, False)
    acc = jnp.where(has_prev, acc + p_acc, acc)
    carry = jnp.where(has_prev, p_carry, carry)

    def cond(state):
        j, top = state[0], state[1]
        return jnp.logical_and(j >= 0, top > -SB_SKIP)

    def body(state):
        j, _, carry, acc = state
        d_acc, carry = weigh(logits(j, False), carry, False)
        return j - 1, jnp.max(carry), carry, acc + d_acc

    state = lax.while_loop(cond, body, (i - 2, jnp.max(carry), carry, acc))
    o_ref[...] = state[3].astype(BF16)


def _stick_breaking(proj, bsz, seq, tq):
    n_tok = proj.shape[0]
    nq = seq // tq
    return pl.pallas_call(
        functools.partial(_sb_kernel, tq=tq),
        out_shape=jax.ShapeDtypeStruct((n_tok, MIX_W), BF16),
        grid=(bsz, SB_HEADS, nq),
        in_specs=[
            pl.BlockSpec((tq, SB_DH), lambda b, h, i: (b * nq + i, OFF_SB_Q // SB_DH + h)),
            pl.BlockSpec((seq, SB_DH), lambda b, h, i: (b, OFF_SB_K // SB_DH + h)),
            pl.BlockSpec((seq, SB_DH), lambda b, h, i: (b, OFF_SB_V // SB_DH + h)),
        ],
        out_specs=pl.BlockSpec((tq, SB_DH), lambda b, h, i: (b * nq + i, h)),
        compiler_params=_cparams(("parallel", "parallel", "arbitrary")),
        name="stick_breaking",
    )(proj, proj, proj)


def _split3(x):
    hi = x.astype(BF16)
    r1 = x - hi.astype(F32)
    mid = r1.astype(BF16)
    lo = (r1 - mid.astype(F32)).astype(BF16)
    return hi, mid, lo


def _gla_kernel(q_ref, k_ref, v_ref, r_ref, lr_ref, wg_ref, bg_ref, g_ref, o_ref, st_ref, sc_ref,
                *, n_chunks, n_heads):
    c_len, sub = GLA_CHUNK, GLA_SUB
    st_ref[...] = jnp.zeros_like(st_ref)
    row = lax.broadcasted_iota(jnp.int32, (c_len, c_len), 0)
    col = lax.broadcasted_iota(jnp.int32, (c_len, c_len), 1)
    tri_incl = jnp.where(col <= row, 1.0, 0.0).astype(BF16)
    col_s = lax.broadcasted_iota(jnp.int32, (sub, c_len), 1)
    row_s = lax.broadcasted_iota(jnp.int32, (sub, GLA_DK), 0)
    nt = (((1,), (1,)), ((), ()))

    def head_chunk(hd, rows):
        kcols = slice(hd * GLA_DK, (hd + 1) * GLA_DK)
        vcols = slice(hd * GLA_DV, (hd + 1) * GLA_DV)
        q = q_ref[rows, kcols].astype(F32) * (GLA_DK ** -0.5)
        k = k_ref[rows, kcols].astype(F32)
        v = v_ref[rows, vcols]
        pre = jnp.dot(lr_ref[rows, :], wg_ref[:, kcols], preferred_element_type=F32) + bg_ref[:, kcols]
        la = _log_sigmoid(pre) * (1.0 / GLA_TAU)
        b = sum(jnp.dot(tri_incl, part, preferred_element_type=F32) for part in _split3(la))
        bx = b - la
        b_end = b[c_len - 1:c_len]
        st = st_ref[hd]
        qe = (q * jnp.exp(b)).astype(BF16)
        o = lax.dot_general(qe, st.astype(BF16), nt, preferred_element_type=F32)
        for i in range(c_len // sub):
            lo_row = i * sub
            sl = slice(lo_row, lo_row + sub)
            bi, qi, ki = b[sl], q[sl], k[sl]
            ref_i = bx[lo_row:lo_row + 1]
            qd = (qi * jnp.exp(bi - ref_i)).astype(BF16)
            kd = (k * jnp.exp(jnp.minimum(ref_i - b, 0.0))).astype(BF16)
            sc = lax.dot_general(qd, kd, nt, preferred_element_type=F32)
            sc = jnp.where(col_s < lo_row, sc, 0.0)
            for s in range(sub):
                dec = jnp.exp(jnp.where(row_s >= s, bi - bi[s:s + 1], -1e30))
                cs = jnp.sum(qi * (ki[s:s + 1] * dec), axis=-1, keepdims=True)
                sc = jnp.where(col_s == lo_row + s, cs, sc)
            sc_ref[hd, sl, :] = sc
        o = o + jnp.dot(sc_ref[hd].astype(BF16), v, preferred_element_type=F32)
        ke = (k * jnp.exp(b_end - b)).astype(BF16)
        upd = lax.dot_general(v, ke, (((0,), (0,)), ((), ())), preferred_element_type=F32)
        st_ref[hd] = st * jnp.exp(b_end) + upd
        on = _layer_norm_rows(o) * g_ref[:, vcols]
        r = r_ref[rows, vcols].astype(F32)
        o_ref[rows, vcols] = (on * (r * _sigmoid(r))).astype(BF16)

    def chunk(c, _):
        rows = pl.ds(pl.multiple_of(c * c_len, c_len), c_len)
        for hd in range(n_heads):
            head_chunk(hd, rows)
        return 0

    lax.fori_loop(0, n_chunks, chunk, 0)


def _gla(proj, lr, w_gate, b_gate, norm_g, bsz, seq):
    n_tok = proj.shape[0]
    n_chunks = seq // GLA_CHUNK
    nh = GLA_HEADS_PER_STEP
    wk, wv = nh * GLA_DK, nh * GLA_DV
    wg = jnp.zeros((LANES, GLA_HEADS * GLA_DK), BF16).at[:GLA_RANK].set(w_gate.astype(BF16))
    return pl.pallas_call(
        functools.partial(_gla_kernel, n_chunks=n_chunks, n_heads=nh),
        out_shape=jax.ShapeDtypeStruct((n_tok, MIX_W), BF16),
        grid=(bsz, GLA_HEADS // nh),
        in_specs=[
            pl.BlockSpec((seq, wk), lambda b, h: (b, OFF_GL_Q // wk + h)),
            pl.BlockSpec((seq, wk), lambda b, h: (b, OFF_GL_K // wk + h)),
            pl.BlockSpec((seq, wv), lambda b, h: (b, OFF_GL_V // wv + h)),
            pl.BlockSpec((seq, wv), lambda b, h: (b, OFF_GL_R // wv + h)),
            pl.BlockSpec((seq, LANES), lambda b, h: (b, 0)),
            pl.BlockSpec((LANES, wk), lambda b, h: (0, h)),
            pl.BlockSpec((1, wk), lambda b, h: (0, h)),
            pl.BlockSpec((1, wv), lambda b, h: (0, h)),
        ],
        out_specs=pl.BlockSpec((seq, wv), lambda b, h: (b, h)),
        scratch_shapes=[pltpu.VMEM((nh, GLA_DV, GLA_DK), F32),
                        pltpu.VMEM((nh, GLA_CHUNK, GLA_CHUNK), F32)],
        compiler_params=_cparams(("parallel", "parallel")),
        name="gla",
    )(proj, proj, proj, proj, lr, wg, b_gate.reshape(1, -1), norm_g.reshape(1, -1))


def _merge_kernel(pc_ref, ps_ref, pb_ref, pg_ref, gc_ref, gs_ref, gb_ref, gg_ref, x_ref, gt_ref,
                  wc_ref, wv_ref, wt_ref, wb_ref, wl_ref, wo_ref, lg_ref, lb_ref, o_ref, acc_ref,
                  *, alpha):
    j = pl.program_id(1)

    @pl.when(j == 0)
    def _():
        acc_ref[...] = jnp.zeros_like(acc_ref)

    def mm(a_ref, w_ref):
        return jnp.dot(a_ref[...], w_ref[0], preferred_element_type=F32)

    def gate(ref):
        return _sigmoid(ref[...].astype(F32))

    s5 = ps_ref[...]
    y_s5 = jnp.dot(s5, wv_ref[0], preferred_element_type=F32) * _sigmoid(
        jnp.dot(s5, wt_ref[0], preferred_element_type=F32))
    merged = (gate(gc_ref) * mm(pc_ref, wc_ref) + gate(gs_ref) * y_s5
              + gate(gb_ref) * mm(pb_ref, wb_ref) + gate(gg_ref) * mm(pg_ref, wl_ref))
    acc_ref[...] += jnp.dot(merged.astype(BF16), wo_ref[0], preferred_element_type=F32)

    @pl.when(j == pl.num_programs(1) - 1)
    def _():
        _residual_ln_into(o_ref, x_ref, acc_ref, gt_ref, lg_ref, lb_ref, alpha)


def _merge(pre_conv, pre_s5, pre_sb, pre_gla, proj, x, gt, w_conv, w_val, w_gate, w_sb, w_gla, w_o,
           ln_g, ln_b, layer, seq, alpha, tm, tn):
    n_tok, d = x.shape
    per_seq = seq // tm
    pre = pl.BlockSpec((tm, MIX_W), lambda i, j: (i, 0))
    gcol = lambda off: pl.BlockSpec((tm, tn), lambda i, j: (i, off // tn + j))
    wcol = pl.BlockSpec((1, MIX_W, tn), lambda i, j: (layer, 0, j))
    vec = pl.BlockSpec((1, 1, d), lambda i, j: (layer, 0, 0))
    return pl.pallas_call(
        functools.partial(_merge_kernel, alpha=alpha),
        out_shape=jax.ShapeDtypeStruct((n_tok, d), F32),
        grid=(n_tok // tm, d // tn),
        in_specs=[pre, pre, pre, pre,
                  gcol(OFF_G_CONV), gcol(OFF_G_S5), gcol(OFF_G_SB), gcol(OFF_G_GLA),
                  pl.BlockSpec((tm, d), lambda i, j: (i, 0)),
                  pl.BlockSpec((1, 1, d), lambda i, j: (i // per_seq, 0, 0)),
                  wcol, wcol, wcol, wcol, wcol,
                  pl.BlockSpec((1, tn, d), lambda i, j: (layer, j, 0)),
                  vec, vec],
        out_specs=pl.BlockSpec((tm, d), lambda i, j: (i, 0)),
        scratch_shapes=[pltpu.VMEM((tm, d), F32)],
        compiler_params=_cparams(("parallel", "arbitrary")),
        name="merge_out_ln",
    )(pre_conv, pre_s5, pre_sb, pre_gla, proj, proj, proj, proj, x, gt,
      w_conv, w_val, w_gate, w_sb, w_gla, w_o, ln_g.reshape(-1, 1, d), ln_b.reshape(-1, 1, d))


def _ffn_kernel(x_ref, sc_ref, sh_ref, gt_ref, wg_ref, wu_ref, wd_ref, lg_ref, lb_ref, o_ref,
                h_ref, acc_ref, *, alpha):
    j = pl.program_id(1)

    @pl.when(j == 0)
    def _():
        _ln_modulate_into(h_ref, x_ref, sc_ref, sh_ref)
        acc_ref[...] = jnp.zeros_like(acc_ref)

    h = h_ref[...]
    gate = jnp.dot(h, wg_ref[0], preferred_element_type=F32)
    up = jnp.dot(h, wu_ref[0], preferred_element_type=F32)
    act = (gate * _sigmoid(gate) * up).astype(BF16)
    acc_ref[...] += jnp.dot(act, wd_ref[0], preferred_element_type=F32)

    @pl.when(j == pl.num_programs(1) - 1)
    def _():
        _residual_ln_into(o_ref, x_ref, acc_ref, gt_ref, lg_ref, lb_ref, alpha)


def _ffn(x, sc, sh, gt, w_gate, w_up, w_down, ln_g, ln_b, layer, seq, alpha, tm, tf):
    n_tok, d = x.shape
    d_ff = w_gate.shape[2]
    per_seq = seq // tm
    mod = pl.BlockSpec((1, 1, d), lambda i, j: (i // per_seq, 0, 0))
    vec = pl.BlockSpec((1, 1, d), lambda i, j: (layer, 0, 0))
    return pl.pallas_call(
        functools.partial(_ffn_kernel, alpha=alpha),
        out_shape=jax.ShapeDtypeStruct((n_tok, d), F32),
        grid=(n_tok // tm, d_ff // tf),
        in_specs=[pl.BlockSpec((tm, d), lambda i, j: (i, 0)), mod, mod, mod,
                  pl.BlockSpec((1, d, tf), lambda i, j: (layer, 0, j)),
                  pl.BlockSpec((1, d, tf), lambda i, j: (layer, 0, j)),
                  pl.BlockSpec((1, tf, d), lambda i, j: (layer, j, 0)),
                  vec, vec],
        out_specs=pl.BlockSpec((tm, d), lambda i, j: (i, 0)),
        scratch_shapes=[pltpu.VMEM((tm, d), BF16), pltpu.VMEM((tm, d), F32)],
        compiler_params=_cparams(("parallel", "arbitrary")),
        name="ffn_swiglu_ln",
    )(x, sc, sh, gt, w_gate, w_up, w_down, ln_g.reshape(-1, 1, d), ln_b.reshape(-1, 1, d))


def _split_w_in(w):
    depth, d, _ = w.shape
    wb = w.astype(BF16)
    w_ug = jnp.concatenate([_channel_major(wb[:, :, SRC_S5_U:SRC_S5_U + MIX_W], 2), wb[:, :, SRC_GATES:]],
                           axis=2)
    w_lr = jnp.zeros((depth, d, LANES), BF16).at[:, :, :GLA_RANK].set(wb[:, :, SRC_GL_LR:SRC_GATES])
    return wb, w_ug, w_lr


def kernel(x, c, ada_w, ada_b, w_in, conv_w, w_conv_out, s5_lam_re, s5_lam_im, s5_b_re, s5_b_im, s5_c_re, s5_c_im, s5_d, s5_log_dt, w_s5_val, w_s5_gate, w_sb_out, gla_w_gate, gla_b_gate, gla_norm_g, w_gla_out, w_o, ln1_g, ln1_b, ffn_w_gate, ffn_w_up, ffn_w_down, ln2_g, ln2_b):
    bsz, seq, d = x.shape
    depth = ada_w.shape[0]
    alpha = (2.0 * depth) ** 0.25
    mod = _ada_mod(c, ada_w, ada_b)
    xf = x.reshape(bsz * seq, d)
    bf = lambda a: a.astype(BF16)
    t = _tiles(seq)
    w_in_parts = _split_w_in(w_in)
    merge_w = (bf(w_conv_out), bf(_channel_major(w_s5_val, 1)), bf(_channel_major(w_s5_gate, 1)),
               bf(w_sb_out), bf(w_gla_out), bf(w_o))
    ffn_w = (bf(ffn_w_gate), bf(ffn_w_up), bf(ffn_w_down))
    for l in range(depth):
        sh1, sc1, gt1, sh2, sc2, gt2 = [m.reshape(bsz, 1, d) for m in jnp.split(mod[l], 6, axis=-1)]
        proj, lr = _inproj(xf, sc1, sh1, *w_in_parts, l, seq, t["inproj_tm"], t["inproj_tn"])
        pre_conv = _short_conv(proj, conv_w[l], seq, t["conv_tc"])
        s5_params = _s5_params(s5_lam_re[l], s5_lam_im[l], s5_b_re[l], s5_b_im[l], s5_c_re[l],
                               s5_c_im[l], s5_d[l], s5_log_dt[l])
        pre_s5 = _s5_gelu(proj, s5_params, bsz, seq)
        pre_sb = _stick_breaking(proj, bsz, seq, t["sb_tq"])
        pre_gla = _gla(proj, lr, gla_w_gate[l], gla_b_gate[l], gla_norm_g[l], bsz, seq)
        xf = _merge(pre_conv, pre_s5, pre_sb, pre_gla, proj, xf, gt1, *merge_w, ln1_g, ln1_b,
                    l, seq, alpha, t["merge_tm"], t["merge_tn"])
        xf = _ffn(xf, sc2, sh2, gt2, *ffn_w, ln2_g, ln2_b, l, seq, alpha, t["ffn_tm"], t["ffn_tf"])
    return xf.reshape(bsz, seq, d)
```

```python
import functools
import math

import jax
import jax.numpy as jnp
from jax import lax
from jax.experimental import pallas as pl
from jax.experimental.pallas import tpu as pltpu

F32 = jnp.float32
BF16 = jnp.bfloat16

LN_EPS = 1e-5
LANES = 128
MIX_W = 1024
S5_GROUP = 16
S5_STATE = 64
S5_CHUNK = 16
SB_HEADS = 8
SB_DH = 128
SB_HEADS_PER_STEP = 2
GLA_HEADS = 4
GLA_DK = 128
GLA_DV = 256
GLA_RANK = 16
GLA_TAU = 16.0
GLA_CHUNK = 128
GLA_SUB = 16
GLA_HEADS_PER_STEP = 4
VMEM_LIMIT = 52 * 1024 * 1024
SB_SKIP = 90.0

OFF_CV_B, OFF_CV_C, OFF_CV_X = 0, 1024, 2048
OFF_SB_Q, OFF_SB_K, OFF_SB_V = 3072, 4096, 5120
OFF_GL_Q, OFF_GL_K, OFF_GL_V, OFF_GL_R = 6144, 6656, 7168, 8192
OFF_S5_U = 9216
MAIN_W = 10240
OFF_G_CONV, OFF_G_S5, OFF_G_SB, OFF_G_GLA = 10240, 12288, 14336, 16384
PROJ_W = 18432
SRC_S5_U = 3072
SRC_GL_LR = 10240
SRC_GATES = 10256


def _tiles(seq):
    return dict(inproj_tm=min(1024, seq), inproj_tn=1024, conv_tc=min(512, seq), sb_tq=min(256, seq),
                merge_tm=min(1024, seq), merge_tn=512, out_tm=min(512, seq),
                ffn_tm=min(512, seq), ffn_tf=512)


def _cparams(sem):
    return pltpu.CompilerParams(dimension_semantics=sem, vmem_limit_bytes=VMEM_LIMIT)


def _layer_norm_rows(x):
    mu = jnp.mean(x, axis=-1, keepdims=True)
    xc = x - mu
    var = jnp.mean(xc * xc, axis=-1, keepdims=True)
    return xc * lax.rsqrt(var + LN_EPS)


ROW_CHUNK = 256


def _for_row_chunks(n_rows, fn):
    chunk = min(ROW_CHUNK, n_rows)

    def body(r, _):
        fn(pl.ds(pl.multiple_of(r * chunk, chunk), chunk))
        return 0

    lax.fori_loop(0, n_rows // chunk, body, 0)


def _ln_modulate_into(h_ref, x_ref, sc_ref, sh_ref):
    def rows(sl):
        h = _layer_norm_rows(x_ref[sl, :]) * (1.0 + sc_ref[0]) + sh_ref[0]
        h_ref[sl, :] = h.astype(BF16)

    _for_row_chunks(x_ref.shape[0], rows)


def _residual_ln_into(o_ref, x_ref, y_ref, gt_ref, g_ref, b_ref, alpha):
    def rows(sl):
        z = alpha * x_ref[sl, :] + gt_ref[0] * y_ref[sl, :]
        o_ref[sl, :] = _layer_norm_rows(z) * g_ref[0] + b_ref[0]

    _for_row_chunks(x_ref.shape[0], rows)


def _sigmoid(x):
    return 1.0 / (1.0 + jnp.exp(-x))


def _log_sigmoid(x):
    return jnp.minimum(x, 0.0) - jnp.log(1.0 + jnp.exp(-jnp.abs(x)))


def _ada_kernel(c_ref, w_ref, b_ref, o_ref):
    c = c_ref[...]
    s = (c * _sigmoid(c)).astype(BF16)
    o_ref[0] = jnp.dot(s, w_ref[0].astype(BF16), preferred_element_type=F32) + b_ref[0]


def _ada_mod(c, ada_w, ada_b):
    depth, d, n = ada_w.shape
    bsz = c.shape[0]
    rows = 8
    c_pad = jnp.zeros((rows, d), F32).at[:bsz].set(c)
    tn = 1024
    out = pl.pallas_call(
        _ada_kernel,
        out_shape=jax.ShapeDtypeStruct((depth, rows, n), F32),
        grid=(depth, n // tn),
        in_specs=[
            pl.BlockSpec((rows, d), lambda l, j: (0, 0)),
            pl.BlockSpec((1, d, tn), lambda l, j: (l, 0, j)),
            pl.BlockSpec((1, 1, tn), lambda l, j: (l, 0, j)),
        ],
        out_specs=pl.BlockSpec((1, rows, tn), lambda l, j: (l, 0, j)),
        compiler_params=_cparams(("parallel", "parallel")),
        name="ada_mod",
    )(c_pad, ada_w, ada_b.reshape(depth, 1, n))
    return out[:, :bsz]


def _inproj_kernel(x_ref, sc_ref, sh_ref, wf_ref, wu_ref, wl_ref, o_ref, lr_ref, h_ref, *, n_direct):
    j = pl.program_id(1)

    @pl.when(j == 0)
    def _():
        _ln_modulate_into(h_ref, x_ref, sc_ref, sh_ref)
        lr_ref[...] = jnp.dot(h_ref[...], wl_ref[0], preferred_element_type=F32).astype(BF16)

    @pl.when(j < n_direct)
    def _():
        o_ref[...] = jnp.dot(h_ref[...], wf_ref[0], preferred_element_type=F32).astype(BF16)

    @pl.when(j >= n_direct)
    def _():
        o_ref[...] = jnp.dot(h_ref[...], wu_ref[0], preferred_element_type=F32).astype(BF16)


def _inproj(x, sc, sh, w_full, w_ug, w_lr, layer, seq, tm, tn):
    n_tok, d = x.shape
    n_a = SRC_S5_U // tn
    n_direct = (MAIN_W - MIX_W) // tn
    n_ug = w_ug.shape[2] // tn
    skip = MIX_W // tn
    per_seq = seq // tm
    mod = pl.BlockSpec((1, 1, d), lambda i, j: (i // per_seq, 0, 0))

    def direct_col(j):
        jj = jnp.minimum(j, n_direct - 1)
        return jnp.where(jj < n_a, jj, jj + skip)

    return pl.pallas_call(
        functools.partial(_inproj_kernel, n_direct=n_direct),
        out_shape=(jax.ShapeDtypeStruct((n_tok, (n_direct + n_ug) * tn), BF16),
                   jax.ShapeDtypeStruct((n_tok, LANES), BF16)),
        grid=(n_tok // tm, n_direct + n_ug),
        in_specs=[
            pl.BlockSpec((tm, d), lambda i, j: (i, 0)),
            mod, mod,
            pl.BlockSpec((1, d, tn), lambda i, j: (layer, 0, direct_col(j))),
            pl.BlockSpec((1, d, tn), lambda i, j: (layer, 0, jnp.maximum(j - n_direct, 0))),
            pl.BlockSpec((1, d, LANES), lambda i, j: (layer, 0, 0)),
        ],
        out_specs=(pl.BlockSpec((tm, tn), lambda i, j: (i, j)),
                   pl.BlockSpec((tm, LANES), lambda i, j: (i, 0))),
        scratch_shapes=[pltpu.VMEM((tm, d), BF16)],
        compiler_params=_cparams(("parallel", "arbitrary")),
        name="ln_mod_inproj",
    )(x, sc, sh, w_full, w_ug, w_lr)


def _conv_kernel(b_ref, c_ref, x_ref, cp_ref, xp_ref, w_ref, o_ref, *, per_seq):
    h = c_ref[...].astype(F32) * x_ref[...].astype(F32)
    hp = cp_ref[...].astype(F32) * xp_ref[...].astype(F32)
    first = (pl.program_id(0) % per_seq) == 0
    hp = jnp.where(first, 0.0, hp)
    row = lax.broadcasted_iota(jnp.int32, h.shape, 0)
    rows_p = hp.shape[0]
    h1 = jnp.where(row == 0, hp[rows_p - 1:rows_p], pltpu.roll(h, 1, 0))
    h2 = pltpu.roll(h, 2, 0)
    h2 = jnp.where(row == 0, hp[rows_p - 2:rows_p - 1], h2)
    h2 = jnp.where(row == 1, hp[rows_p - 1:rows_p], h2)
    w = w_ref[...]
    y = w[0:1] * h2 + w[1:2] * h1 + w[2:3] * h
    o_ref[...] = (b_ref[...].astype(F32) * y).astype(BF16)


def _short_conv(proj, conv_w, seq, tc):
    n_tok = proj.shape[0]
    halo = 16
    per_seq = seq // tc
    ratio = tc // halo

    def prev(i):
        return jnp.maximum(i * ratio - 1, 0)

    return pl.pallas_call(
        functools.partial(_conv_kernel, per_seq=per_seq),
        out_shape=jax.ShapeDtypeStruct((n_tok, MIX_W), BF16),
        grid=(n_tok // tc,),
        in_specs=[
            pl.BlockSpec((tc, MIX_W), lambda i: (i, OFF_CV_B // MIX_W)),
            pl.BlockSpec((tc, MIX_W), lambda i: (i, OFF_CV_C // MIX_W)),
            pl.BlockSpec((tc, MIX_W), lambda i: (i, OFF_CV_X // MIX_W)),
            pl.BlockSpec((halo, MIX_W), lambda i: (prev(i), OFF_CV_C // MIX_W)),
            pl.BlockSpec((halo, MIX_W), lambda i: (prev(i), OFF_CV_X // MIX_W)),
            pl.BlockSpec((8, MIX_W), lambda i: (0, 0)),
        ],
        out_specs=pl.BlockSpec((tc, MIX_W), lambda i: (i, 0)),
        compiler_params=_cparams(("parallel",)),
        name="short_conv",
    )(proj, proj, proj, proj, proj, jnp.zeros((8, MIX_W), F32).at[:conv_w.shape[0]].set(conv_w))


def _cmul(xr, xi, yr, yi):
    return xr * yr - xi * yi, xr * yi + xi * yr


def _complex_powers(ar, ai, exponent, shape):
    wr, wi = jnp.ones(shape, F32), jnp.zeros(shape, F32)
    for bit in range(4):
        on = ((exponent >> bit) & 1) == 1
        wr, wi = _cmul(wr, wi, jnp.where(on, ar, 1.0), jnp.where(on, ai, 0.0))
        ar, ai = _cmul(ar, ai, ar, ai)
    return wr, wi, ar, ai


def _s5_chunk_matrices(bt_re, bt_im, ct_re, ct_im, ac_re, ac_im, arow_re, arow_im):
    t, hh, p = S5_CHUNK, S5_GROUP, S5_STATE
    hi = lax.Precision.HIGHEST
    lane_tau = lax.broadcasted_iota(jnp.int32, (p, t * hh), 1) // hh
    wr, wi, _, _ = _complex_powers(ac_re, ac_im, lane_tau, (p, t * hh))
    cw_re, cw_im = _cmul(ct_re, ct_im, wr, wi)
    strip = (jnp.dot(bt_re, cw_re, precision=hi, preferred_element_type=F32)
             - jnp.dot(bt_im, cw_im, precision=hi, preferred_element_type=F32))
    lane = lax.broadcasted_iota(jnp.int32, (hh, t * hh), 1)
    m_intra = jnp.concatenate(
        [strip if tp == 0 else jnp.where(lane >= tp * hh, pltpu.roll(strip, tp * hh, 1), 0.0)
         for tp in range(t)], axis=0)
    mo_re, mo_im = _cmul(cw_re, cw_im, ac_re, ac_im)
    row_tau = (t - 1) - lax.broadcasted_iota(jnp.int32, (t, p), 0)
    rr, ri, a16_re, a16_im = _complex_powers(arow_re, arow_im, row_tau, (t, p))
    ms = [_cmul(bt_re, bt_im, rr[tp:tp + 1], ri[tp:tp + 1]) for tp in range(t)]
    ms_re = jnp.concatenate([m[0] for m in ms], axis=0)
    ms_im = jnp.concatenate([m[1] for m in ms], axis=0)
    return (m_intra.astype(BF16), ms_re.astype(BF16), ms_im.astype(BF16), mo_re.astype(BF16),
            (-mo_im).astype(BF16), a16_re, a16_im)


def _s5_kernel(u_ref, btr_ref, bti_ref, ctr_ref, cti_ref, acr_ref, aci_ref, arr_ref, ari_ref, d_ref,
               o_ref, *, n_chunks):
    u = u_ref[0]
    m_intra, ms_re, ms_im, mo_re, mo_im, ar, ai = _s5_chunk_matrices(
        btr_ref[0], bti_ref[0], ctr_ref[0], cti_ref[0], acr_ref[0], aci_ref[0], arr_ref[0], ari_ref[0])
    s_re = jnp.dot(u, ms_re, preferred_element_type=F32)
    s_im = jnp.dot(u, ms_im, preferred_element_type=F32)
    chunk = lax.broadcasted_iota(jnp.int32, s_re.shape, 0) % n_chunks

    def shifted(a, rows):
        return jnp.where(chunk >= rows, pltpu.roll(a, rows, 0), 0.0)

    step = 1
    while step < n_chunks:
        p_re, p_im = shifted(s_re, step), shifted(s_im, step)
        s_re, s_im = s_re + ar * p_re - ai * p_im, s_im + ar * p_im + ai * p_re
        ar, ai = ar * ar - ai * ai, 2.0 * ar * ai
        step *= 2
    prev_re, prev_im = shifted(s_re, 1), shifted(s_im, 1)
    y = jnp.dot(u, m_intra, preferred_element_type=F32)
    y += jnp.dot(prev_re.astype(BF16), mo_re, preferred_element_type=F32)
    y += jnp.dot(prev_im.astype(BF16), mo_im, preferred_element_type=F32)
    y += d_ref[0] * u.astype(F32)
    inner = math.sqrt(2.0 / math.pi) * (y + 0.044715 * (y * y * y))
    o_ref[0] = (0.5 * y * (1.0 + jnp.tanh(inner))).astype(BF16)


def _s5_params(lam_re, lam_im, b_re, b_im, c_re, c_im, d_skip, log_dt):
    g, p = lam_re.shape
    hh = S5_GROUP
    t = S5_CHUNK
    dt = jnp.exp(log_dt)[:, None]
    mag = jnp.exp(lam_re * dt)
    ang = lam_im * dt
    ab_re, ab_im = mag * jnp.cos(ang), mag * jnp.sin(ang)
    den = lam_re * lam_re + lam_im * lam_im
    nr, ni = ab_re - 1.0, ab_im
    f_re = (nr * lam_re + ni * lam_im) / den
    f_im = (ni * lam_re - nr * lam_im) / den
    bb_re = f_re[..., None] * b_re - f_im[..., None] * b_im
    bb_im = f_re[..., None] * b_im + f_im[..., None] * b_re
    bt_re, bt_im = bb_re.transpose(0, 2, 1), bb_im.transpose(0, 2, 1)
    tile = lambda c: jnp.tile(c.transpose(0, 2, 1), (1, 1, t))
    d_row = jnp.tile(d_skip.reshape(g, 1, hh), (1, 1, t))
    return (bt_re, bt_im, tile(c_re), tile(c_im), ab_re[:, :, None], ab_im[:, :, None],
            ab_re[:, None, :], ab_im[:, None, :], d_row)


def _s5_gelu(proj, params, bsz, seq):
    g = params[0].shape[0]
    t, hh, p = S5_CHUNK, S5_GROUP, S5_STATE
    n_chunks = seq // t
    rows = n_chunks * bsz
    u = proj[:, OFF_S5_U:OFF_S5_U + MIX_W].reshape(rows * t * hh, g)
    u = u.T.reshape(g, rows, t * hh)
    grp = lambda shape: pl.BlockSpec((1,) + shape, lambda i: (i, 0, 0))
    y = pl.pallas_call(
        functools.partial(_s5_kernel, n_chunks=n_chunks),
        out_shape=jax.ShapeDtypeStruct((g, rows, t * hh), BF16),
        grid=(g,),
        in_specs=[grp((rows, t * hh)), grp((hh, p)), grp((hh, p)), grp((p, t * hh)), grp((p, t * hh)),
                  grp((p, 1)), grp((p, 1)), grp((1, p)), grp((1, p)), grp((1, t * hh))],
        out_specs=grp((rows, t * hh)),
        compiler_params=_cparams(("parallel",)),
        name="s5_chunked",
    )(u, *params)
    return y.reshape(g, rows * t * hh).T.reshape(bsz * seq, hh * g)


def _channel_major(w, axis):
    g = w.shape[axis] // S5_GROUP
    shape = w.shape[:axis] + (g, S5_GROUP) + w.shape[axis + 1:]
    return jnp.swapaxes(w.reshape(shape), axis, axis + 1).reshape(w.shape)


def _sb_kernel(q_ref, k_ref, v_ref, o_ref, *, tq, n_heads):
    i = pl.program_id(2)
    scale = SB_DH ** -0.5
    row = lax.broadcasted_iota(jnp.int32, (tq, tq), 0)
    col = lax.broadcasted_iota(jnp.int32, (tq, tq), 1)
    below = row > col
    tri = jnp.where(below, 1.0, 0.0).astype(BF16)

    def logits(cols, j, diag):
        rows = pl.ds(pl.multiple_of(j * tq, tq), tq)
        z = lax.dot_general(q_ref[:, cols], k_ref[rows, cols], (((1,), (1,)), ((), ())),
                            preferred_element_type=F32) * scale
        lp = jnp.log(1.0 + jnp.exp(-jnp.abs(z)))
        log_beta = jnp.minimum(z, 0.0) - lp
        log_1m = log_beta - z
        if diag:
            log_1m = jnp.where(below, log_1m, 0.0)
        local = jnp.dot(log_1m.astype(BF16), tri, preferred_element_type=F32)
        return log_beta, local, jnp.sum(log_1m, axis=-1, keepdims=True), v_ref[rows, cols]

    def weigh(parts, carry, diag):
        log_beta, local, total, vj = parts
        w = jnp.exp(log_beta + local + carry)
        if diag:
            w = jnp.where(below, w, 0.0)
        return jnp.dot(w.astype(BF16), vj, preferred_element_type=F32), carry + total

    has_prev = i > 0
    heads = [slice(hd * SB_DH, (hd + 1) * SB_DH) for hd in range(n_heads)]
    started = []
    for cols in heads:
        d_parts = logits(cols, i, True)
        p_parts = logits(cols, jnp.maximum(i - 1, 0), False)
        acc, carry = weigh(d_parts, jnp.zeros((tq, 1), F32), True)
        p_acc, p_carry = weigh(p_parts, carry, False)
        started.append((jnp.where(has_prev, p_carry, carry), jnp.where(has_prev, acc + p_acc, acc)))

    def cond(state):
        j, top = state[0], state[1]
        return jnp.logical_and(j >= 0, top > -SB_SKIP)

    for cols, (carry, acc) in zip(heads, started):
        def body(state, cols=cols):
            j, _, carry, acc = state
            d_acc, carry = weigh(logits(cols, j, False), carry, False)
            return j - 1, jnp.max(carry), carry, acc + d_acc

        state = lax.while_loop(cond, body, (i - 2, jnp.max(carry), carry, acc))
        o_ref[:, cols] = state[3].astype(BF16)


def _stick_breaking(proj, bsz, seq, tq):
    n_tok = proj.shape[0]
    nq = seq // tq
    nh = SB_HEADS_PER_STEP
    width = nh * SB_DH
    return pl.pallas_call(
        functools.partial(_sb_kernel, tq=tq, n_heads=nh),
        out_shape=jax.ShapeDtypeStruct((n_tok, MIX_W), BF16),
        grid=(bsz, SB_HEADS // nh, nq),
        in_specs=[
            pl.BlockSpec((tq, width), lambda b, h, i: (b * nq + i, OFF_SB_Q // width + h)),
            pl.BlockSpec((seq, width), lambda b, h, i: (b, OFF_SB_K // width + h)),
            pl.BlockSpec((seq, width), lambda b, h, i: (b, OFF_SB_V // width + h)),
        ],
        out_specs=pl.BlockSpec((tq, width), lambda b, h, i: (b * nq + i, h)),
        compiler_params=_cparams(("parallel", "parallel", "arbitrary")),
        name="stick_breaking",
    )(proj, proj, proj)


def _split3(x):
    hi = x.astype(BF16)
    r1 = x - hi.astype(F32)
    mid = r1.astype(BF16)
    lo = (r1 - mid.astype(F32)).astype(BF16)
    return hi, mid, lo


def _gla_kernel(q_ref, k_ref, v_ref, r_ref, lr_ref, wg_ref, bg_ref, g_ref, o_ref, st_ref, sc_ref, kd_ref,
                *, n_chunks, n_heads):
    c_len, sub = GLA_CHUNK, GLA_SUB
    st_ref[...] = jnp.zeros_like(st_ref)
    row = lax.broadcasted_iota(jnp.int32, (c_len, c_len), 0)
    col = lax.broadcasted_iota(jnp.int32, (c_len, c_len), 1)
    tri_incl = jnp.where(col <= row, 1.0, 0.0).astype(BF16)
    col_s = lax.broadcasted_iota(jnp.int32, (sub, c_len), 1)
    row_s = lax.broadcasted_iota(jnp.int32, (sub, GLA_DK), 0)
    nt = (((1,), (1,)), ((), ()))

    def head_chunk(hd, rows):
        kcols = slice(hd * GLA_DK, (hd + 1) * GLA_DK)
        vcols = slice(hd * GLA_DV, (hd + 1) * GLA_DV)
        q = q_ref[rows, kcols].astype(F32) * (GLA_DK ** -0.5)
        k = k_ref[rows, kcols].astype(F32)
        v = v_ref[rows, vcols]
        pre = jnp.dot(lr_ref[rows, :], wg_ref[:, kcols], preferred_element_type=F32) + bg_ref[:, kcols]
        la = _log_sigmoid(pre) * (1.0 / GLA_TAU)
        b = sum(jnp.dot(tri_incl, part, preferred_element_type=F32) for part in _split3(la))
        bx = b - la
        b_end = b[c_len - 1:c_len]
        st = st_ref[hd]
        qe = (q * jnp.exp(b)).astype(BF16)
        o = lax.dot_general(qe, st.astype(BF16), nt, preferred_element_type=F32)
        kd_ref[hd] = jnp.zeros((c_len, GLA_DK), F32)
        for i in range(c_len // sub):
            lo_row = i * sub
            sl = slice(lo_row, lo_row + sub)
            bi, qi, ki = b[sl], q[sl], k[sl]
            ref_i = bx[lo_row:lo_row + 1]
            if i == 0:
                sc = jnp.zeros((sub, c_len), F32)
            else:
                pl_row = lo_row - sub
                ps = slice(pl_row, lo_row)
                if pl_row:
                    kd_ref[hd, :pl_row, :] = kd_ref[hd, :pl_row, :] * jnp.exp(ref_i - bx[pl_row:pl_row + 1])
                kd_ref[hd, ps, :] = k[ps] * jnp.exp(ref_i - b[ps])
                qd = (qi * jnp.exp(bi - ref_i)).astype(BF16)
                sc = lax.dot_general(qd, kd_ref[hd].astype(BF16), nt, preferred_element_type=F32)
            for s in range(sub):
                dec = jnp.exp(jnp.where(row_s >= s, bi - bi[s:s + 1], -1e30))
                cs = jnp.sum(qi * (ki[s:s + 1] * dec), axis=-1, keepdims=True)
                sc = jnp.where(col_s == lo_row + s, cs, sc)
            sc_ref[hd, sl, :] = sc
        o = o + jnp.dot(sc_ref[hd].astype(BF16), v, preferred_element_type=F32)
        ke = (k * jnp.exp(b_end - b)).astype(BF16)
        upd = lax.dot_general(v, ke, (((0,), (0,)), ((), ())), preferred_element_type=F32)
        st_ref[hd] = st * jnp.exp(b_end) + upd
        on = _layer_norm_rows(o) * g_ref[:, vcols]
        r = r_ref[rows, vcols].astype(F32)
        o_ref[rows, vcols] = (on * (r * _sigmoid(r))).astype(BF16)

    def chunk(c, _):
        rows = pl.ds(pl.multiple_of(c * c_len, c_len), c_len)
        for hd in range(n_heads):
            head_chunk(hd, rows)
        return 0

    lax.fori_loop(0, n_chunks, chunk, 0)


def _gla(proj, lr, w_gate, b_gate, norm_g, bsz, seq):
    n_tok = proj.shape[0]
    n_chunks = seq // GLA_CHUNK
    nh = GLA_HEADS_PER_STEP
    wk, wv = nh * GLA_DK, nh * GLA_DV
    wg = jnp.zeros((LANES, GLA_HEADS * GLA_DK), BF16).at[:GLA_RANK].set(w_gate.astype(BF16))
    return pl.pallas_call(
        functools.partial(_gla_kernel, n_chunks=n_chunks, n_heads=nh),
        out_shape=jax.ShapeDtypeStruct((n_tok, MIX_W), BF16),
        grid=(bsz, GLA_HEADS // nh),
        in_specs=[
            pl.BlockSpec((seq, wk), lambda b, h: (b, OFF_GL_Q // wk + h)),
            pl.BlockSpec((seq, wk), lambda b, h: (b, OFF_GL_K // wk + h)),
            pl.BlockSpec((seq, wv), lambda b, h: (b, OFF_GL_V // wv + h)),
            pl.BlockSpec((seq, wv), lambda b, h: (b, OFF_GL_R // wv + h)),
            pl.BlockSpec((seq, LANES), lambda b, h: (b, 0)),
            pl.BlockSpec((LANES, wk), lambda b, h: (0, h)),
            pl.BlockSpec((1, wk), lambda b, h: (0, h)),
            pl.BlockSpec((1, wv), lambda b, h: (0, h)),
        ],
        out_specs=pl.BlockSpec((seq, wv), lambda b, h: (b, h)),
        scratch_shapes=[pltpu.VMEM((nh, GLA_DV, GLA_DK), F32),
                        pltpu.VMEM((nh, GLA_CHUNK, GLA_CHUNK), F32),
                        pltpu.VMEM((nh, GLA_CHUNK, GLA_DK), F32)],
        compiler_params=_cparams(("parallel", "parallel")),
        name="gla",
    )(proj, proj, proj, proj, lr, wg, b_gate.reshape(1, -1), norm_g.reshape(1, -1))


def _merge_kernel(pc_ref, ps_ref, pb_ref, pg_ref, gc_ref, gs_ref, gb_ref, gg_ref,
                  wc_ref, wv_ref, wt_ref, wb_ref, wl_ref, o_ref):
    def mm(a_ref, w_ref):
        return jnp.dot(a_ref[...], w_ref[0], preferred_element_type=F32)

    def gate(ref):
        return _sigmoid(ref[...].astype(F32))

    y_s5 = mm(ps_ref, wv_ref) * _sigmoid(mm(ps_ref, wt_ref))
    merged = (gate(gc_ref) * mm(pc_ref, wc_ref) + gate(gs_ref) * y_s5
              + gate(gb_ref) * mm(pb_ref, wb_ref) + gate(gg_ref) * mm(pg_ref, wl_ref))
    o_ref[...] = merged.astype(BF16)


def _merge(pre_conv, pre_s5, pre_sb, pre_gla, proj, w_conv, w_val, w_gate, w_sb, w_gla, layer, tm, tn):
    n_tok = proj.shape[0]
    d = w_conv.shape[2]
    pre = pl.BlockSpec((tm, MIX_W), lambda i, j: (i, 0))
    gcol = lambda off: pl.BlockSpec((tm, tn), lambda i, j: (i, off // tn + j))
    wcol = pl.BlockSpec((1, MIX_W, tn), lambda i, j: (layer, 0, j))
    return pl.pallas_call(
        _merge_kernel,
        out_shape=jax.ShapeDtypeStruct((n_tok, d), BF16),
        grid=(n_tok // tm, d // tn),
        in_specs=[pre, pre, pre, pre,
                  gcol(OFF_G_CONV), gcol(OFF_G_S5), gcol(OFF_G_SB), gcol(OFF_G_GLA),
                  wcol, wcol, wcol, wcol, wcol],
        out_specs=pl.BlockSpec((tm, tn), lambda i, j: (i, j)),
        compiler_params=_cparams(("parallel", "arbitrary")),
        name="branch_merge",
    )(pre_conv, pre_s5, pre_sb, pre_gla, proj, proj, proj, proj, w_conv, w_val, w_gate, w_sb, w_gla)


def _out_proj_kernel(m_ref, x_ref, gt_ref, wo_ref, lg_ref, lb_ref, o_ref, y_ref, *, alpha):
    y_ref[...] = jnp.dot(m_ref[...], wo_ref[0], preferred_element_type=F32)
    _residual_ln_into(o_ref, x_ref, y_ref, gt_ref, lg_ref, lb_ref, alpha)


def _out_proj_ln(merged, x, gt, w_o, ln_g, ln_b, layer, seq, alpha, tm):
    n_tok, d = x.shape
    per_seq = seq // tm
    vec = pl.BlockSpec((1, 1, d), lambda i: (layer, 0, 0))
    return pl.pallas_call(
        functools.partial(_out_proj_kernel, alpha=alpha),
        out_shape=jax.ShapeDtypeStruct((n_tok, d), F32),
        grid=(n_tok // tm,),
        in_specs=[pl.BlockSpec((tm, d), lambda i: (i, 0)),
                  pl.BlockSpec((tm, d), lambda i: (i, 0)),
                  pl.BlockSpec((1, 1, d), lambda i: (i // per_seq, 0, 0)),
                  pl.BlockSpec((1, d, d), lambda i: (layer, 0, 0)),
                  vec, vec],
        out_specs=pl.BlockSpec((tm, d), lambda i: (i, 0)),
        scratch_shapes=[pltpu.VMEM((tm, d), F32)],
        compiler_params=_cparams(("parallel",)),
        name="out_proj_ln",
    )(merged, x, gt, w_o, ln_g.reshape(-1, 1, d), ln_b.reshape(-1, 1, d))


def _ffn_kernel(x_ref, sc_ref, sh_ref, gt_ref, wg_ref, wu_ref, wd_ref, lg_ref, lb_ref, o_ref,
                h_ref, acc_ref, *, alpha):
    j = pl.program_id(1)

    @pl.when(j == 0)
    def _():
        _ln_modulate_into(h_ref, x_ref, sc_ref, sh_ref)
        acc_ref[...] = jnp.zeros_like(acc_ref)

    h = h_ref[...]
    gate = jnp.dot(h, wg_ref[0], preferred_element_type=F32)
    up = jnp.dot(h, wu_ref[0], preferred_element_type=F32)
    act = (gate * _sigmoid(gate) * up).astype(BF16)
    acc_ref[...] += jnp.dot(act, wd_ref[0], preferred_element_type=F32)

    @pl.when(j == pl.num_programs(1) - 1)
    def _():
        _residual_ln_into(o_ref, x_ref, acc_ref, gt_ref, lg_ref, lb_ref, alpha)


def _ffn(x, sc, sh, gt, w_gate, w_up, w_down, ln_g, ln_b, layer, seq, alpha, tm, tf):
    n_tok, d = x.shape
    d_ff = w_gate.shape[2]
    per_seq = seq // tm
    mod = pl.BlockSpec((1, 1, d), lambda i, j: (i // per_seq, 0, 0))
    vec = pl.BlockSpec((1, 1, d), lambda i, j: (layer, 0, 0))
    return pl.pallas_call(
        functools.partial(_ffn_kernel, alpha=alpha),
        out_shape=jax.ShapeDtypeStruct((n_tok, d), F32),
        grid=(n_tok // tm, d_ff // tf),
        in_specs=[pl.BlockSpec((tm, d), lambda i, j: (i, 0)), mod, mod, mod,
                  pl.BlockSpec((1, d, tf), lambda i, j: (layer, 0, j)),
                  pl.BlockSpec((1, d, tf), lambda i, j: (layer, 0, j)),
                  pl.BlockSpec((1, tf, d), lambda i, j: (layer, j, 0)),
                  vec, vec],
        out_specs=pl.BlockSpec((tm, d), lambda i, j: (i, 0)),
        scratch_shapes=[pltpu.VMEM((tm, d), BF16), pltpu.VMEM((tm, d), F32)],
        compiler_params=_cparams(("parallel", "arbitrary")),
        name="ffn_swiglu_ln",
    )(x, sc, sh, gt, w_gate, w_up, w_down, ln_g.reshape(-1, 1, d), ln_b.reshape(-1, 1, d))


def _split_w_in(w):
    depth, d, _ = w.shape
    wb = w.astype(BF16)
    w_ug = jnp.concatenate([_channel_major(wb[:, :, SRC_S5_U:SRC_S5_U + MIX_W], 2), wb[:, :, SRC_GATES:]],
                           axis=2)
    w_lr = jnp.zeros((depth, d, LANES), BF16).at[:, :, :GLA_RANK].set(wb[:, :, SRC_GL_LR:SRC_GATES])
    return wb, w_ug, w_lr


def kernel(x, c, ada_w, ada_b, w_in, conv_w, w_conv_out, s5_lam_re, s5_lam_im, s5_b_re, s5_b_im, s5_c_re, s5_c_im, s5_d, s5_log_dt, w_s5_val, w_s5_gate, w_sb_out, gla_w_gate, gla_b_gate, gla_norm_g, w_gla_out, w_o, ln1_g, ln1_b, ffn_w_gate, ffn_w_up, ffn_w_down, ln2_g, ln2_b):
    bsz, seq, d = x.shape
    depth = ada_w.shape[0]
    alpha = (2.0 * depth) ** 0.25
    mod = _ada_mod(c, ada_w, ada_b)
    xf = x.reshape(bsz * seq, d)
    bf = lambda a: a.astype(BF16)
    t = _tiles(seq)
    w_in_parts = _split_w_in(w_in)
    merge_w = (bf(w_conv_out), bf(_channel_major(w_s5_val, 1)), bf(_channel_major(w_s5_gate, 1)),
               bf(w_sb_out), bf(w_gla_out))
    w_o_bf = bf(w_o)
    ffn_w = (bf(ffn_w_gate), bf(ffn_w_up), bf(ffn_w_down))
    for l in range(depth):
        sh1, sc1, gt1, sh2, sc2, gt2 = [m.reshape(bsz, 1, d) for m in jnp.split(mod[l], 6, axis=-1)]
        proj, lr = _inproj(xf, sc1, sh1, *w_in_parts, l, seq, t["inproj_tm"], t["inproj_tn"])
        pre_conv = _short_conv(proj, conv_w[l], seq, t["conv_tc"])
        s5_params = _s5_params(s5_lam_re[l], s5_lam_im[l], s5_b_re[l], s5_b_im[l], s5_c_re[l],
                               s5_c_im[l], s5_d[l], s5_log_dt[l])
        pre_s5 = _s5_gelu(proj, s5_params, bsz, seq)
        pre_sb = _stick_breaking(proj, bsz, seq, t["sb_tq"])
        pre_gla = _gla(proj, lr, gla_w_gate[l], gla_b_gate[l], gla_norm_g[l], bsz, seq)
        merged = _merge(pre_conv, pre_s5, pre_sb, pre_gla, proj, *merge_w, l, t["merge_tm"], t["merge_tn"])
        xf = _out_proj_ln(merged, xf, gt1, w_o_bf, ln1_g, ln1_b, l, seq, alpha, t["out_tm"])
        xf = _ffn(xf, sc2, sh2, gt2, *ffn_w, ln2_g, ln2_b, l, seq, alpha, t["ffn_tm"], t["ffn_tf"])
    return xf.reshape(bsz, seq, d)
```

```python
import functools
import math

import jax
import jax.numpy as jnp
from jax import lax
from jax.experimental import pallas as pl
from jax.experimental.pallas import tpu as pltpu

F32 = jnp.float32
BF16 = jnp.bfloat16

LN_EPS = 1e-5
LANES = 128
MIX_W = 1024
S5_GROUP = 16
S5_STATE = 64
S5_CHUNK = 16
SB_HEADS = 8
SB_DH = 128
SB_HEADS_PER_STEP = 2
GLA_HEADS = 4
GLA_DK = 128
GLA_DV = 256
GLA_RANK = 16
GLA_TAU = 16.0
GLA_CHUNK = 128
GLA_SUB = 16
GLA_HEADS_PER_STEP = 4
VMEM_LIMIT = 52 * 1024 * 1024
SB_SKIP = 90.0

OFF_CV_B, OFF_CV_C, OFF_CV_X = 0, 1024, 2048
OFF_SB_Q, OFF_SB_K, OFF_SB_V = 3072, 4096, 5120
OFF_GL_Q, OFF_GL_K, OFF_GL_V, OFF_GL_R = 6144, 6656, 7168, 8192
OFF_S5_U = 9216
MAIN_W = 10240
OFF_G_CONV, OFF_G_S5, OFF_G_SB, OFF_G_GLA = 10240, 12288, 14336, 16384
PROJ_W = 18432
SRC_S5_U = 3072
SRC_GL_LR = 10240
SRC_GATES = 10256


def _tiles(seq):
    return dict(prep_rt=1024, prep_tn=1024, inproj_tm=min(1024, seq), inproj_tn=2048, conv_tc=min(512, seq),
                sb_tq=min(256, seq),
                merge_tm=min(1024, seq), merge_tn=512, out_tm=min(512, seq),
                ffn_tm=min(512, seq), ffn_tf=512)


def _cparams(sem):
    return pltpu.CompilerParams(dimension_semantics=sem, vmem_limit_bytes=VMEM_LIMIT)


def _layer_norm_rows(x):
    mu = jnp.mean(x, axis=-1, keepdims=True)
    xc = x - mu
    var = jnp.mean(xc * xc, axis=-1, keepdims=True)
    return xc * lax.rsqrt(var + LN_EPS)


ROW_CHUNK = 256


def _for_row_chunks(n_rows, fn):
    chunk = min(ROW_CHUNK, n_rows)

    def body(r, _):
        fn(pl.ds(pl.multiple_of(r * chunk, chunk), chunk))
        return 0

    lax.fori_loop(0, n_rows // chunk, body, 0)


def _ln_modulate_into(h_ref, x_ref, sc_ref, sh_ref):
    def rows(sl):
        h = _layer_norm_rows(x_ref[sl, :]) * (1.0 + sc_ref[0]) + sh_ref[0]
        h_ref[sl, :] = h.astype(BF16)

    _for_row_chunks(x_ref.shape[0], rows)


def _residual_ln_into(o_ref, x_ref, y_ref, gt_ref, g_ref, b_ref, alpha):
    def rows(sl):
        z = alpha * x_ref[sl, :] + gt_ref[0] * y_ref[sl, :]
        o_ref[sl, :] = _layer_norm_rows(z) * g_ref[0] + b_ref[0]

    _for_row_chunks(x_ref.shape[0], rows)


def _sigmoid(x):
    return 1.0 / (1.0 + jnp.exp(-x))


def _log_sigmoid(x):
    return jnp.minimum(x, 0.0) - jnp.log(1.0 + jnp.exp(-jnp.abs(x)))


def _ada_kernel(c_ref, w_ref, b_ref, o_ref):
    c = c_ref[...]
    s = (c * _sigmoid(c)).astype(BF16)
    o_ref[0] = jnp.dot(s, w_ref[0].astype(BF16), preferred_element_type=F32) + b_ref[0]


def _ada_mod(c, ada_w, ada_b):
    depth, d, n = ada_w.shape
    bsz = c.shape[0]
    rows = 8
    c_pad = jnp.zeros((rows, d), F32).at[:bsz].set(c)
    tn = 1024
    out = pl.pallas_call(
        _ada_kernel,
        out_shape=jax.ShapeDtypeStruct((depth, rows, n), F32),
        grid=(depth, n // tn),
        in_specs=[
            pl.BlockSpec((rows, d), lambda l, j: (0, 0)),
            pl.BlockSpec((1, d, tn), lambda l, j: (l, 0, j)),
            pl.BlockSpec((1, 1, tn), lambda l, j: (l, 0, j)),
        ],
        out_specs=pl.BlockSpec((1, rows, tn), lambda l, j: (l, 0, j)),
        compiler_params=_cparams(("parallel", "parallel")),
        name="ada_mod",
    )(c_pad, ada_w, ada_b.reshape(depth, 1, n))
    return out[:, :bsz]


def _w_in_prep_kernel(a_ref, b_ref, l_ref, p_ref, o_ref, lr_ref, *, n_a, n_direct, shift):
    j = pl.program_id(2)

    @pl.when(j == 0)
    def _():
        lane = lax.broadcasted_iota(jnp.int32, l_ref.shape[1:], 1)
        lr_ref[0] = jnp.where(lane < GLA_RANK, l_ref[0], 0.0).astype(BF16)

    @pl.when(j < n_direct)
    def _():
        o_ref[0] = a_ref[0].astype(BF16)

    @pl.when(j == n_direct)
    def _():
        o_ref[0] = jnp.dot(a_ref[0].astype(BF16), p_ref[...], preferred_element_type=F32).astype(BF16)

    @pl.when(j > n_direct)
    def _():
        both = jnp.concatenate([a_ref[0], b_ref[0]], axis=1)
        o_ref[0] = both[:, shift:shift + o_ref.shape[2]].astype(BF16)


def _w_in_prep(w_in, rt, tn):
    depth, d, _ = w_in.shape
    n_a = SRC_S5_U // tn
    n_direct = (MAIN_W - MIX_W) // tn
    n_gate = (PROJ_W - MAIN_W) // tn
    skip = MIX_W // tn
    gate0 = SRC_GL_LR // tn
    shift = SRC_GATES - SRC_GL_LR
    per = tn // LANES
    src = jnp.arange(MIX_W)
    perm = ((src[:, None] % S5_GROUP) * (MIX_W // S5_GROUP) + src[:, None] // S5_GROUP
            == src[None, :]).astype(BF16)

    def src_col(j):
        direct = jnp.where(j < n_a, j, j + skip)
        return jnp.where(j < n_direct, direct, jnp.where(j == n_direct, n_a, gate0 + j - n_direct - 1))

    return pl.pallas_call(
        functools.partial(_w_in_prep_kernel, n_a=n_a, n_direct=n_direct, shift=shift),
        out_shape=(jax.ShapeDtypeStruct((depth, d, PROJ_W), BF16),
                   jax.ShapeDtypeStruct((depth, d, LANES), BF16)),
        grid=(depth, d // rt, n_direct + 1 + n_gate),
        in_specs=[
            pl.BlockSpec((1, rt, tn), lambda l, r, j: (l, r, src_col(j))),
            pl.BlockSpec((1, rt, LANES),
                         lambda l, r, j: (l, r, (gate0 + jnp.maximum(j - n_direct, 1)) * per)),
            pl.BlockSpec((1, rt, LANES), lambda l, r, j: (l, r, gate0 * per)),
            pl.BlockSpec((MIX_W, MIX_W), lambda l, r, j: (0, 0)),
        ],
        out_specs=(pl.BlockSpec((1, rt, tn), lambda l, r, j: (l, r, j)),
                   pl.BlockSpec((1, rt, LANES), lambda l, r, j: (l, r, 0))),
        compiler_params=_cparams(("parallel", "parallel", "arbitrary")),
        name="w_in_prep",
    )(w_in, w_in, w_in, perm)


def _inproj_kernel(x_ref, sc_ref, sh_ref, w_ref, wl_ref, o_ref, lr_ref, h_ref):
    @pl.when(pl.program_id(1) == 0)
    def _():
        _ln_modulate_into(h_ref, x_ref, sc_ref, sh_ref)
        lr_ref[...] = jnp.dot(h_ref[...], wl_ref[0], preferred_element_type=F32).astype(BF16)

    o_ref[...] = jnp.dot(h_ref[...], w_ref[0], preferred_element_type=F32).astype(BF16)


def _inproj(x, sc, sh, w_all, w_lr, layer, seq, tm, tn):
    n_tok, d = x.shape
    width = w_all.shape[2]
    per_seq = seq // tm
    mod = pl.BlockSpec((1, 1, d), lambda i, j: (i // per_seq, 0, 0))
    return pl.pallas_call(
        _inproj_kernel,
        out_shape=(jax.ShapeDtypeStruct((n_tok, width), BF16),
                   jax.ShapeDtypeStruct((n_tok, LANES), BF16)),
        grid=(n_tok // tm, width // tn),
        in_specs=[
            pl.BlockSpec((tm, d), lambda i, j: (i, 0)),
            mod, mod,
            pl.BlockSpec((1, d, tn), lambda i, j: (layer, 0, j)),
            pl.BlockSpec((1, d, LANES), lambda i, j: (layer, 0, 0)),
        ],
        out_specs=(pl.BlockSpec((tm, tn), lambda i, j: (i, j)),
                   pl.BlockSpec((tm, LANES), lambda i, j: (i, 0))),
        scratch_shapes=[pltpu.VMEM((tm, d), BF16)],
        compiler_params=_cparams(("parallel", "arbitrary")),
        name="ln_mod_inproj",
    )(x, sc, sh, w_all, w_lr)


def _conv_kernel(b_ref, c_ref, x_ref, cp_ref, xp_ref, w_ref, o_ref, *, per_seq):
    h = c_ref[...].astype(F32) * x_ref[...].astype(F32)
    hp = cp_ref[...].astype(F32) * xp_ref[...].astype(F32)
    first = (pl.program_id(0) % per_seq) == 0
    hp = jnp.where(first, 0.0, hp)
    row = lax.broadcasted_iota(jnp.int32, h.shape, 0)
    rows_p = hp.shape[0]
    h1 = jnp.where(row == 0, hp[rows_p - 1:rows_p], pltpu.roll(h, 1, 0))
    h2 = pltpu.roll(h, 2, 0)
    h2 = jnp.where(row == 0, hp[rows_p - 2:rows_p - 1], h2)
    h2 = jnp.where(row == 1, hp[rows_p - 1:rows_p], h2)
    w = w_ref[...]
    y = w[0:1] * h2 + w[1:2] * h1 + w[2:3] * h
    o_ref[...] = (b_ref[...].astype(F32) * y).astype(BF16)


def _short_conv(proj, conv_w, seq, tc):
    n_tok = proj.shape[0]
    halo = 16
    per_seq = seq // tc
    ratio = tc // halo

    def prev(i):
        return jnp.maximum(i * ratio - 1, 0)

    return pl.pallas_call(
        functools.partial(_conv_kernel, per_seq=per_seq),
        out_shape=jax.ShapeDtypeStruct((n_tok, MIX_W), BF16),
        grid=(n_tok // tc,),
        in_specs=[
            pl.BlockSpec((tc, MIX_W), lambda i: (i, OFF_CV_B // MIX_W)),
            pl.BlockSpec((tc, MIX_W), lambda i: (i, OFF_CV_C // MIX_W)),
            pl.BlockSpec((tc, MIX_W), lambda i: (i, OFF_CV_X // MIX_W)),
            pl.BlockSpec((halo, MIX_W), lambda i: (prev(i), OFF_CV_C // MIX_W)),
            pl.BlockSpec((halo, MIX_W), lambda i: (prev(i), OFF_CV_X // MIX_W)),
            pl.BlockSpec((8, MIX_W), lambda i: (0, 0)),
        ],
        out_specs=pl.BlockSpec((tc, MIX_W), lambda i: (i, 0)),
        compiler_params=_cparams(("parallel",)),
        name="short_conv",
    )(proj, proj, proj, proj, proj, jnp.zeros((8, MIX_W), F32).at[:conv_w.shape[0]].set(conv_w))


def _cmul(xr, xi, yr, yi):
    return xr * yr - xi * yi, xr * yi + xi * yr


def _complex_powers(ar, ai, exponent, shape):
    wr, wi = jnp.ones(shape, F32), jnp.zeros(shape, F32)
    for bit in range(4):
        on = ((exponent >> bit) & 1) == 1
        wr, wi = _cmul(wr, wi, jnp.where(on, ar, 1.0), jnp.where(on, ai, 0.0))
        ar, ai = _cmul(ar, ai, ar, ai)
    return wr, wi, ar, ai


def _s5_chunk_matrices(bt_re, bt_im, ct_re, ct_im, ac_re, ac_im, arow_re, arow_im):
    t, hh, p = S5_CHUNK, S5_GROUP, S5_STATE
    hi = lax.Precision.HIGHEST
    lane_tau = lax.broadcasted_iota(jnp.int32, (p, t * hh), 1) // hh
    wr, wi, _, _ = _complex_powers(ac_re, ac_im, lane_tau, (p, t * hh))
    cw_re, cw_im = _cmul(ct_re, ct_im, wr, wi)
    strip = (jnp.dot(bt_re, cw_re, precision=hi, preferred_element_type=F32)
             - jnp.dot(bt_im, cw_im, precision=hi, preferred_element_type=F32))
    lane = lax.broadcasted_iota(jnp.int32, (hh, t * hh), 1)
    m_intra = jnp.concatenate(
        [strip if tp == 0 else jnp.where(lane >= tp * hh, pltpu.roll(strip, tp * hh, 1), 0.0)
         for tp in range(t)], axis=0)
    mo_re, mo_im = _cmul(cw_re, cw_im, ac_re, ac_im)
    both = lambda a: jnp.concatenate([a, a], axis=1)
    low = lax.broadcasted_iota(jnp.int32, (t, 2 * p), 1) < p
    row_tau = (t - 1) - lax.broadcasted_iota(jnp.int32, (t, 2 * p), 0)
    rr, ri, a16_re, a16_im = _complex_powers(both(arow_re), both(arow_im), row_tau, (t, 2 * p))
    r_same, r_cross = jnp.where(low, rr, ri), jnp.where(low, -ri, rr)
    b_re2, b_im2 = both(bt_re), both(bt_im)
    ms = jnp.concatenate([b_re2 * r_same[tp:tp + 1] + b_im2 * r_cross[tp:tp + 1] for tp in range(t)],
                         axis=0)
    mo = jnp.concatenate([mo_re, -mo_im], axis=0)
    return m_intra.astype(BF16), ms.astype(BF16), mo.astype(BF16), a16_re, a16_im


def _s5_kernel(u_ref, btr_ref, bti_ref, ctr_ref, cti_ref, acr_ref, aci_ref, arr_ref, ari_ref, d_ref,
               o_ref, *, n_chunks):
    u = u_ref[0]
    m_intra, ms, mo, ar, ai = _s5_chunk_matrices(
        btr_ref[0], bti_ref[0], ctr_ref[0], cti_ref[0], acr_ref[0], aci_ref[0], arr_ref[0], ari_ref[0])
    s = jnp.dot(u, ms, preferred_element_type=F32)
    chunk = lax.broadcasted_iota(jnp.int32, s.shape, 0) % n_chunks
    low = lax.broadcasted_iota(jnp.int32, ar.shape, 1) < S5_STATE

    def shifted(a, rows):
        return jnp.where(chunk >= rows, pltpu.roll(a, rows, 0), 0.0)

    step = 1
    while step < n_chunks:
        prev = shifted(s, step)
        s = s + ar * prev + jnp.where(low, -ai, ai) * pltpu.roll(prev, S5_STATE, 1)
        ar, ai = ar * ar - ai * ai, 2.0 * ar * ai
        step *= 2
    y = jnp.dot(u, m_intra, preferred_element_type=F32)
    y += jnp.dot(shifted(s, 1).astype(BF16), mo, preferred_element_type=F32)
    y += d_ref[0] * u.astype(F32)
    inner = math.sqrt(2.0 / math.pi) * (y + 0.044715 * (y * y * y))
    o_ref[0] = (0.5 * y * (1.0 + jnp.tanh(inner))).astype(BF16)


def _s5_params(lam_re, lam_im, b_re, b_im, c_re, c_im, d_skip, log_dt):
    g, p = lam_re.shape
    hh = S5_GROUP
    t = S5_CHUNK
    dt = jnp.exp(log_dt)[:, None]
    mag = jnp.exp(lam_re * dt)
    ang = lam_im * dt
    ab_re, ab_im = mag * jnp.cos(ang), mag * jnp.sin(ang)
    den = lam_re * lam_re + lam_im * lam_im
    nr, ni = ab_re - 1.0, ab_im
    f_re = (nr * lam_re + ni * lam_im) / den
    f_im = (ni * lam_re - nr * lam_im) / den
    bb_re = f_re[..., None] * b_re - f_im[..., None] * b_im
    bb_im = f_re[..., None] * b_im + f_im[..., None] * b_re
    bt_re, bt_im = bb_re.transpose(0, 2, 1), bb_im.transpose(0, 2, 1)
    tile = lambda c: jnp.tile(c.transpose(0, 2, 1), (1, 1, t))
    d_row = jnp.tile(d_skip.reshape(g, 1, hh), (1, 1, t))
    return (bt_re, bt_im, tile(c_re), tile(c_im), ab_re[:, :, None], ab_im[:, :, None],
            ab_re[:, None, :], ab_im[:, None, :], d_row)


def _s5_gelu(proj, params, bsz, seq):
    g = params[0].shape[0]
    t, hh, p = S5_CHUNK, S5_GROUP, S5_STATE
    n_chunks = seq // t
    rows = n_chunks * bsz
    u = proj[:, OFF_S5_U:OFF_S5_U + MIX_W].reshape(rows * t * hh, g)
    u = u.T.reshape(g, rows, t * hh)
    grp = lambda shape: pl.BlockSpec((1,) + shape, lambda i: (i, 0, 0))
    y = pl.pallas_call(
        functools.partial(_s5_kernel, n_chunks=n_chunks),
        out_shape=jax.ShapeDtypeStruct((g, rows, t * hh), BF16),
        grid=(g,),
        in_specs=[grp((rows, t * hh)), grp((hh, p)), grp((hh, p)), grp((p, t * hh)), grp((p, t * hh)),
                  grp((p, 1)), grp((p, 1)), grp((1, p)), grp((1, p)), grp((1, t * hh))],
        out_specs=grp((rows, t * hh)),
        compiler_params=_cparams(("parallel",)),
        name="s5_chunked",
    )(u, *params)
    return y.reshape(g, rows * t * hh).T.reshape(bsz * seq, hh * g)


def _channel_major(w, axis):
    g = w.shape[axis] // S5_GROUP
    shape = w.shape[:axis] + (g, S5_GROUP) + w.shape[axis + 1:]
    return jnp.swapaxes(w.reshape(shape), axis, axis + 1).reshape(w.shape)


def _sb_kernel(q_ref, k_ref, v_ref, o_ref, *, tq, n_heads):
    i = pl.program_id(2)
    scale = SB_DH ** -0.5
    row = lax.broadcasted_iota(jnp.int32, (tq, tq), 0)
    col = lax.broadcasted_iota(jnp.int32, (tq, tq), 1)
    below = row > col
    tri = jnp.where(below, 1.0, 0.0).astype(BF16)

    def logits(cols, j, diag):
        rows = pl.ds(pl.multiple_of(j * tq, tq), tq)
        z = lax.dot_general(q_ref[:, cols], k_ref[rows, cols], (((1,), (1,)), ((), ())),
                            preferred_element_type=F32) * scale
        lp = jnp.log(1.0 + jnp.exp(-jnp.abs(z)))
        log_beta = jnp.minimum(z, 0.0) - lp
        log_1m = log_beta - z
        if diag:
            log_1m = jnp.where(below, log_1m, 0.0)
        local = jnp.dot(log_1m.astype(BF16), tri, preferred_element_type=F32)
        return log_beta, local, jnp.sum(log_1m, axis=-1, keepdims=True), v_ref[rows, cols]

    def weigh(parts, carry, diag):
        log_beta, local, total, vj = parts
        w = jnp.exp(log_beta + local + carry)
        if diag:
            w = jnp.where(below, w, 0.0)
        return jnp.dot(w.astype(BF16), vj, preferred_element_type=F32), carry + total

    has_prev = i > 0
    heads = [slice(hd * SB_DH, (hd + 1) * SB_DH) for hd in range(n_heads)]
    started = []
    for cols in heads:
        d_parts = logits(cols, i, True)
        p_parts = logits(cols, jnp.maximum(i - 1, 0), False)
        acc, carry = weigh(d_parts, jnp.zeros((tq, 1), F32), True)
        p_acc, p_carry = weigh(p_parts, carry, False)
        started.append((jnp.where(has_prev, p_carry, carry), jnp.where(has_prev, acc + p_acc, acc)))

    def cond(state):
        j, top = state[0], state[1]
        return jnp.logical_and(j >= 0, top > -SB_SKIP)

    for cols, (carry, acc) in zip(heads, started):
        def body(state, cols=cols):
            j, _, carry, acc = state
            d_acc, carry = weigh(logits(cols, j, False), carry, False)
            return j - 1, jnp.max(carry), carry, acc + d_acc

        state = lax.while_loop(cond, body, (i - 2, jnp.max(carry), carry, acc))
        o_ref[:, cols] = state[3].astype(BF16)


def _stick_breaking(proj, bsz, seq, tq):
    n_tok = proj.shape[0]
    nq = seq // tq
    nh = SB_HEADS_PER_STEP
    width = nh * SB_DH
    return pl.pallas_call(
        functools.partial(_sb_kernel, tq=tq, n_heads=nh),
        out_shape=jax.ShapeDtypeStruct((n_tok, MIX_W), BF16),
        grid=(bsz, SB_HEADS // nh, nq),
        in_specs=[
            pl.BlockSpec((tq, width), lambda b, h, i: (b * nq + i, OFF_SB_Q // width + h)),
            pl.BlockSpec((seq, width), lambda b, h, i: (b, OFF_SB_K // width + h)),
            pl.BlockSpec((seq, width), lambda b, h, i: (b, OFF_SB_V // width + h)),
        ],
        out_specs=pl.BlockSpec((tq, width), lambda b, h, i: (b * nq + i, h)),
        compiler_params=_cparams(("parallel", "parallel", "arbitrary")),
        name="stick_breaking",
    )(proj, proj, proj)


def _split3(x):
    hi = x.astype(BF16)
    r1 = x - hi.astype(F32)
    mid = r1.astype(BF16)
    lo = (r1 - mid.astype(F32)).astype(BF16)
    return hi, mid, lo


def _gla_kernel(q_ref, k_ref, v_ref, r_ref, lr_ref, wg_ref, bg_ref, g_ref, o_ref, st_ref, sc_ref, kd_ref,
                *, n_chunks, n_heads):
    c_len, sub = GLA_CHUNK, GLA_SUB
    st_ref[...] = jnp.zeros_like(st_ref)
    row = lax.broadcasted_iota(jnp.int32, (c_len, c_len), 0)
    col = lax.broadcasted_iota(jnp.int32, (c_len, c_len), 1)
    tri_incl = jnp.where(col <= row, 1.0, 0.0).astype(BF16)
    col_s = lax.broadcasted_iota(jnp.int32, (sub, c_len), 1)
    row_s = lax.broadcasted_iota(jnp.int32, (sub, GLA_DK), 0)
    nt = (((1,), (1,)), ((), ()))

    def head_chunk(hd, rows):
        kcols = slice(hd * GLA_DK, (hd + 1) * GLA_DK)
        vcols = slice(hd * GLA_DV, (hd + 1) * GLA_DV)
        q = q_ref[rows, kcols].astype(F32) * (GLA_DK ** -0.5)
        k = k_ref[rows, kcols].astype(F32)
        v = v_ref[rows, vcols]
        pre = jnp.dot(lr_ref[rows, :], wg_ref[:, kcols], preferred_element_type=F32) + bg_ref[:, kcols]
        la = _log_sigmoid(pre) * (1.0 / GLA_TAU)
        b = sum(jnp.dot(tri_incl, part, preferred_element_type=F32) for part in _split3(la))
        bx = b - la
        b_end = b[c_len - 1:c_len]
        st = st_ref[hd]
        qe = (q * jnp.exp(b)).astype(BF16)
        o = lax.dot_general(qe, st.astype(BF16), nt, preferred_element_type=F32)
        kd_ref[hd] = jnp.zeros((c_len, GLA_DK), F32)
        for i in range(c_len // sub):
            lo_row = i * sub
            sl = slice(lo_row, lo_row + sub)
            bi, qi, ki = b[sl], q[sl], k[sl]
            ref_i = bx[lo_row:lo_row + 1]
            if i == 0:
                sc = jnp.zeros((sub, c_len), F32)
            else:
                pl_row = lo_row - sub
                ps = slice(pl_row, lo_row)
                if pl_row:
                    kd_ref[hd, :pl_row, :] = kd_ref[hd, :pl_row, :] * jnp.exp(ref_i - bx[pl_row:pl_row + 1])
                kd_ref[hd, ps, :] = k[ps] * jnp.exp(ref_i - b[ps])
                qd = (qi * jnp.exp(bi - ref_i)).astype(BF16)
                sc = lax.dot_general(qd, kd_ref[hd].astype(BF16), nt, preferred_element_type=F32)
            for s in range(sub):
                dec = jnp.exp(jnp.where(row_s >= s, bi - bi[s:s + 1], -1e30))
                cs = jnp.sum(qi * (ki[s:s + 1] * dec), axis=-1, keepdims=True)
                sc = jnp.where(col_s == lo_row + s, cs, sc)
            sc_ref[hd, sl, :] = sc
        o = o + jnp.dot(sc_ref[hd].astype(BF16), v, preferred_element_type=F32)
        ke = (k * jnp.exp(b_end - b)).astype(BF16)
        upd = lax.dot_general(v, ke, (((0,), (0,)), ((), ())), preferred_element_type=F32)
        st_ref[hd] = st * jnp.exp(b_end) + upd
        on = _layer_norm_rows(o) * g_ref[:, vcols]
        r = r_ref[rows, vcols].astype(F32)
        o_ref[rows, vcols] = (on * (r * _sigmoid(r))).astype(BF16)

    def chunk(c, _):
        rows = pl.ds(pl.multiple_of(c * c_len, c_len), c_len)
        for hd in range(n_heads):
            head_chunk(hd, rows)
        return 0

    lax.fori_loop(0, n_chunks, chunk, 0)


def _gla(proj, lr, w_gate, b_gate, norm_g, bsz, seq):
    n_tok = proj.shape[0]
    n_chunks = seq // GLA_CHUNK
    nh = GLA_HEADS_PER_STEP
    wk, wv = nh * GLA_DK, nh * GLA_DV
    wg = jnp.zeros((LANES, GLA_HEADS * GLA_DK), BF16).at[:GLA_RANK].set(w_gate.astype(BF16))
    return pl.pallas_call(
        functools.partial(_gla_kernel, n_chunks=n_chunks, n_heads=nh),
        out_shape=jax.ShapeDtypeStruct((n_tok, MIX_W), BF16),
        grid=(bsz, GLA_HEADS // nh),
        in_specs=[
            pl.BlockSpec((seq, wk), lambda b, h: (b, OFF_GL_Q // wk + h)),
            pl.BlockSpec((seq, wk), lambda b, h: (b, OFF_GL_K // wk + h)),
            pl.BlockSpec((seq, wv), lambda b, h: (b, OFF_GL_V // wv + h)),
            pl.BlockSpec((seq, wv), lambda b, h: (b, OFF_GL_R // wv + h)),
            pl.BlockSpec((seq, LANES), lambda b, h: (b, 0)),
            pl.BlockSpec((LANES, wk), lambda b, h: (0, h)),
            pl.BlockSpec((1, wk), lambda b, h: (0, h)),
            pl.BlockSpec((1, wv), lambda b, h: (0, h)),
        ],
        out_specs=pl.BlockSpec((seq, wv), lambda b, h: (b, h)),
        scratch_shapes=[pltpu.VMEM((nh, GLA_DV, GLA_DK), F32),
                        pltpu.VMEM((nh, GLA_CHUNK, GLA_CHUNK), F32),
                        pltpu.VMEM((nh, GLA_CHUNK, GLA_DK), F32)],
        compiler_params=_cparams(("parallel", "parallel")),
        name="gla",
    )(proj, proj, proj, proj, lr, wg, b_gate.reshape(1, -1), norm_g.reshape(1, -1))


def _merge_kernel(pc_ref, ps_ref, pb_ref, pg_ref, gc_ref, gs_ref, gb_ref, gg_ref,
                  wc_ref, wv_ref, wt_ref, wb_ref, wl_ref, o_ref):
    def mm(a_ref, w_ref):
        return jnp.dot(a_ref[...], w_ref[0], preferred_element_type=F32)

    def gate(ref):
        return _sigmoid(ref[...].astype(F32))

    y_s5 = mm(ps_ref, wv_ref) * _sigmoid(mm(ps_ref, wt_ref))
    merged = (gate(gc_ref) * mm(pc_ref, wc_ref) + gate(gs_ref) * y_s5
              + gate(gb_ref) * mm(pb_ref, wb_ref) + gate(gg_ref) * mm(pg_ref, wl_ref))
    o_ref[...] = merged.astype(BF16)


def _merge(pre_conv, pre_s5, pre_sb, pre_gla, proj, w_conv, w_val, w_gate, w_sb, w_gla, layer, tm, tn):
    n_tok = proj.shape[0]
    d = w_conv.shape[2]
    pre = pl.BlockSpec((tm, MIX_W), lambda i, j: (i, 0))
    gcol = lambda off: pl.BlockSpec((tm, tn), lambda i, j: (i, off // tn + j))
    wcol = pl.BlockSpec((1, MIX_W, tn), lambda i, j: (layer, 0, j))
    return pl.pallas_call(
        _merge_kernel,
        out_shape=jax.ShapeDtypeStruct((n_tok, d), BF16),
        grid=(n_tok // tm, d // tn),
        in_specs=[pre, pre, pre, pre,
                  gcol(OFF_G_CONV), gcol(OFF_G_S5), gcol(OFF_G_SB), gcol(OFF_G_GLA),
                  wcol, wcol, wcol, wcol, wcol],
        out_specs=pl.BlockSpec((tm, tn), lambda i, j: (i, j)),
        compiler_params=_cparams(("parallel", "arbitrary")),
        name="branch_merge",
    )(pre_conv, pre_s5, pre_sb, pre_gla, proj, proj, proj, proj, w_conv, w_val, w_gate, w_sb, w_gla)


def _out_proj_kernel(m_ref, x_ref, gt_ref, wo_ref, lg_ref, lb_ref, o_ref, y_ref, *, alpha):
    y_ref[...] = jnp.dot(m_ref[...], wo_ref[0], preferred_element_type=F32)
    _residual_ln_into(o_ref, x_ref, y_ref, gt_ref, lg_ref, lb_ref, alpha)


def _out_proj_ln(merged, x, gt, w_o, ln_g, ln_b, layer, seq, alpha, tm):
    n_tok, d = x.shape
    per_seq = seq // tm
    vec = pl.BlockSpec((1, 1, d), lambda i: (layer, 0, 0))
    return pl.pallas_call(
        functools.partial(_out_proj_kernel, alpha=alpha),
        out_shape=jax.ShapeDtypeStruct((n_tok, d), F32),
        grid=(n_tok // tm,),
        in_specs=[pl.BlockSpec((tm, d), lambda i: (i, 0)),
                  pl.BlockSpec((tm, d), lambda i: (i, 0)),
                  pl.BlockSpec((1, 1, d), lambda i: (i // per_seq, 0, 0)),
                  pl.BlockSpec((1, d, d), lambda i: (layer, 0, 0)),
                  vec, vec],
        out_specs=pl.BlockSpec((tm, d), lambda i: (i, 0)),
        scratch_shapes=[pltpu.VMEM((tm, d), F32)],
        compiler_params=_cparams(("parallel",)),
        name="out_proj_ln",
    )(merged, x, gt, w_o, ln_g.reshape(-1, 1, d), ln_b.reshape(-1, 1, d))


def _ffn_kernel(x_ref, sc_ref, sh_ref, gt_ref, wg_ref, wu_ref, wd_ref, lg_ref, lb_ref, o_ref,
                h_ref, acc_ref, *, alpha):
    j = pl.program_id(1)

    @pl.when(j == 0)
    def _():
        _ln_modulate_into(h_ref, x_ref, sc_ref, sh_ref)
        acc_ref[...] = jnp.zeros_like(acc_ref)

    h = h_ref[...]
    gate = jnp.dot(h, wg_ref[0], preferred_element_type=F32)
    up = jnp.dot(h, wu_ref[0], preferred_element_type=F32)
    act = (gate * _sigmoid(gate) * up).astype(BF16)
    acc_ref[...] += jnp.dot(act, wd_ref[0], preferred_element_type=F32)

    @pl.when(j == pl.num_programs(1) - 1)
    def _():
        _residual_ln_into(o_ref, x_ref, acc_ref, gt_ref, lg_ref, lb_ref, alpha)


def _ffn(x, sc, sh, gt, w_gate, w_up, w_down, ln_g, ln_b, layer, seq, alpha, tm, tf):
    n_tok, d = x.shape
    d_ff = w_gate.shape[2]
    per_seq = seq // tm
    mod = pl.BlockSpec((1, 1, d), lambda i, j: (i // per_seq, 0, 0))
    vec = pl.BlockSpec((1, 1, d), lambda i, j: (layer, 0, 0))
    return pl.pallas_call(
        functools.partial(_ffn_kernel, alpha=alpha),
        out_shape=jax.ShapeDtypeStruct((n_tok, d), F32),
        grid=(n_tok // tm, d_ff // tf),
        in_specs=[pl.BlockSpec((tm, d), lambda i, j: (i, 0)), mod, mod, mod,
                  pl.BlockSpec((1, d, tf), lambda i, j: (layer, 0, j)),
                  pl.BlockSpec((1, d, tf), lambda i, j: (layer, 0, j)),
                  pl.BlockSpec((1, tf, d), lambda i, j: (layer, j, 0)),
                  vec, vec],
        out_specs=pl.BlockSpec((tm, d), lambda i, j: (i, 0)),
        scratch_shapes=[pltpu.VMEM((tm, d), BF16), pltpu.VMEM((tm, d), F32)],
        compiler_params=_cparams(("parallel", "arbitrary")),
        name="ffn_swiglu_ln",
    )(x, sc, sh, gt, w_gate, w_up, w_down, ln_g.reshape(-1, 1, d), ln_b.reshape(-1, 1, d))


def kernel(x, c, ada_w, ada_b, w_in, conv_w, w_conv_out, s5_lam_re, s5_lam_im, s5_b_re, s5_b_im, s5_c_re, s5_c_im, s5_d, s5_log_dt, w_s5_val, w_s5_gate, w_sb_out, gla_w_gate, gla_b_gate, gla_norm_g, w_gla_out, w_o, ln1_g, ln1_b, ffn_w_gate, ffn_w_up, ffn_w_down, ln2_g, ln2_b):
    bsz, seq, d = x.shape
    depth = ada_w.shape[0]
    alpha = (2.0 * depth) ** 0.25
    mod = _ada_mod(c, ada_w, ada_b)
    xf = x.reshape(bsz * seq, d)
    bf = lambda a: a.astype(BF16)
    t = _tiles(seq)
    w_in_parts = _w_in_prep(w_in, t["prep_rt"], t["prep_tn"])
    merge_w = (bf(w_conv_out), bf(_channel_major(w_s5_val, 1)), bf(_channel_major(w_s5_gate, 1)),
               bf(w_sb_out), bf(w_gla_out))
    w_o_bf = bf(w_o)
    ffn_w = (bf(ffn_w_gate), bf(ffn_w_up), bf(ffn_w_down))
    for l in range(depth):
        sh1, sc1, gt1, sh2, sc2, gt2 = [m.reshape(bsz, 1, d) for m in jnp.split(mod[l], 6, axis=-1)]
        proj, lr = _inproj(xf, sc1, sh1, *w_in_parts, l, seq, t["inproj_tm"], t["inproj_tn"])
        pre_conv = _short_conv(proj, conv_w[l], seq, t["conv_tc"])
        s5_params = _s5_params(s5_lam_re[l], s5_lam_im[l], s5_b_re[l], s5_b_im[l], s5_c_re[l],
                               s5_c_im[l], s5_d[l], s5_log_dt[l])
        pre_s5 = _s5_gelu(proj, s5_params, bsz, seq)
        pre_sb = _stick_breaking(proj, bsz, seq, t["sb_tq"])
        pre_gla = _gla(proj, lr, gla_w_gate[l], gla_b_gate[l], gla_norm_g[l], bsz, seq)
        merged = _merge(pre_conv, pre_s5, pre_sb, pre_gla, proj, *merge_w, l, t["merge_tm"], t["merge_tn"])
        xf = _out_proj_ln(merged, xf, gt1, w_o_bf, ln1_g, ln1_b, l, seq, alpha, t["out_tm"])
        xf = _ffn(xf, sc2, sh2, gt2, *ffn_w, ln2_g, ln2_b, l, seq, alpha, t["ffn_tm"], t["ffn_tf"])
    return xf.reshape(bsz, seq, d)
```

```python
import functools
import math

import jax
import jax.numpy as jnp
from jax import lax
from jax.experimental import pallas as pl
from jax.experimental.pallas import tpu as pltpu

F32 = jnp.float32
BF16 = jnp.bfloat16

LN_EPS = 1e-5
LANES = 128
MIX_W = 1024
S5_GROUP = 16
S5_STATE = 64
S5_CHUNK = 16
SB_HEADS = 8
SB_DH = 128
SB_HEADS_PER_STEP = 2
GLA_HEADS = 4
GLA_DK = 128
GLA_DV = 256
GLA_RANK = 16
GLA_TAU = 16.0
GLA_CHUNK = 128
GLA_SUB = 16
GLA_HEADS_PER_STEP = 4
VMEM_LIMIT = 52 * 1024 * 1024
SB_SKIP = 90.0

OFF_CV_B, OFF_CV_C, OFF_CV_X = 0, 1024, 2048
OFF_SB_Q, OFF_SB_K, OFF_SB_V = 3072, 4096, 5120
OFF_GL_Q, OFF_GL_K, OFF_GL_V, OFF_GL_R = 6144, 6656, 7168, 8192
OFF_S5_U = 9216
MAIN_W = 10240
OFF_G_CONV, OFF_G_S5, OFF_G_SB, OFF_G_GLA = 10240, 12288, 14336, 16384
PROJ_W = 18432
SRC_S5_U = 3072
SRC_GL_LR = 10240
SRC_GATES = 10256


def _tiles(seq):
    return dict(prep_tn=1024,inproj_tm=min(1024, seq), inproj_tn=2048, conv_tc=min(512, seq),
                sb_tq=min(256, seq),
                merge_tm=min(1024, seq), merge_tn=512, out_tm=min(512, seq),
                ffn_tm=min(512, seq), ffn_tf=512)


def _cparams(sem):
    return pltpu.CompilerParams(dimension_semantics=sem, vmem_limit_bytes=VMEM_LIMIT)


def _layer_norm_rows(x):
    mu = jnp.mean(x, axis=-1, keepdims=True)
    xc = x - mu
    var = jnp.mean(xc * xc, axis=-1, keepdims=True)
    return xc * lax.rsqrt(var + LN_EPS)


ROW_CHUNK = 256


def _for_row_chunks(n_rows, fn):
    chunk = min(ROW_CHUNK, n_rows)

    def body(r, _):
        fn(pl.ds(pl.multiple_of(r * chunk, chunk), chunk))
        return 0

    lax.fori_loop(0, n_rows // chunk, body, 0)


def _ln_modulate_into(h_ref, x_ref, sc_ref, sh_ref):
    def rows(sl):
        h = _layer_norm_rows(x_ref[sl, :]) * (1.0 + sc_ref[0]) + sh_ref[0]
        h_ref[sl, :] = h.astype(BF16)

    _for_row_chunks(x_ref.shape[0], rows)


def _residual_ln_into(o_ref, x_ref, y_ref, gt_ref, g_ref, b_ref, alpha):
    def rows(sl):
        z = alpha * x_ref[sl, :] + gt_ref[0] * y_ref[sl, :]
        o_ref[sl, :] = _layer_norm_rows(z) * g_ref[0] + b_ref[0]

    _for_row_chunks(x_ref.shape[0], rows)


def _sigmoid(x):
    return 1.0 / (1.0 + jnp.exp(-x))


def _log_sigmoid(x):
    return jnp.minimum(x, 0.0) - jnp.log(1.0 + jnp.exp(-jnp.abs(x)))


def _ada_kernel(c_ref, w_ref, b_ref, o_ref):
    c = c_ref[...]
    s = (c * _sigmoid(c)).astype(BF16)
    o_ref[0] = jnp.dot(s, w_ref[0].astype(BF16), preferred_element_type=F32) + b_ref[0]


def _ada_mod(c, ada_w, ada_b):
    depth, d, n = ada_w.shape
    bsz = c.shape[0]
    rows = 8
    c_pad = jnp.zeros((rows, d), F32).at[:bsz].set(c)
    tn = 1024
    out = pl.pallas_call(
        _ada_kernel,
        out_shape=jax.ShapeDtypeStruct((depth, rows, n), F32),
        grid=(depth, n // tn),
        in_specs=[
            pl.BlockSpec((rows, d), lambda l, j: (0, 0)),
            pl.BlockSpec((1, d, tn), lambda l, j: (l, 0, j)),
            pl.BlockSpec((1, 1, tn), lambda l, j: (l, 0, j)),
        ],
        out_specs=pl.BlockSpec((1, rows, tn), lambda l, j: (l, 0, j)),
        compiler_params=_cparams(("parallel", "parallel")),
        name="ada_mod",
    )(c_pad, ada_w, ada_b.reshape(depth, 1, n))
    return out[:, :bsz]


def _w_in_prep_kernel(a_ref, l_ref, p_ref, o_ref, lr_ref, *, n_direct):
    j = pl.program_id(1)

    @pl.when(j == 0)
    def _():
        row = lax.broadcasted_iota(jnp.int32, l_ref.shape[1:], 0)
        lr_ref[0] = jnp.where(row < GLA_RANK, l_ref[0], 0.0).T.astype(BF16)

    @pl.when(j != n_direct)
    def _():
        o_ref[0] = a_ref[0].T.astype(BF16)

    @pl.when(j == n_direct)
    def _():
        rows = jnp.dot(p_ref[...], a_ref[0].astype(BF16), preferred_element_type=F32)
        o_ref[0] = rows.T.astype(BF16)


def _w_in_prep(w_in_t, tn):
    depth, _, d = w_in_t.shape
    n_a = SRC_S5_U // tn
    n_direct = (MAIN_W - MIX_W) // tn
    n_gate = (PROJ_W - MAIN_W) // tn
    dst = jnp.arange(MIX_W)
    perm = (dst[:, None] == (dst[None, :] % S5_GROUP) * (MIX_W // S5_GROUP) + dst[None, :] // S5_GROUP
            ).astype(BF16)

    def src_row(j):
        direct = jnp.where(j < n_a, j * tn, j * tn + MIX_W)
        row = jnp.where(j < n_direct, direct,
                        jnp.where(j == n_direct, SRC_S5_U, SRC_GATES + (j - n_direct - 1) * tn))
        return pl.multiple_of(row, math.gcd(SRC_GATES, tn))

    return pl.pallas_call(
        functools.partial(_w_in_prep_kernel, n_direct=n_direct),
        out_shape=(jax.ShapeDtypeStruct((depth, d, PROJ_W), BF16),
                   jax.ShapeDtypeStruct((depth, d, LANES), BF16)),
        grid=(depth, n_direct + 1 + n_gate),
        in_specs=[
            pl.BlockSpec((pl.Element(1), pl.Element(tn), pl.Element(d)), lambda l, j: (l, src_row(j), 0)),
            pl.BlockSpec((pl.Element(1), pl.Element(LANES), pl.Element(d)), lambda l, j: (l, SRC_GL_LR, 0)),
            pl.BlockSpec((MIX_W, MIX_W), lambda l, j: (0, 0)),
        ],
        out_specs=(pl.BlockSpec((1, d, tn), lambda l, j: (l, 0, j)),
                   pl.BlockSpec((1, d, LANES), lambda l, j: (l, 0, 0))),
        compiler_params=_cparams(("parallel", "arbitrary")),
        name="w_in_prep",
    )(w_in_t, w_in_t, perm)


def _inproj_kernel(x_ref, sc_ref, sh_ref, w_ref, wl_ref, o_ref, lr_ref, h_ref):
    @pl.when(pl.program_id(1) == 0)
    def _():
        _ln_modulate_into(h_ref, x_ref, sc_ref, sh_ref)
        lr_ref[...] = jnp.dot(h_ref[...], wl_ref[0], preferred_element_type=F32).astype(BF16)

    o_ref[...] = jnp.dot(h_ref[...], w_ref[0], preferred_element_type=F32).astype(BF16)


def _inproj(x, sc, sh, w_all, w_lr, layer, seq, tm, tn):
    n_tok, d = x.shape
    width = w_all.shape[2]
    per_seq = seq // tm
    mod = pl.BlockSpec((1, 1, d), lambda i, j: (i // per_seq, 0, 0))
    return pl.pallas_call(
        _inproj_kernel,
        out_shape=(jax.ShapeDtypeStruct((n_tok, width), BF16),
                   jax.ShapeDtypeStruct((n_tok, LANES), BF16)),
        grid=(n_tok // tm, width // tn),
        in_specs=[
            pl.BlockSpec((tm, d), lambda i, j: (i, 0)),
            mod, mod,
            pl.BlockSpec((1, d, tn), lambda i, j: (layer, 0, j)),
            pl.BlockSpec((1, d, LANES), lambda i, j: (layer, 0, 0)),
        ],
        out_specs=(pl.BlockSpec((tm, tn), lambda i, j: (i, j)),
                   pl.BlockSpec((tm, LANES), lambda i, j: (i, 0))),
        scratch_shapes=[pltpu.VMEM((tm, d), BF16)],
        compiler_params=_cparams(("parallel", "arbitrary")),
        name="ln_mod_inproj",
    )(x, sc, sh, w_all, w_lr)


def _conv_kernel(b_ref, c_ref, x_ref, cp_ref, xp_ref, w_ref, o_ref, *, per_seq):
    h = c_ref[...].astype(F32) * x_ref[...].astype(F32)
    hp = cp_ref[...].astype(F32) * xp_ref[...].astype(F32)
    first = (pl.program_id(0) % per_seq) == 0
    hp = jnp.where(first, 0.0, hp)
    row = lax.broadcasted_iota(jnp.int32, h.shape, 0)
    rows_p = hp.shape[0]
    h1 = jnp.where(row == 0, hp[rows_p - 1:rows_p], pltpu.roll(h, 1, 0))
    h2 = pltpu.roll(h, 2, 0)
    h2 = jnp.where(row == 0, hp[rows_p - 2:rows_p - 1], h2)
    h2 = jnp.where(row == 1, hp[rows_p - 1:rows_p], h2)
    w = w_ref[...]
    y = w[0:1] * h2 + w[1:2] * h1 + w[2:3] * h
    o_ref[...] = (b_ref[...].astype(F32) * y).astype(BF16)


def _short_conv(proj, conv_w, seq, tc):
    n_tok = proj.shape[0]
    halo = 16
    per_seq = seq // tc
    ratio = tc // halo

    def prev(i):
        return jnp.maximum(i * ratio - 1, 0)

    return pl.pallas_call(
        functools.partial(_conv_kernel, per_seq=per_seq),
        out_shape=jax.ShapeDtypeStruct((n_tok, MIX_W), BF16),
        grid=(n_tok // tc,),
        in_specs=[
            pl.BlockSpec((tc, MIX_W), lambda i: (i, OFF_CV_B // MIX_W)),
            pl.BlockSpec((tc, MIX_W), lambda i: (i, OFF_CV_C // MIX_W)),
            pl.BlockSpec((tc, MIX_W), lambda i: (i, OFF_CV_X // MIX_W)),
            pl.BlockSpec((halo, MIX_W), lambda i: (prev(i), OFF_CV_C // MIX_W)),
            pl.BlockSpec((halo, MIX_W), lambda i: (prev(i), OFF_CV_X // MIX_W)),
            pl.BlockSpec((8, MIX_W), lambda i: (0, 0)),
        ],
        out_specs=pl.BlockSpec((tc, MIX_W), lambda i: (i, 0)),
        compiler_params=_cparams(("parallel",)),
        name="short_conv",
    )(proj, proj, proj, proj, proj, jnp.zeros((8, MIX_W), F32).at[:conv_w.shape[0]].set(conv_w))


def _cmul(xr, xi, yr, yi):
    return xr * yr - xi * yi, xr * yi + xi * yr


def _complex_powers(ar, ai, exponent, shape):
    wr, wi = jnp.ones(shape, F32), jnp.zeros(shape, F32)
    for bit in range(4):
        on = ((exponent >> bit) & 1) == 1
        wr, wi = _cmul(wr, wi, jnp.where(on, ar, 1.0), jnp.where(on, ai, 0.0))
        ar, ai = _cmul(ar, ai, ar, ai)
    return wr, wi, ar, ai


def _s5_chunk_matrices(bt_re, bt_im, ct_re, ct_im, ac_re, ac_im, arow_re, arow_im):
    t, hh, p = S5_CHUNK, S5_GROUP, S5_STATE
    hi = lax.Precision.HIGHEST
    lane_tau = lax.broadcasted_iota(jnp.int32, (p, t * hh), 1) // hh
    wr, wi, _, _ = _complex_powers(ac_re, ac_im, lane_tau, (p, t * hh))
    cw_re, cw_im = _cmul(ct_re, ct_im, wr, wi)
    strip = (jnp.dot(bt_re, cw_re, precision=hi, preferred_element_type=F32)
             - jnp.dot(bt_im, cw_im, precision=hi, preferred_element_type=F32))
    lane = lax.broadcasted_iota(jnp.int32, (hh, t * hh), 1)
    m_intra = jnp.concatenate(
        [strip if tp == 0 else jnp.where(lane >= tp * hh, pltpu.roll(strip, tp * hh, 1), 0.0)
         for tp in range(t)], axis=0)
    mo_re, mo_im = _cmul(cw_re, cw_im, ac_re, ac_im)
    both = lambda a: jnp.concatenate([a, a], axis=1)
    low = lax.broadcasted_iota(jnp.int32, (t, 2 * p), 1) < p
    row_tau = (t - 1) - lax.broadcasted_iota(jnp.int32, (t, 2 * p), 0)
    rr, ri, a16_re, a16_im = _complex_powers(both(arow_re), both(arow_im), row_tau, (t, 2 * p))
    r_same, r_cross = jnp.where(low, rr, ri), jnp.where(low, -ri, rr)
    b_re2, b_im2 = both(bt_re), both(bt_im)
    ms = jnp.concatenate([b_re2 * r_same[tp:tp + 1] + b_im2 * r_cross[tp:tp + 1] for tp in range(t)],
                         axis=0)
    mo = jnp.concatenate([mo_re, -mo_im], axis=0)
    return m_intra.astype(BF16), ms.astype(BF16), mo.astype(BF16), a16_re, a16_im


def _s5_kernel(u_ref, btr_ref, bti_ref, ctr_ref, cti_ref, acr_ref, aci_ref, arr_ref, ari_ref, d_ref,
               o_ref, *, n_chunks):
    u = u_ref[0]
    m_intra, ms, mo, ar, ai = _s5_chunk_matrices(
        btr_ref[0], bti_ref[0], ctr_ref[0], cti_ref[0], acr_ref[0], aci_ref[0], arr_ref[0], ari_ref[0])
    s = jnp.dot(u, ms, preferred_element_type=F32)
    chunk = lax.broadcasted_iota(jnp.int32, s.shape, 0) % n_chunks
    low = lax.broadcasted_iota(jnp.int32, ar.shape, 1) < S5_STATE

    def shifted(a, rows):
        return jnp.where(chunk >= rows, pltpu.roll(a, rows, 0), 0.0)

    step = 1
    while step < n_chunks:
        prev = shifted(s, step)
        s = s + ar * prev + jnp.where(low, -ai, ai) * pltpu.roll(prev, S5_STATE, 1)
        ar, ai = ar * ar - ai * ai, 2.0 * ar * ai
        step *= 2
    y = jnp.dot(u, m_intra, preferred_element_type=F32)
    y += jnp.dot(shifted(s, 1).astype(BF16), mo, preferred_element_type=F32)
    y += d_ref[0] * u.astype(F32)
    inner = math.sqrt(2.0 / math.pi) * (y + 0.044715 * (y * y * y))
    o_ref[0] = (0.5 * y * (1.0 + jnp.tanh(inner))).astype(BF16)


def _s5_params(lam_re, lam_im, b_re, b_im, c_re, c_im, d_skip, log_dt):
    g, p = lam_re.shape
    hh = S5_GROUP
    t = S5_CHUNK
    dt = jnp.exp(log_dt)[:, None]
    mag = jnp.exp(lam_re * dt)
    ang = lam_im * dt
    ab_re, ab_im = mag * jnp.cos(ang), mag * jnp.sin(ang)
    den = lam_re * lam_re + lam_im * lam_im
    nr, ni = ab_re - 1.0, ab_im
    f_re = (nr * lam_re + ni * lam_im) / den
    f_im = (ni * lam_re - nr * lam_im) / den
    bb_re = f_re[..., None] * b_re - f_im[..., None] * b_im
    bb_im = f_re[..., None] * b_im + f_im[..., None] * b_re
    bt_re, bt_im = bb_re.transpose(0, 2, 1), bb_im.transpose(0, 2, 1)
    tile = lambda c: jnp.tile(c.transpose(0, 2, 1), (1, 1, t))
    d_row = jnp.tile(d_skip.reshape(g, 1, hh), (1, 1, t))
    return (bt_re, bt_im, tile(c_re), tile(c_im), ab_re[:, :, None], ab_im[:, :, None],
            ab_re[:, None, :], ab_im[:, None, :], d_row)


def _s5_gelu(proj, params, bsz, seq):
    g = params[0].shape[0]
    t, hh, p = S5_CHUNK, S5_GROUP, S5_STATE
    n_chunks = seq // t
    rows = n_chunks * bsz
    u = proj[:, OFF_S5_U:OFF_S5_U + MIX_W].reshape(rows * t * hh, g)
    u = u.T.reshape(g, rows, t * hh)
    grp = lambda shape: pl.BlockSpec((1,) + shape, lambda i: (i, 0, 0))
    y = pl.pallas_call(
        functools.partial(_s5_kernel, n_chunks=n_chunks),
        out_shape=jax.ShapeDtypeStruct((g, rows, t * hh), BF16),
        grid=(g,),
        in_specs=[grp((rows, t * hh)), grp((hh, p)), grp((hh, p)), grp((p, t * hh)), grp((p, t * hh)),
                  grp((p, 1)), grp((p, 1)), grp((1, p)), grp((1, p)), grp((1, t * hh))],
        out_specs=grp((rows, t * hh)),
        compiler_params=_cparams(("parallel",)),
        name="s5_chunked",
    )(u, *params)
    return y.reshape(g, rows * t * hh).T.reshape(bsz * seq, hh * g)


def _channel_major(w, axis):
    g = w.shape[axis] // S5_GROUP
    shape = w.shape[:axis] + (g, S5_GROUP) + w.shape[axis + 1:]
    return jnp.swapaxes(w.reshape(shape), axis, axis + 1).reshape(w.shape)


def _sb_kernel(q_ref, k_ref, v_ref, o_ref, *, tq, n_heads):
    i = pl.program_id(2)
    scale = SB_DH ** -0.5
    row = lax.broadcasted_iota(jnp.int32, (tq, tq), 0)
    col = lax.broadcasted_iota(jnp.int32, (tq, tq), 1)
    below = row > col
    tri = jnp.where(below, 1.0, 0.0).astype(BF16)

    def logits(cols, j, diag):
        rows = pl.ds(pl.multiple_of(j * tq, tq), tq)
        z = lax.dot_general(q_ref[:, cols], k_ref[rows, cols], (((1,), (1,)), ((), ())),
                            preferred_element_type=F32) * scale
        lp = jnp.log(1.0 + jnp.exp(-jnp.abs(z)))
        log_beta = jnp.minimum(z, 0.0) - lp
        log_1m = log_beta - z
        if diag:
            log_1m = jnp.where(below, log_1m, 0.0)
        local = jnp.dot(log_1m.astype(BF16), tri, preferred_element_type=F32)
        return log_beta, local, jnp.sum(log_1m, axis=-1, keepdims=True), v_ref[rows, cols]

    def weigh(parts, carry, diag):
        log_beta, local, total, vj = parts
        w = jnp.exp(log_beta + local + carry)
        if diag:
            w = jnp.where(below, w, 0.0)
        return jnp.dot(w.astype(BF16), vj, preferred_element_type=F32), carry + total

    has_prev = i > 0
    heads = [slice(hd * SB_DH, (hd + 1) * SB_DH) for hd in range(n_heads)]
    started = []
    for cols in heads:
        d_parts = logits(cols, i, True)
        p_parts = logits(cols, jnp.maximum(i - 1, 0), False)
        acc, carry = weigh(d_parts, jnp.zeros((tq, 1), F32), True)
        p_acc, p_carry = weigh(p_parts, carry, False)
        started.append((jnp.where(has_prev, p_carry, carry), jnp.where(has_prev, acc + p_acc, acc)))

    def cond(state):
        j, top = state[0], state[1]
        return jnp.logical_and(j >= 0, top > -SB_SKIP)

    for cols, (carry, acc) in zip(heads, started):
        def body(state, cols=cols):
            j, _, carry, acc = state
            d_acc, carry = weigh(logits(cols, j, False), carry, False)
            return j - 1, jnp.max(carry), carry, acc + d_acc

        state = lax.while_loop(cond, body, (i - 2, jnp.max(carry), carry, acc))
        o_ref[:, cols] = state[3].astype(BF16)


def _stick_breaking(proj, bsz, seq, tq):
    n_tok = proj.shape[0]
    nq = seq // tq
    nh = SB_HEADS_PER_STEP
    width = nh * SB_DH
    return pl.pallas_call(
        functools.partial(_sb_kernel, tq=tq, n_heads=nh),
        out_shape=jax.ShapeDtypeStruct((n_tok, MIX_W), BF16),
        grid=(bsz, SB_HEADS // nh, nq),
        in_specs=[
            pl.BlockSpec((tq, width), lambda b, h, i: (b * nq + i, OFF_SB_Q // width + h)),
            pl.BlockSpec((seq, width), lambda b, h, i: (b, OFF_SB_K // width + h)),
            pl.BlockSpec((seq, width), lambda b, h, i: (b, OFF_SB_V // width + h)),
        ],
        out_specs=pl.BlockSpec((tq, width), lambda b, h, i: (b * nq + i, h)),
        compiler_params=_cparams(("parallel", "parallel", "arbitrary")),
        name="stick_breaking",
    )(proj, proj, proj)


def _split3(x):
    hi = x.astype(BF16)
    r1 = x - hi.astype(F32)
    mid = r1.astype(BF16)
    lo = (r1 - mid.astype(F32)).astype(BF16)
    return hi, mid, lo


def _gla_kernel(q_ref, k_ref, v_ref, r_ref, lr_ref, wg_ref, bg_ref, g_ref, o_ref, st_ref, sc_ref, kd_ref,
                *, n_chunks, n_heads):
    c_len, sub = GLA_CHUNK, GLA_SUB
    st_ref[...] = jnp.zeros_like(st_ref)
    row = lax.broadcasted_iota(jnp.int32, (c_len, c_len), 0)
    col = lax.broadcasted_iota(jnp.int32, (c_len, c_len), 1)
    tri_incl = jnp.where(col <= row, 1.0, 0.0).astype(BF16)
    col_s = lax.broadcasted_iota(jnp.int32, (sub, c_len), 1)
    row_s = lax.broadcasted_iota(jnp.int32, (sub, GLA_DK), 0)
    nt = (((1,), (1,)), ((), ()))

    def head_chunk(hd, rows):
        kcols = slice(hd * GLA_DK, (hd + 1) * GLA_DK)
        vcols = slice(hd * GLA_DV, (hd + 1) * GLA_DV)
        q = q_ref[rows, kcols].astype(F32) * (GLA_DK ** -0.5)
        k = k_ref[rows, kcols].astype(F32)
        v = v_ref[rows, vcols]
        pre = jnp.dot(lr_ref[rows, :], wg_ref[:, kcols], preferred_element_type=F32) + bg_ref[:, kcols]
        la = _log_sigmoid(pre) * (1.0 / GLA_TAU)
        b = sum(jnp.dot(tri_incl, part, preferred_element_type=F32) for part in _split3(la))
        bx = b - la
        b_end = b[c_len - 1:c_len]
        st = st_ref[hd]
        qe = (q * jnp.exp(b)).astype(BF16)
        o = lax.dot_general(qe, st.astype(BF16), nt, preferred_element_type=F32)
        kd_ref[hd] = jnp.zeros((c_len, GLA_DK), F32)
        for i in range(c_len // sub):
            lo_row = i * sub
            sl = slice(lo_row, lo_row + sub)
            bi, qi, ki = b[sl], q[sl], k[sl]
            ref_i = bx[lo_row:lo_row + 1]
            if i == 0:
                sc = jnp.zeros((sub, c_len), F32)
            else:
                pl_row = lo_row - sub
                ps = slice(pl_row, lo_row)
                if pl_row:
                    kd_ref[hd, :pl_row, :] = kd_ref[hd, :pl_row, :] * jnp.exp(ref_i - bx[pl_row:pl_row + 1])
                kd_ref[hd, ps, :] = k[ps] * jnp.exp(ref_i - b[ps])
                qd = (qi * jnp.exp(bi - ref_i)).astype(BF16)
                sc = lax.dot_general(qd, kd_ref[hd].astype(BF16), nt, preferred_element_type=F32)
            for s in range(sub):
                dec = jnp.exp(jnp.where(row_s >= s, bi - bi[s:s + 1], -1e30))
                cs = jnp.sum(qi * (ki[s:s + 1] * dec), axis=-1, keepdims=True)
                sc = jnp.where(col_s == lo_row + s, cs, sc)
            sc_ref[hd, sl, :] = sc
        o = o + jnp.dot(sc_ref[hd].astype(BF16), v, preferred_element_type=F32)
        ke = (k * jnp.exp(b_end - b)).astype(BF16)
        upd = lax.dot_general(v, ke, (((0,), (0,)), ((), ())), preferred_element_type=F32)
        st_ref[hd] = st * jnp.exp(b_end) + upd
        on = _layer_norm_rows(o) * g_ref[:, vcols]
        r = r_ref[rows, vcols].astype(F32)
        o_ref[rows, vcols] = (on * (r * _sigmoid(r))).astype(BF16)

    def chunk(c, _):
        rows = pl.ds(pl.multiple_of(c * c_len, c_len), c_len)
        for hd in range(n_heads):
            head_chunk(hd, rows)
        return 0

    lax.fori_loop(0, n_chunks, chunk, 0)


def _gla(proj, lr, w_gate, b_gate, norm_g, bsz, seq):
    n_tok = proj.shape[0]
    n_chunks = seq // GLA_CHUNK
    nh = GLA_HEADS_PER_STEP
    wk, wv = nh * GLA_DK, nh * GLA_DV
    wg = jnp.zeros((LANES, GLA_HEADS * GLA_DK), BF16).at[:GLA_RANK].set(w_gate.astype(BF16))
    return pl.pallas_call(
        functools.partial(_gla_kernel, n_chunks=n_chunks, n_heads=nh),
        out_shape=jax.ShapeDtypeStruct((n_tok, MIX_W), BF16),
        grid=(bsz, GLA_HEADS // nh),
        in_specs=[
            pl.BlockSpec((seq, wk), lambda b, h: (b, OFF_GL_Q // wk + h)),
            pl.BlockSpec((seq, wk), lambda b, h: (b, OFF_GL_K // wk + h)),
            pl.BlockSpec((seq, wv), lambda b, h: (b, OFF_GL_V // wv + h)),
            pl.BlockSpec((seq, wv), lambda b, h: (b, OFF_GL_R // wv + h)),
            pl.BlockSpec((seq, LANES), lambda b, h: (b, 0)),
            pl.BlockSpec((LANES, wk), lambda b, h: (0, h)),
            pl.BlockSpec((1, wk), lambda b, h: (0, h)),
            pl.BlockSpec((1, wv), lambda b, h: (0, h)),
        ],
        out_specs=pl.BlockSpec((seq, wv), lambda b, h: (b, h)),
        scratch_shapes=[pltpu.VMEM((nh, GLA_DV, GLA_DK), F32),
                        pltpu.VMEM((nh, GLA_CHUNK, GLA_CHUNK), F32),
                        pltpu.VMEM((nh, GLA_CHUNK, GLA_DK), F32)],
        compiler_params=_cparams(("parallel", "parallel")),
        name="gla",
    )(proj, proj, proj, proj, lr, wg, b_gate.reshape(1, -1), norm_g.reshape(1, -1))


def _merge_kernel(pc_ref, ps_ref, pb_ref, pg_ref, gc_ref, gs_ref, gb_ref, gg_ref,
                  wc_ref, wv_ref, wt_ref, wb_ref, wl_ref, o_ref):
    def mm(a_ref, w_ref):
        return jnp.dot(a_ref[...], w_ref[0], preferred_element_type=F32)

    def gate(ref):
        return _sigmoid(ref[...].astype(F32))

    y_s5 = mm(ps_ref, wv_ref) * _sigmoid(mm(ps_ref, wt_ref))
    merged = (gate(gc_ref) * mm(pc_ref, wc_ref) + gate(gs_ref) * y_s5
              + gate(gb_ref) * mm(pb_ref, wb_ref) + gate(gg_ref) * mm(pg_ref, wl_ref))
    o_ref[...] = merged.astype(BF16)


def _merge(pre_conv, pre_s5, pre_sb, pre_gla, proj, w_conv, w_val, w_gate, w_sb, w_gla, layer, tm, tn):
    n_tok = proj.shape[0]
    d = w_conv.shape[2]
    pre = pl.BlockSpec((tm, MIX_W), lambda i, j: (i, 0))
    gcol = lambda off: pl.BlockSpec((tm, tn), lambda i, j: (i, off // tn + j))
    wcol = pl.BlockSpec((1, MIX_W, tn), lambda i, j: (layer, 0, j))
    return pl.pallas_call(
        _merge_kernel,
        out_shape=jax.ShapeDtypeStruct((n_tok, d), BF16),
        grid=(n_tok // tm, d // tn),
        in_specs=[pre, pre, pre, pre,
                  gcol(OFF_G_CONV), gcol(OFF_G_S5), gcol(OFF_G_SB), gcol(OFF_G_GLA),
                  wcol, wcol, wcol, wcol, wcol],
        out_specs=pl.BlockSpec((tm, tn), lambda i, j: (i, j)),
        compiler_params=_cparams(("parallel", "arbitrary")),
        name="branch_merge",
    )(pre_conv, pre_s5, pre_sb, pre_gla, proj, proj, proj, proj, w_conv, w_val, w_gate, w_sb, w_gla)


def _out_proj_kernel(m_ref, x_ref, gt_ref, wo_ref, lg_ref, lb_ref, o_ref, y_ref, *, alpha):
    y_ref[...] = jnp.dot(m_ref[...], wo_ref[0], preferred_element_type=F32)
    _residual_ln_into(o_ref, x_ref, y_ref, gt_ref, lg_ref, lb_ref, alpha)


def _out_proj_ln(merged, x, gt, w_o, ln_g, ln_b, layer, seq, alpha, tm):
    n_tok, d = x.shape
    per_seq = seq // tm
    vec = pl.BlockSpec((1, 1, d), lambda i: (layer, 0, 0))
    return pl.pallas_call(
        functools.partial(_out_proj_kernel, alpha=alpha),
        out_shape=jax.ShapeDtypeStruct((n_tok, d), F32),
        grid=(n_tok // tm,),
        in_specs=[pl.BlockSpec((tm, d), lambda i: (i, 0)),
                  pl.BlockSpec((tm, d), lambda i: (i, 0)),
                  pl.BlockSpec((1, 1, d), lambda i: (i // per_seq, 0, 0)),
                  pl.BlockSpec((1, d, d), lambda i: (layer, 0, 0)),
                  vec, vec],
        out_specs=pl.BlockSpec((tm, d), lambda i: (i, 0)),
        scratch_shapes=[pltpu.VMEM((tm, d), F32)],
        compiler_params=_cparams(("parallel",)),
        name="out_proj_ln",
    )(merged, x, gt, w_o, ln_g.reshape(-1, 1, d), ln_b.reshape(-1, 1, d))


def _ffn_kernel(x_ref, sc_ref, sh_ref, gt_ref, wg_ref, wu_ref, wd_ref, lg_ref, lb_ref, o_ref,
                h_ref, acc_ref, *, alpha):
    j = pl.program_id(1)

    @pl.when(j == 0)
    def _():
        _ln_modulate_into(h_ref, x_ref, sc_ref, sh_ref)
        acc_ref[...] = jnp.zeros_like(acc_ref)

    h = h_ref[...]
    gate = jnp.dot(h, wg_ref[0], preferred_element_type=F32)
    up = jnp.dot(h, wu_ref[0], preferred_element_type=F32)
    act = (gate * _sigmoid(gate) * up).astype(BF16)
    acc_ref[...] += jnp.dot(act, wd_ref[0], preferred_element_type=F32)

    @pl.when(j == pl.num_programs(1) - 1)
    def _():
        _residual_ln_into(o_ref, x_ref, acc_ref, gt_ref, lg_ref, lb_ref, alpha)


def _ffn(x, sc, sh, gt, w_gate, w_up, w_down, ln_g, ln_b, layer, seq, alpha, tm, tf):
    n_tok, d = x.shape
    d_ff = w_gate.shape[2]
    per_seq = seq // tm
    mod = pl.BlockSpec((1, 1, d), lambda i, j: (i // per_seq, 0, 0))
    vec = pl.BlockSpec((1, 1, d), lambda i, j: (layer, 0, 0))
    return pl.pallas_call(
        functools.partial(_ffn_kernel, alpha=alpha),
        out_shape=jax.ShapeDtypeStruct((n_tok, d), F32),
        grid=(n_tok // tm, d_ff // tf),
        in_specs=[pl.BlockSpec((tm, d), lambda i, j: (i, 0)), mod, mod, mod,
                  pl.BlockSpec((1, d, tf), lambda i, j: (layer, 0, j)),
                  pl.BlockSpec((1, d, tf), lambda i, j: (layer, 0, j)),
                  pl.BlockSpec((1, tf, d), lambda i, j: (layer, j, 0)),
                  vec, vec],
        out_specs=pl.BlockSpec((tm, d), lambda i, j: (i, 0)),
        scratch_shapes=[pltpu.VMEM((tm, d), BF16), pltpu.VMEM((tm, d), F32)],
        compiler_params=_cparams(("parallel", "arbitrary")),
        name="ffn_swiglu_ln",
    )(x, sc, sh, gt, w_gate, w_up, w_down, ln_g.reshape(-1, 1, d), ln_b.reshape(-1, 1, d))


def kernel(x, c, ada_w, ada_b, w_in, conv_w, w_conv_out, s5_lam_re, s5_lam_im, s5_b_re, s5_b_im, s5_c_re, s5_c_im, s5_d, s5_log_dt, w_s5_val, w_s5_gate, w_sb_out, gla_w_gate, gla_b_gate, gla_norm_g, w_gla_out, w_o, ln1_g, ln1_b, ffn_w_gate, ffn_w_up, ffn_w_down, ln2_g, ln2_b):
    bsz, seq, d = x.shape
    depth = ada_w.shape[0]
    alpha = (2.0 * depth) ** 0.25
    mod = _ada_mod(c, ada_w, ada_b)
    xf = x.reshape(bsz * seq, d)
    bf = lambda a: a.astype(BF16)
    t = _tiles(seq)
    w_in_parts = _w_in_prep(jnp.swapaxes(w_in, 1, 2), t["prep_tn"])
    merge_w = (bf(w_conv_out), bf(_channel_major(w_s5_val, 1)), bf(_channel_major(w_s5_gate, 1)),
               bf(w_sb_out), bf(w_gla_out))
    w_o_bf = bf(w_o)
    ffn_w = (bf(ffn_w_gate), bf(ffn_w_up), bf(ffn_w_down))
    for l in range(depth):
        sh1, sc1, gt1, sh2, sc2, gt2 = [m.reshape(bsz, 1, d) for m in jnp.split(mod[l], 6, axis=-1)]
        proj, lr = _inproj(xf, sc1, sh1, *w_in_parts, l, seq, t["inproj_tm"], t["inproj_tn"])
        pre_conv = _short_conv(proj, conv_w[l], seq, t["conv_tc"])
        s5_params = _s5_params(s5_lam_re[l], s5_lam_im[l], s5_b_re[l], s5_b_im[l], s5_c_re[l],
                               s5_c_im[l], s5_d[l], s5_log_dt[l])
        pre_s5 = _s5_gelu(proj, s5_params, bsz, seq)
        pre_sb = _stick_breaking(proj, bsz, seq, t["sb_tq"])
        pre_gla = _gla(proj, lr, gla_w_gate[l], gla_b_gate[l], gla_norm_g[l], bsz, seq)
        merged = _merge(pre_conv, pre_s5, pre_sb, pre_gla, proj, *merge_w, l, t["merge_tm"], t["merge_tn"])
        xf = _out_proj_ln(merged, xf, gt1, w_o_bf, ln1_g, ln1_b, l, seq, alpha, t["out_tm"])
        xf = _ffn(xf, sc2, sh2, gt2, *ffn_w, ln2_g, ln2_b, l, seq, alpha, t["ffn_tm"], t["ffn_tf"])
    return xf.reshape(bsz, seq, d)
```

```python
import functools
import math

import jax
import jax.numpy as jnp
from jax import lax
from jax.experimental import pallas as pl
from jax.experimental.pallas import tpu as pltpu

F32 = jnp.float32
BF16 = jnp.bfloat16

LN_EPS = 1e-5
LOG2_E = math.log2(math.e)
LANES = 128
MIX_W = 1024
S5_GROUP = 16
S5_STATE = 64
S5_CHUNK = 16
S5_GROUPS_PER_STEP = 2
SB_HEADS = 8
SB_DH = 128
SB_HEADS_PER_STEP = 4
GLA_HEADS = 4
GLA_DK = 128
GLA_DV = 256
GLA_RANK = 16
GLA_TAU = 16.0
GLA_CHUNK = 128
GLA_SUB = 16
GLA_HEADS_PER_STEP = 4
VMEM_LIMIT = 52 * 1024 * 1024
SB_SKIP = 90.0

OFF_CV_B, OFF_CV_C, OFF_CV_X = 0, 1024, 2048
OFF_SB_Q, OFF_SB_K, OFF_SB_V = 3072, 4096, 5120
OFF_GL_Q, OFF_GL_K, OFF_GL_V, OFF_GL_R = 6144, 6656, 7168, 8192
OFF_S5_U = 9216
MAIN_W = 10240
OFF_G_CONV, OFF_G_S5, OFF_G_SB, OFF_G_GLA = 10240, 12288, 14336, 16384
PROJ_W = 18432
SRC_S5_U = 3072
SRC_GL_LR = 10240
SRC_GATES = 10256


def _tiles(seq):
    return dict(prep_tn=1024,inproj_tm=min(1024, seq), inproj_tn=2048, conv_tc=min(512, seq),
                sb_tq=min(256, seq),
                merge_tm=min(1024, seq), merge_tn=512, out_tm=min(512, seq),
                ffn_tm=min(512, seq), ffn_tf=512)


def _cparams(sem):
    return pltpu.CompilerParams(dimension_semantics=sem, vmem_limit_bytes=VMEM_LIMIT)


def _layer_norm_rows(x):
    mu = jnp.mean(x, axis=-1, keepdims=True)
    xc = x - mu
    var = jnp.mean(xc * xc, axis=-1, keepdims=True)
    return xc * lax.rsqrt(var + LN_EPS)


ROW_CHUNK = 256


def _for_row_chunks(n_rows, fn):
    chunk = min(ROW_CHUNK, n_rows)

    def body(r, _):
        fn(pl.ds(pl.multiple_of(r * chunk, chunk), chunk))
        return 0

    lax.fori_loop(0, n_rows // chunk, body, 0)


def _ln_modulate_into(h_ref, x_ref, sc_ref, sh_ref):
    def rows(sl):
        h = _layer_norm_rows(x_ref[sl, :]) * (1.0 + sc_ref[0]) + sh_ref[0]
        h_ref[sl, :] = h.astype(BF16)

    _for_row_chunks(x_ref.shape[0], rows)


def _residual_ln_into(o_ref, x_ref, y_ref, gt_ref, g_ref, b_ref, alpha):
    def rows(sl):
        z = alpha * x_ref[sl, :] + gt_ref[0] * y_ref[sl, :]
        o_ref[sl, :] = _layer_norm_rows(z) * g_ref[0] + b_ref[0]

    _for_row_chunks(x_ref.shape[0], rows)


def _sigmoid(x):
    return 1.0 / (1.0 + jnp.exp(-x))


def _log_sigmoid(x):
    return jnp.minimum(x, 0.0) - jnp.log(1.0 + jnp.exp(-jnp.abs(x)))


def _ada_kernel(c_ref, w_ref, b_ref, o_ref):
    c = c_ref[...]
    s = (c * _sigmoid(c)).astype(BF16)
    o_ref[0] = jnp.dot(s, w_ref[0].astype(BF16), preferred_element_type=F32) + b_ref[0]


def _ada_mod(c, ada_w, ada_b):
    depth, d, n = ada_w.shape
    bsz = c.shape[0]
    rows = 8
    c_pad = jnp.zeros((rows, d), F32).at[:bsz].set(c)
    tn = 1024
    out = pl.pallas_call(
        _ada_kernel,
        out_shape=jax.ShapeDtypeStruct((depth, rows, n), F32),
        grid=(depth, n // tn),
        in_specs=[
            pl.BlockSpec((rows, d), lambda l, j: (0, 0)),
            pl.BlockSpec((1, d, tn), lambda l, j: (l, 0, j)),
            pl.BlockSpec((1, 1, tn), lambda l, j: (l, 0, j)),
        ],
        out_specs=pl.BlockSpec((1, rows, tn), lambda l, j: (l, 0, j)),
        compiler_params=_cparams(("parallel", "parallel")),
        name="ada_mod",
    )(c_pad, ada_w, ada_b.reshape(depth, 1, n))
    return out[:, :bsz]


def _w_in_prep_kernel(a_ref, l_ref, p_ref, o_ref, lr_ref, *, n_direct):
    j = pl.program_id(1)

    @pl.when(j == 0)
    def _():
        row = lax.broadcasted_iota(jnp.int32, l_ref.shape[1:], 0)
        lr_ref[0] = jnp.where(row < GLA_RANK, l_ref[0], 0.0).T.astype(BF16)

    @pl.when(j != n_direct)
    def _():
        o_ref[0] = a_ref[0].T.astype(BF16)

    @pl.when(j == n_direct)
    def _():
        rows = jnp.dot(p_ref[...], a_ref[0].astype(BF16), preferred_element_type=F32)
        o_ref[0] = rows.T.astype(BF16)


def _w_in_prep(w_in_t, tn):
    depth, _, d = w_in_t.shape
    n_a = SRC_S5_U // tn
    n_direct = (MAIN_W - MIX_W) // tn
    n_gate = (PROJ_W - MAIN_W) // tn
    dst = jnp.arange(MIX_W)
    perm = (dst[:, None] == (dst[None, :] % S5_GROUP) * (MIX_W // S5_GROUP) + dst[None, :] // S5_GROUP
            ).astype(BF16)

    def src_row(j):
        direct = jnp.where(j < n_a, j * tn, j * tn + MIX_W)
        row = jnp.where(j < n_direct, direct,
                        jnp.where(j == n_direct, SRC_S5_U, SRC_GATES + (j - n_direct - 1) * tn))
        return pl.multiple_of(row, math.gcd(SRC_GATES, tn))

    return pl.pallas_call(
        functools.partial(_w_in_prep_kernel, n_direct=n_direct),
        out_shape=(jax.ShapeDtypeStruct((depth, d, PROJ_W), BF16),
                   jax.ShapeDtypeStruct((depth, d, LANES), BF16)),
        grid=(depth, n_direct + 1 + n_gate),
        in_specs=[
            pl.BlockSpec((pl.Element(1), pl.Element(tn), pl.Element(d)), lambda l, j: (l, src_row(j), 0)),
            pl.BlockSpec((pl.Element(1), pl.Element(LANES), pl.Element(d)), lambda l, j: (l, SRC_GL_LR, 0)),
            pl.BlockSpec((MIX_W, MIX_W), lambda l, j: (0, 0)),
        ],
        out_specs=(pl.BlockSpec((1, d, tn), lambda l, j: (l, 0, j)),
                   pl.BlockSpec((1, d, LANES), lambda l, j: (l, 0, 0))),
        compiler_params=_cparams(("parallel", "arbitrary")),
        name="w_in_prep",
    )(w_in_t, w_in_t, perm)


def _inproj_kernel(x_ref, sc_ref, sh_ref, w_ref, wl_ref, o_ref, lr_ref, h_ref):
    @pl.when(pl.program_id(1) == 0)
    def _():
        _ln_modulate_into(h_ref, x_ref, sc_ref, sh_ref)
        lr_ref[...] = jnp.dot(h_ref[...], wl_ref[0], preferred_element_type=F32).astype(BF16)

    o_ref[...] = jnp.dot(h_ref[...], w_ref[0], preferred_element_type=F32).astype(BF16)


def _inproj(x, sc, sh, w_all, w_lr, layer, seq, tm, tn):
    n_tok, d = x.shape
    width = w_all.shape[2]
    per_seq = seq // tm
    mod = pl.BlockSpec((1, 1, d), lambda i, j: (i // per_seq, 0, 0))
    return pl.pallas_call(
        _inproj_kernel,
        out_shape=(jax.ShapeDtypeStruct((n_tok, width), BF16),
                   jax.ShapeDtypeStruct((n_tok, LANES), BF16)),
        grid=(n_tok // tm, width // tn),
        in_specs=[
            pl.BlockSpec((tm, d), lambda i, j: (i, 0)),
            mod, mod,
            pl.BlockSpec((1, d, tn), lambda i, j: (layer, 0, j)),
            pl.BlockSpec((1, d, LANES), lambda i, j: (layer, 0, 0)),
        ],
        out_specs=(pl.BlockSpec((tm, tn), lambda i, j: (i, j)),
                   pl.BlockSpec((tm, LANES), lambda i, j: (i, 0))),
        scratch_shapes=[pltpu.VMEM((tm, d), BF16)],
        compiler_params=_cparams(("parallel", "arbitrary")),
        name="ln_mod_inproj",
    )(x, sc, sh, w_all, w_lr)


def _conv_kernel(b_ref, c_ref, x_ref, cp_ref, xp_ref, w_ref, o_ref, *, per_seq):
    h = c_ref[...].astype(F32) * x_ref[...].astype(F32)
    hp = cp_ref[...].astype(F32) * xp_ref[...].astype(F32)
    first = (pl.program_id(0) % per_seq) == 0
    hp = jnp.where(first, 0.0, hp)
    row = lax.broadcasted_iota(jnp.int32, h.shape, 0)
    rows_p = hp.shape[0]
    h1 = jnp.where(row == 0, hp[rows_p - 1:rows_p], pltpu.roll(h, 1, 0))
    h2 = pltpu.roll(h, 2, 0)
    h2 = jnp.where(row == 0, hp[rows_p - 2:rows_p - 1], h2)
    h2 = jnp.where(row == 1, hp[rows_p - 1:rows_p], h2)
    w = w_ref[...]
    y = w[0:1] * h2 + w[1:2] * h1 + w[2:3] * h
    o_ref[...] = (b_ref[...].astype(F32) * y).astype(BF16)


def _short_conv(proj, conv_w, seq, tc):
    n_tok = proj.shape[0]
    halo = 16
    per_seq = seq // tc
    ratio = tc // halo

    def prev(i):
        return jnp.maximum(i * ratio - 1, 0)

    return pl.pallas_call(
        functools.partial(_conv_kernel, per_seq=per_seq),
        out_shape=jax.ShapeDtypeStruct((n_tok, MIX_W), BF16),
        grid=(n_tok // tc,),
        in_specs=[
            pl.BlockSpec((tc, MIX_W), lambda i: (i, OFF_CV_B // MIX_W)),
            pl.BlockSpec((tc, MIX_W), lambda i: (i, OFF_CV_C // MIX_W)),
            pl.BlockSpec((tc, MIX_W), lambda i: (i, OFF_CV_X // MIX_W)),
            pl.BlockSpec((halo, MIX_W), lambda i: (prev(i), OFF_CV_C // MIX_W)),
            pl.BlockSpec((halo, MIX_W), lambda i: (prev(i), OFF_CV_X // MIX_W)),
            pl.BlockSpec((8, MIX_W), lambda i: (0, 0)),
        ],
        out_specs=pl.BlockSpec((tc, MIX_W), lambda i: (i, 0)),
        compiler_params=_cparams(("parallel",)),
        name="short_conv",
    )(proj, proj, proj, proj, proj, jnp.zeros((8, MIX_W), F32).at[:conv_w.shape[0]].set(conv_w))


def _cmul(xr, xi, yr, yi):
    return xr * yr - xi * yi, xr * yi + xi * yr


def _complex_powers(ar, ai, exponent, shape):
    wr, wi = jnp.ones(shape, F32), jnp.zeros(shape, F32)
    for bit in range(4):
        on = ((exponent >> bit) & 1) == 1
        wr, wi = _cmul(wr, wi, jnp.where(on, ar, 1.0), jnp.where(on, ai, 0.0))
        ar, ai = _cmul(ar, ai, ar, ai)
    return wr, wi, ar, ai


def _s5_chunk_matrices(bt_re, bt_im, ct_re, ct_im, ac_re, ac_im, arow_re, arow_im):
    t, hh, p = S5_CHUNK, S5_GROUP, S5_STATE
    hi = lax.Precision.HIGHEST
    lane_tau = lax.broadcasted_iota(jnp.int32, (p, t * hh), 1) // hh
    wr, wi, _, _ = _complex_powers(ac_re, ac_im, lane_tau, (p, t * hh))
    cw_re, cw_im = _cmul(ct_re, ct_im, wr, wi)
    strip = (jnp.dot(bt_re, cw_re, precision=hi, preferred_element_type=F32)
             - jnp.dot(bt_im, cw_im, precision=hi, preferred_element_type=F32))
    lane = lax.broadcasted_iota(jnp.int32, (hh, t * hh), 1)
    m_intra = jnp.concatenate(
        [strip if tp == 0 else jnp.where(lane >= tp * hh, pltpu.roll(strip, tp * hh, 1), 0.0)
         for tp in range(t)], axis=0)
    mo_re, mo_im = _cmul(cw_re, cw_im, ac_re, ac_im)
    both = lambda a: jnp.concatenate([a, a], axis=1)
    low = lax.broadcasted_iota(jnp.int32, (t, 2 * p), 1) < p
    row_tau = (t - 1) - lax.broadcasted_iota(jnp.int32, (t, 2 * p), 0)
    rr, ri, a16_re, a16_im = _complex_powers(both(arow_re), both(arow_im), row_tau, (t, 2 * p))
    r_same, r_cross = jnp.where(low, rr, ri), jnp.where(low, -ri, rr)
    b_re2, b_im2 = both(bt_re), both(bt_im)
    ms = jnp.concatenate([b_re2 * r_same[tp:tp + 1] + b_im2 * r_cross[tp:tp + 1] for tp in range(t)],
                         axis=0)
    mo = jnp.concatenate([mo_re, -mo_im], axis=0)
    return m_intra.astype(BF16), ms.astype(BF16), mo.astype(BF16), a16_re, a16_im


def _s5_kernel(u_ref, btr_ref, bti_ref, ctr_ref, cti_ref, acr_ref, aci_ref, arr_ref, ari_ref, d_ref,
               o_ref, *, n_chunks):
    for gi in range(u_ref.shape[0]):
        _s5_group(gi, u_ref, btr_ref, bti_ref, ctr_ref, cti_ref, acr_ref, aci_ref, arr_ref, ari_ref, d_ref,
                  o_ref, n_chunks)


def _s5_group(gi, u_ref, btr_ref, bti_ref, ctr_ref, cti_ref, acr_ref, aci_ref, arr_ref, ari_ref, d_ref,
              o_ref, n_chunks):
    u = u_ref[gi]
    m_intra, ms, mo, ar, ai = _s5_chunk_matrices(
        btr_ref[gi], bti_ref[gi], ctr_ref[gi], cti_ref[gi], acr_ref[gi], aci_ref[gi], arr_ref[gi],
        ari_ref[gi])
    s = jnp.dot(u, ms, preferred_element_type=F32)
    chunk = lax.broadcasted_iota(jnp.int32, s.shape, 0) % n_chunks
    low = lax.broadcasted_iota(jnp.int32, ar.shape, 1) < S5_STATE

    def shifted(a, rows):
        return jnp.where(chunk >= rows, pltpu.roll(a, rows, 0), 0.0)

    step = 1
    while step < n_chunks:
        prev = shifted(s, step)
        s = s + ar * prev + jnp.where(low, -ai, ai) * pltpu.roll(prev, S5_STATE, 1)
        ar, ai = ar * ar - ai * ai, 2.0 * ar * ai
        step *= 2
    y = jnp.dot(u, m_intra, preferred_element_type=F32)
    y += jnp.dot(shifted(s, 1).astype(BF16), mo, preferred_element_type=F32)
    y += d_ref[gi] * u.astype(F32)
    inner = math.sqrt(2.0 / math.pi) * (y + 0.044715 * (y * y * y))
    o_ref[gi] = (0.5 * y * (1.0 + jnp.tanh(inner))).astype(BF16)


def _s5_params(lam_re, lam_im, b_re, b_im, c_re, c_im, d_skip, log_dt):
    g, p = lam_re.shape
    hh = S5_GROUP
    t = S5_CHUNK
    dt = jnp.exp(log_dt)[:, None]
    mag = jnp.exp(lam_re * dt)
    ang = lam_im * dt
    ab_re, ab_im = mag * jnp.cos(ang), mag * jnp.sin(ang)
    den = lam_re * lam_re + lam_im * lam_im
    nr, ni = ab_re - 1.0, ab_im
    f_re = (nr * lam_re + ni * lam_im) / den
    f_im = (ni * lam_re - nr * lam_im) / den
    bb_re = f_re[..., None] * b_re - f_im[..., None] * b_im
    bb_im = f_re[..., None] * b_im + f_im[..., None] * b_re
    bt_re, bt_im = bb_re.transpose(0, 2, 1), bb_im.transpose(0, 2, 1)
    tile = lambda c: jnp.tile(c.transpose(0, 2, 1), (1, 1, t))
    d_row = jnp.tile(d_skip.reshape(g, 1, hh), (1, 1, t))
    return (bt_re, bt_im, tile(c_re), tile(c_im), ab_re[:, :, None], ab_im[:, :, None],
            ab_re[:, None, :], ab_im[:, None, :], d_row)


def _s5_gelu(proj, params, bsz, seq):
    g = params[0].shape[0]
    t, hh, p = S5_CHUNK, S5_GROUP, S5_STATE
    n_chunks = seq // t
    rows = n_chunks * bsz
    u = proj[:, OFF_S5_U:OFF_S5_U + MIX_W].reshape(rows * t * hh, g)
    u = u.T.reshape(g, rows, t * hh)
    grp = lambda shape: pl.BlockSpec((S5_GROUPS_PER_STEP,) + shape, lambda i: (i, 0, 0))
    y = pl.pallas_call(
        functools.partial(_s5_kernel, n_chunks=n_chunks),
        out_shape=jax.ShapeDtypeStruct((g, rows, t * hh), BF16),
        grid=(g // S5_GROUPS_PER_STEP,),
        in_specs=[grp((rows, t * hh)), grp((hh, p)), grp((hh, p)), grp((p, t * hh)), grp((p, t * hh)),
                  grp((p, 1)), grp((p, 1)), grp((1, p)), grp((1, p)), grp((1, t * hh))],
        out_specs=grp((rows, t * hh)),
        compiler_params=_cparams(("parallel",)),
        name="s5_chunked",
    )(u, *params)
    return y.reshape(g, rows * t * hh).T.reshape(bsz * seq, hh * g)


def _channel_major(w, axis):
    g = w.shape[axis] // S5_GROUP
    shape = w.shape[:axis] + (g, S5_GROUP) + w.shape[axis + 1:]
    return jnp.swapaxes(w.reshape(shape), axis, axis + 1).reshape(w.shape)


def _sb_kernel(q_ref, k_ref, v_ref, o_ref, *, tq, n_heads):
    i = pl.program_id(2)
    scale = SB_DH ** -0.5 * LOG2_E
    row = lax.broadcasted_iota(jnp.int32, (tq, tq), 0)
    col = lax.broadcasted_iota(jnp.int32, (tq, tq), 1)
    below = row > col
    tri = jnp.where(below, 1.0, 0.0).astype(BF16)

    def logits(cols, j, diag):
        rows = pl.ds(pl.multiple_of(j * tq, tq), tq)
        z = lax.dot_general(q_ref[:, cols], k_ref[rows, cols], (((1,), (1,)), ((), ())),
                            preferred_element_type=F32) * scale
        lp = jnp.log2(1.0 + jnp.exp2(-jnp.abs(z)))
        log_beta = jnp.minimum(z, 0.0) - lp
        log_1m = log_beta - z
        if diag:
            log_1m = jnp.where(below, log_1m, 0.0)
        local = jnp.dot(log_1m.astype(BF16), tri, preferred_element_type=F32)
        return log_beta, local, jnp.sum(log_1m, axis=-1, keepdims=True), v_ref[rows, cols]

    def weigh(parts, carry, diag):
        log_beta, local, total, vj = parts
        w = jnp.exp2(log_beta + local + carry)
        if diag:
            w = jnp.where(below, w, 0.0)
        return jnp.dot(w.astype(BF16), vj, preferred_element_type=F32), carry + total

    has_prev = i > 0
    heads = [slice(hd * SB_DH, (hd + 1) * SB_DH) for hd in range(n_heads)]
    started = []
    for cols in heads:
        d_parts = logits(cols, i, True)
        p_parts = logits(cols, jnp.maximum(i - 1, 0), False)
        acc, carry = weigh(d_parts, jnp.zeros((tq, 1), F32), True)
        p_acc, p_carry = weigh(p_parts, carry, False)
        started.append((jnp.where(has_prev, p_carry, carry), jnp.where(has_prev, acc + p_acc, acc)))

    def cond(state):
        j, top = state[0], state[1]
        return jnp.logical_and(j >= 0, top > -SB_SKIP * LOG2_E)

    for cols, (carry, acc) in zip(heads, started):
        def body(state, cols=cols):
            j, _, carry, acc = state
            d_acc, carry = weigh(logits(cols, j, False), carry, False)
            return j - 1, jnp.max(carry), carry, acc + d_acc

        state = lax.while_loop(cond, body, (i - 2, jnp.max(carry), carry, acc))
        o_ref[:, cols] = state[3].astype(BF16)


def _stick_breaking(proj, bsz, seq, tq):
    n_tok = proj.shape[0]
    nq = seq // tq
    nh = SB_HEADS_PER_STEP
    width = nh * SB_DH
    return pl.pallas_call(
        functools.partial(_sb_kernel, tq=tq, n_heads=nh),
        out_shape=jax.ShapeDtypeStruct((n_tok, MIX_W), BF16),
        grid=(bsz, SB_HEADS // nh, nq),
        in_specs=[
            pl.BlockSpec((tq, width), lambda b, h, i: (b * nq + i, OFF_SB_Q // width + h)),
            pl.BlockSpec((seq, width), lambda b, h, i: (b, OFF_SB_K // width + h)),
            pl.BlockSpec((seq, width), lambda b, h, i: (b, OFF_SB_V // width + h)),
        ],
        out_specs=pl.BlockSpec((tq, width), lambda b, h, i: (b * nq + i, h)),
        compiler_params=_cparams(("parallel", "parallel", "arbitrary")),
        name="stick_breaking",
    )(proj, proj, proj)


def _split3(x):
    hi = x.astype(BF16)
    r1 = x - hi.astype(F32)
    mid = r1.astype(BF16)
    lo = (r1 - mid.astype(F32)).astype(BF16)
    return hi, mid, lo


def _gla_kernel(q_ref, k_ref, v_ref, r_ref, lr_ref, wg_ref, bg_ref, g_ref, o_ref, st_ref, sc_ref, kd_ref,
                *, n_chunks, n_heads):
    c_len, sub = GLA_CHUNK, GLA_SUB
    st_ref[...] = jnp.zeros_like(st_ref)
    row = lax.broadcasted_iota(jnp.int32, (c_len, c_len), 0)
    col = lax.broadcasted_iota(jnp.int32, (c_len, c_len), 1)
    tri_incl = jnp.where(col <= row, 1.0, 0.0).astype(BF16)
    col_s = lax.broadcasted_iota(jnp.int32, (sub, c_len), 1)
    row_s = lax.broadcasted_iota(jnp.int32, (sub, GLA_DK), 0)
    nt = (((1,), (1,)), ((), ()))

    def head_chunk(hd, rows):
        kcols = slice(hd * GLA_DK, (hd + 1) * GLA_DK)
        vcols = slice(hd * GLA_DV, (hd + 1) * GLA_DV)
        q = q_ref[rows, kcols].astype(F32) * (GLA_DK ** -0.5)
        k = k_ref[rows, kcols].astype(F32)
        v = v_ref[rows, vcols]
        pre = jnp.dot(lr_ref[rows, :], wg_ref[:, kcols], preferred_element_type=F32) + bg_ref[:, kcols]
        la = _log_sigmoid(pre) * (1.0 / GLA_TAU)
        b = sum(jnp.dot(tri_incl, part, preferred_element_type=F32) for part in _split3(la))
        b = b * LOG2_E
        bx = b - la * LOG2_E
        b_end = b[c_len - 1:c_len]
        st = st_ref[hd]
        qe = (q * jnp.exp2(b)).astype(BF16)
        o = lax.dot_general(qe, st.astype(BF16), nt, preferred_element_type=F32)
        kd_ref[hd] = jnp.zeros((c_len, GLA_DK), F32)
        for i in range(c_len // sub):
            lo_row = i * sub
            sl = slice(lo_row, lo_row + sub)
            bi, qi, ki = b[sl], q[sl], k[sl]
            ref_i = bx[lo_row:lo_row + 1]
            if i == 0:
                sc = jnp.zeros((sub, c_len), F32)
            else:
                pl_row = lo_row - sub
                ps = slice(pl_row, lo_row)
                if pl_row:
                    kd_ref[hd, :pl_row, :] = kd_ref[hd, :pl_row, :] * jnp.exp2(ref_i - bx[pl_row:pl_row + 1])
                kd_ref[hd, ps, :] = k[ps] * jnp.exp2(ref_i - b[ps])
                qd = (qi * jnp.exp2(bi - ref_i)).astype(BF16)
                sc = lax.dot_general(qd, kd_ref[hd].astype(BF16), nt, preferred_element_type=F32)
            for s in range(sub):
                dec = jnp.exp2(jnp.where(row_s >= s, bi - bi[s:s + 1], -1e30))
                cs = jnp.sum(qi * (ki[s:s + 1] * dec), axis=-1, keepdims=True)
                sc = jnp.where(col_s == lo_row + s, cs, sc)
            sc_ref[hd, sl, :] = sc
        o = o + jnp.dot(sc_ref[hd].astype(BF16), v, preferred_element_type=F32)
        ke = (k * jnp.exp2(b_end - b)).astype(BF16)
        upd = lax.dot_general(v, ke, (((0,), (0,)), ((), ())), preferred_element_type=F32)
        st_ref[hd] = st * jnp.exp2(b_end) + upd
        on = _layer_norm_rows(o) * g_ref[:, vcols]
        r = r_ref[rows, vcols].astype(F32)
        o_ref[rows, vcols] = (on * (r * _sigmoid(r))).astype(BF16)

    def chunk(c, _):
        rows = pl.ds(pl.multiple_of(c * c_len, c_len), c_len)
        for hd in range(n_heads):
            head_chunk(hd, rows)
        return 0

    lax.fori_loop(0, n_chunks, chunk, 0)


def _gla(proj, lr, w_gate, b_gate, norm_g, bsz, seq):
    n_tok = proj.shape[0]
    n_chunks = seq // GLA_CHUNK
    nh = GLA_HEADS_PER_STEP
    wk, wv = nh * GLA_DK, nh * GLA_DV
    wg = jnp.zeros((LANES, GLA_HEADS * GLA_DK), BF16).at[:GLA_RANK].set(w_gate.astype(BF16))
    return pl.pallas_call(
        functools.partial(_gla_kernel, n_chunks=n_chunks, n_heads=nh),
        out_shape=jax.ShapeDtypeStruct((n_tok, MIX_W), BF16),
        grid=(bsz, GLA_HEADS // nh),
        in_specs=[
            pl.BlockSpec((seq, wk), lambda b, h: (b, OFF_GL_Q // wk + h)),
            pl.BlockSpec((seq, wk), lambda b, h: (b, OFF_GL_K // wk + h)),
            pl.BlockSpec((seq, wv), lambda b, h: (b, OFF_GL_V // wv + h)),
            pl.BlockSpec((seq, wv), lambda b, h: (b, OFF_GL_R // wv + h)),
            pl.BlockSpec((seq, LANES), lambda b, h: (b, 0)),
            pl.BlockSpec((LANES, wk), lambda b, h: (0, h)),
            pl.BlockSpec((1, wk), lambda b, h: (0, h)),
            pl.BlockSpec((1, wv), lambda b, h: (0, h)),
        ],
        out_specs=pl.BlockSpec((seq, wv), lambda b, h: (b, h)),
        scratch_shapes=[pltpu.VMEM((nh, GLA_DV, GLA_DK), F32),
                        pltpu.VMEM((nh, GLA_CHUNK, GLA_CHUNK), F32),
                        pltpu.VMEM((nh, GLA_CHUNK, GLA_DK), F32)],
        compiler_params=_cparams(("parallel", "parallel")),
        name="gla",
    )(proj, proj, proj, proj, lr, wg, b_gate.reshape(1, -1), norm_g.reshape(1, -1))


def _merge_kernel(pc_ref, ps_ref, pb_ref, pg_ref, gc_ref, gs_ref, gb_ref, gg_ref,
                  wc_ref, wv_ref, wt_ref, wb_ref, wl_ref, o_ref):
    def mm(a_ref, w_ref):
        return jnp.dot(a_ref[...], w_ref[0], preferred_element_type=F32)

    def gate(ref):
        return _sigmoid(ref[...].astype(F32))

    y_s5 = mm(ps_ref, wv_ref) * _sigmoid(mm(ps_ref, wt_ref))
    merged = (gate(gc_ref) * mm(pc_ref, wc_ref) + gate(gs_ref) * y_s5
              + gate(gb_ref) * mm(pb_ref, wb_ref) + gate(gg_ref) * mm(pg_ref, wl_ref))
    o_ref[...] = merged.astype(BF16)


def _merge(pre_conv, pre_s5, pre_sb, pre_gla, proj, w_conv, w_val, w_gate, w_sb, w_gla, layer, tm, tn):
    n_tok = proj.shape[0]
    d = w_conv.shape[2]
    pre = pl.BlockSpec((tm, MIX_W), lambda i, j: (i, 0))
    gcol = lambda off: pl.BlockSpec((tm, tn), lambda i, j: (i, off // tn + j))
    wcol = pl.BlockSpec((1, MIX_W, tn), lambda i, j: (layer, 0, j))
    return pl.pallas_call(
        _merge_kernel,
        out_shape=jax.ShapeDtypeStruct((n_tok, d), BF16),
        grid=(n_tok // tm, d // tn),
        in_specs=[pre, pre, pre, pre,
                  gcol(OFF_G_CONV), gcol(OFF_G_S5), gcol(OFF_G_SB), gcol(OFF_G_GLA),
                  wcol, wcol, wcol, wcol, wcol],
        out_specs=pl.BlockSpec((tm, tn), lambda i, j: (i, j)),
        compiler_params=_cparams(("parallel", "arbitrary")),
        name="branch_merge",
    )(pre_conv, pre_s5, pre_sb, pre_gla, proj, proj, proj, proj, w_conv, w_val, w_gate, w_sb, w_gla)


def _out_proj_kernel(m_ref, x_ref, gt_ref, wo_ref, lg_ref, lb_ref, o_ref, y_ref, *, alpha):
    y_ref[...] = jnp.dot(m_ref[...], wo_ref[0], preferred_element_type=F32)
    _residual_ln_into(o_ref, x_ref, y_ref, gt_ref, lg_ref, lb_ref, alpha)


def _out_proj_ln(merged, x, gt, w_o, ln_g, ln_b, layer, seq, alpha, tm):
    n_tok, d = x.shape
    per_seq = seq // tm
    vec = pl.BlockSpec((1, 1, d), lambda i: (layer, 0, 0))
    return pl.pallas_call(
        functools.partial(_out_proj_kernel, alpha=alpha),
        out_shape=jax.ShapeDtypeStruct((n_tok, d), F32),
        grid=(n_tok // tm,),
        in_specs=[pl.BlockSpec((tm, d), lambda i: (i, 0)),
                  pl.BlockSpec((tm, d), lambda i: (i, 0)),
                  pl.BlockSpec((1, 1, d), lambda i: (i // per_seq, 0, 0)),
                  pl.BlockSpec((1, d, d), lambda i: (layer, 0, 0)),
                  vec, vec],
        out_specs=pl.BlockSpec((tm, d), lambda i: (i, 0)),
        scratch_shapes=[pltpu.VMEM((tm, d), F32)],
        compiler_params=_cparams(("parallel",)),
        name="out_proj_ln",
    )(merged, x, gt, w_o, ln_g.reshape(-1, 1, d), ln_b.reshape(-1, 1, d))


def _ffn_kernel(x_ref, sc_ref, sh_ref, gt_ref, wg_ref, wu_ref, wd_ref, lg_ref, lb_ref, o_ref,
                h_ref, acc_ref, *, alpha):
    j = pl.program_id(1)

    @pl.when(j == 0)
    def _():
        _ln_modulate_into(h_ref, x_ref, sc_ref, sh_ref)
        acc_ref[...] = jnp.zeros_like(acc_ref)

    h = h_ref[...]
    gate = jnp.dot(h, wg_ref[0], preferred_element_type=F32)
    up = jnp.dot(h, wu_ref[0], preferred_element_type=F32)
    act = (gate * _sigmoid(gate) * up).astype(BF16)
    acc_ref[...] += jnp.dot(act, wd_ref[0], preferred_element_type=F32)

    @pl.when(j == pl.num_programs(1) - 1)
    def _():
        _residual_ln_into(o_ref, x_ref, acc_ref, gt_ref, lg_ref, lb_ref, alpha)


def _ffn(x, sc, sh, gt, w_gate, w_up, w_down, ln_g, ln_b, layer, seq, alpha, tm, tf):
    n_tok, d = x.shape
    d_ff = w_gate.shape[2]
    per_seq = seq // tm
    mod = pl.BlockSpec((1, 1, d), lambda i, j: (i // per_seq, 0, 0))
    vec = pl.BlockSpec((1, 1, d), lambda i, j: (layer, 0, 0))
    return pl.pallas_call(
        functools.partial(_ffn_kernel, alpha=alpha),
        out_shape=jax.ShapeDtypeStruct((n_tok, d), F32),
        grid=(n_tok // tm, d_ff // tf),
        in_specs=[pl.BlockSpec((tm, d), lambda i, j: (i, 0)), mod, mod, mod,
                  pl.BlockSpec((1, d, tf), lambda i, j: (layer, 0, j)),
                  pl.BlockSpec((1, d, tf), lambda i, j: (layer, 0, j)),
                  pl.BlockSpec((1, tf, d), lambda i, j: (layer, j, 0)),
                  vec, vec],
        out_specs=pl.BlockSpec((tm, d), lambda i, j: (i, 0)),
        scratch_shapes=[pltpu.VMEM((tm, d), BF16), pltpu.VMEM((tm, d), F32)],
        compiler_params=_cparams(("parallel", "arbitrary")),
        name="ffn_swiglu_ln",
    )(x, sc, sh, gt, w_gate, w_up, w_down, ln_g.reshape(-1, 1, d), ln_b.reshape(-1, 1, d))


def kernel(x, c, ada_w, ada_b, w_in, conv_w, w_conv_out, s5_lam_re, s5_lam_im, s5_b_re, s5_b_im, s5_c_re, s5_c_im, s5_d, s5_log_dt, w_s5_val, w_s5_gate, w_sb_out, gla_w_gate, gla_b_gate, gla_norm_g, w_gla_out, w_o, ln1_g, ln1_b, ffn_w_gate, ffn_w_up, ffn_w_down, ln2_g, ln2_b):
    bsz, seq, d = x.shape
    depth = ada_w.shape[0]
    alpha = (2.0 * depth) ** 0.25
    mod = _ada_mod(c, ada_w, ada_b)
    xf = x.reshape(bsz * seq, d)
    bf = lambda a: a.astype(BF16)
    t = _tiles(seq)
    w_in_parts = _w_in_prep(jnp.swapaxes(w_in, 1, 2), t["prep_tn"])
    merge_w = (bf(w_conv_out), bf(_channel_major(w_s5_val, 1)), bf(_channel_major(w_s5_gate, 1)),
               bf(w_sb_out), bf(w_gla_out))
    w_o_bf = bf(w_o)
    ffn_w = (bf(ffn_w_gate), bf(ffn_w_up), bf(ffn_w_down))
    s5_params = jax.vmap(_s5_params)(s5_lam_re, s5_lam_im, s5_b_re, s5_b_im, s5_c_re, s5_c_im, s5_d,
                                     s5_log_dt)
    for l in range(depth):
        sh1, sc1, gt1, sh2, sc2, gt2 = [m.reshape(bsz, 1, d) for m in jnp.split(mod[l], 6, axis=-1)]
        proj, lr = _inproj(xf, sc1, sh1, *w_in_parts, l, seq, t["inproj_tm"], t["inproj_tn"])
        pre_conv = _short_conv(proj, conv_w[l], seq, t["conv_tc"])
        pre_s5 = _s5_gelu(proj, [p[l] for p in s5_params], bsz, seq)
        pre_sb = _stick_breaking(proj, bsz, seq, t["sb_tq"])
        pre_gla = _gla(proj, lr, gla_w_gate[l], gla_b_gate[l], gla_norm_g[l], bsz, seq)
        merged = _merge(pre_conv, pre_s5, pre_sb, pre_gla, proj, *merge_w, l, t["merge_tm"], t["merge_tn"])
        xf = _out_proj_ln(merged, xf, gt1, w_o_bf, ln1_g, ln1_b, l, seq, alpha, t["out_tm"])
        xf = _ffn(xf, sc2, sh2, gt2, *ffn_w, ln2_g, ln2_b, l, seq, alpha, t["ffn_tm"], t["ffn_tf"])
    return xf.reshape(bsz, seq, d)
```

```python
import functools
import math

import jax
import jax.numpy as jnp
from jax import lax
from jax.experimental import pallas as pl
from jax.experimental.pallas import tpu as pltpu

F32 = jnp.float32
BF16 = jnp.bfloat16

LN_EPS = 1e-5
LOG2_E = math.log2(math.e)
LANES = 128
MIX_W = 1024
S5_GROUP = 16
S5_STATE = 64
S5_CHUNK = 16
S5_GROUPS_PER_STEP = 4
SB_HEADS = 8
SB_DH = 128
SB_HEADS_PER_STEP = 4
GLA_HEADS = 4
GLA_DK = 128
GLA_DV = 256
GLA_RANK = 16
GLA_TAU = 16.0
GLA_CHUNK = 128
GLA_SUB = 16
GLA_HEADS_PER_STEP = 4
VMEM_LIMIT = 52 * 1024 * 1024
SB_SKIP = 90.0

OFF_CV_B, OFF_CV_C, OFF_CV_X = 0, 1024, 2048
OFF_SB_Q, OFF_SB_K, OFF_SB_V = 3072, 4096, 5120
OFF_GL_Q, OFF_GL_K, OFF_GL_V, OFF_GL_R = 6144, 6656, 7168, 8192
OFF_S5_U = 9216
MAIN_W = 10240
OFF_G_CONV, OFF_G_S5, OFF_G_SB, OFF_G_GLA = 10240, 12288, 14336, 16384
PROJ_W = 18432
SRC_S5_U = 3072
SRC_GL_LR = 10240
SRC_GATES = 10256


def _tiles(seq):
    return dict(prep_tn=1024,inproj_tm=min(1024, seq), inproj_tn=2048, conv_tc=min(512, seq),
                sb_tq=min(256, seq),
                merge_tm=min(1024, seq), merge_tn=512, out_tm=min(512, seq),
                ffn_tm=min(512, seq), ffn_tf=512)


def _cparams(sem):
    return pltpu.CompilerParams(dimension_semantics=sem, vmem_limit_bytes=VMEM_LIMIT)


def _layer_norm_rows(x):
    mu = jnp.mean(x, axis=-1, keepdims=True)
    xc = x - mu
    var = jnp.mean(xc * xc, axis=-1, keepdims=True)
    return xc * lax.rsqrt(var + LN_EPS)


OUT_ROW_CHUNK = 256
ROW_CHUNK = 256


def _for_row_chunks(n_rows, fn):
    chunk = min(ROW_CHUNK, n_rows)

    def body(r, _):
        fn(pl.ds(pl.multiple_of(r * chunk, chunk), chunk))
        return 0

    lax.fori_loop(0, n_rows // chunk, body, 0)


def _ln_modulate_into(h_ref, x_ref, sc_ref, sh_ref):
    def rows(sl):
        h = _layer_norm_rows(x_ref[sl, :]) * (1.0 + sc_ref[0]) + sh_ref[0]
        h_ref[sl, :] = h.astype(BF16)

    _for_row_chunks(x_ref.shape[0], rows)


def _residual_ln_into(o_ref, x_ref, y_ref, gt_ref, g_ref, b_ref, alpha):
    def rows(sl):
        z = alpha * x_ref[sl, :] + gt_ref[0] * y_ref[sl, :]
        o_ref[sl, :] = _layer_norm_rows(z) * g_ref[0] + b_ref[0]

    _for_row_chunks(x_ref.shape[0], rows)


def _sigmoid(x):
    return 1.0 / (1.0 + jnp.exp(-x))


def _log_sigmoid(x):
    return jnp.minimum(x, 0.0) - jnp.log(1.0 + jnp.exp(-jnp.abs(x)))


def _ada_kernel(c_ref, w_ref, b_ref, o_ref):
    c = c_ref[...]
    s = (c * _sigmoid(c)).astype(BF16)
    o_ref[0] = jnp.dot(s, w_ref[0].astype(BF16), preferred_element_type=F32) + b_ref[0]


def _ada_mod(c, ada_w, ada_b):
    depth, d, n = ada_w.shape
    bsz = c.shape[0]
    rows = 8
    c_pad = jnp.zeros((rows, d), F32).at[:bsz].set(c)
    tn = 1024
    out = pl.pallas_call(
        _ada_kernel,
        out_shape=jax.ShapeDtypeStruct((depth, rows, n), F32),
        grid=(depth, n // tn),
        in_specs=[
            pl.BlockSpec((rows, d), lambda l, j: (0, 0)),
            pl.BlockSpec((1, d, tn), lambda l, j: (l, 0, j)),
            pl.BlockSpec((1, 1, tn), lambda l, j: (l, 0, j)),
        ],
        out_specs=pl.BlockSpec((1, rows, tn), lambda l, j: (l, 0, j)),
        compiler_params=_cparams(("parallel", "parallel")),
        name="ada_mod",
    )(c_pad, ada_w, ada_b.reshape(depth, 1, n))
    return out[:, :bsz]


def _w_in_prep_kernel(a_ref, l_ref, p_ref, o_ref, lr_ref, *, n_direct):
    j = pl.program_id(1)

    @pl.when(j == 0)
    def _():
        row = lax.broadcasted_iota(jnp.int32, l_ref.shape[1:], 0)
        lr_ref[0] = jnp.where(row < GLA_RANK, l_ref[0], 0.0).T.astype(BF16)

    @pl.when(j != n_direct)
    def _():
        o_ref[0] = a_ref[0].T.astype(BF16)

    @pl.when(j == n_direct)
    def _():
        rows = jnp.dot(p_ref[...], a_ref[0].astype(BF16), preferred_element_type=F32)
        o_ref[0] = rows.T.astype(BF16)


def _w_in_prep(w_in_t, tn):
    depth, _, d = w_in_t.shape
    n_a = SRC_S5_U // tn
    n_direct = (MAIN_W - MIX_W) // tn
    n_gate = (PROJ_W - MAIN_W) // tn
    dst = jnp.arange(MIX_W)
    perm = (dst[:, None] == (dst[None, :] % S5_GROUP) * (MIX_W // S5_GROUP) + dst[None, :] // S5_GROUP
            ).astype(BF16)

    def src_row(j):
        direct = jnp.where(j < n_a, j * tn, j * tn + MIX_W)
        row = jnp.where(j < n_direct, direct,
                        jnp.where(j == n_direct, SRC_S5_U, SRC_GATES + (j - n_direct - 1) * tn))
        return pl.multiple_of(row, math.gcd(SRC_GATES, tn))

    return pl.pallas_call(
        functools.partial(_w_in_prep_kernel, n_direct=n_direct),
        out_shape=(jax.ShapeDtypeStruct((depth, d, PROJ_W), BF16),
                   jax.ShapeDtypeStruct((depth, d, LANES), BF16)),
        grid=(depth, n_direct + 1 + n_gate),
        in_specs=[
            pl.BlockSpec((pl.Element(1), pl.Element(tn), pl.Element(d)), lambda l, j: (l, src_row(j), 0)),
            pl.BlockSpec((pl.Element(1), pl.Element(LANES), pl.Element(d)), lambda l, j: (l, SRC_GL_LR, 0)),
            pl.BlockSpec((MIX_W, MIX_W), lambda l, j: (0, 0)),
        ],
        out_specs=(pl.BlockSpec((1, d, tn), lambda l, j: (l, 0, j)),
                   pl.BlockSpec((1, d, LANES), lambda l, j: (l, 0, 0))),
        compiler_params=_cparams(("parallel", "arbitrary")),
        name="w_in_prep",
    )(w_in_t, w_in_t, perm)


def _inproj_kernel(x_ref, sc_ref, sh_ref, w_ref, wl_ref, o_ref, lr_ref, h_ref):
    @pl.when(pl.program_id(1) == 0)
    def _():
        _ln_modulate_into(h_ref, x_ref, sc_ref, sh_ref)
        lr_ref[...] = jnp.dot(h_ref[...], wl_ref[0], preferred_element_type=F32).astype(BF16)

    o_ref[...] = jnp.dot(h_ref[...], w_ref[0], preferred_element_type=F32).astype(BF16)


def _inproj(x, sc, sh, w_all, w_lr, layer, seq, tm, tn):
    n_tok, d = x.shape
    width = w_all.shape[2]
    per_seq = seq // tm
    mod = pl.BlockSpec((1, 1, d), lambda i, j: (i // per_seq, 0, 0))
    return pl.pallas_call(
        _inproj_kernel,
        out_shape=(jax.ShapeDtypeStruct((n_tok, width), BF16),
                   jax.ShapeDtypeStruct((n_tok, LANES), BF16)),
        grid=(n_tok // tm, width // tn),
        in_specs=[
            pl.BlockSpec((tm, d), lambda i, j: (i, 0)),
            mod, mod,
            pl.BlockSpec((1, d, tn), lambda i, j: (layer, 0, j)),
            pl.BlockSpec((1, d, LANES), lambda i, j: (layer, 0, 0)),
        ],
        out_specs=(pl.BlockSpec((tm, tn), lambda i, j: (i, j)),
                   pl.BlockSpec((tm, LANES), lambda i, j: (i, 0))),
        scratch_shapes=[pltpu.VMEM((tm, d), BF16)],
        compiler_params=_cparams(("parallel", "arbitrary")),
        name="ln_mod_inproj",
    )(x, sc, sh, w_all, w_lr)


def _conv_kernel(b_ref, c_ref, x_ref, cp_ref, xp_ref, w_ref, o_ref, *, per_seq):
    h = c_ref[...].astype(F32) * x_ref[...].astype(F32)
    hp = cp_ref[...].astype(F32) * xp_ref[...].astype(F32)
    first = (pl.program_id(0) % per_seq) == 0
    hp = jnp.where(first, 0.0, hp)
    row = lax.broadcasted_iota(jnp.int32, h.shape, 0)
    rows_p = hp.shape[0]
    h1 = jnp.where(row == 0, hp[rows_p - 1:rows_p], pltpu.roll(h, 1, 0))
    h2 = pltpu.roll(h, 2, 0)
    h2 = jnp.where(row == 0, hp[rows_p - 2:rows_p - 1], h2)
    h2 = jnp.where(row == 1, hp[rows_p - 1:rows_p], h2)
    w = w_ref[...]
    y = w[0:1] * h2 + w[1:2] * h1 + w[2:3] * h
    o_ref[...] = (b_ref[...].astype(F32) * y).astype(BF16)


def _short_conv(proj, conv_w, seq, tc):
    n_tok = proj.shape[0]
    halo = 16
    per_seq = seq // tc
    ratio = tc // halo

    def prev(i):
        return jnp.maximum(i * ratio - 1, 0)

    return pl.pallas_call(
        functools.partial(_conv_kernel, per_seq=per_seq),
        out_shape=jax.ShapeDtypeStruct((n_tok, MIX_W), BF16),
        grid=(n_tok // tc,),
        in_specs=[
            pl.BlockSpec((tc, MIX_W), lambda i: (i, OFF_CV_B // MIX_W)),
            pl.BlockSpec((tc, MIX_W), lambda i: (i, OFF_CV_C // MIX_W)),
            pl.BlockSpec((tc, MIX_W), lambda i: (i, OFF_CV_X // MIX_W)),
            pl.BlockSpec((halo, MIX_W), lambda i: (prev(i), OFF_CV_C // MIX_W)),
            pl.BlockSpec((halo, MIX_W), lambda i: (prev(i), OFF_CV_X // MIX_W)),
            pl.BlockSpec((8, MIX_W), lambda i: (0, 0)),
        ],
        out_specs=pl.BlockSpec((tc, MIX_W), lambda i: (i, 0)),
        compiler_params=_cparams(("parallel",)),
        name="short_conv",
    )(proj, proj, proj, proj, proj, jnp.zeros((8, MIX_W), F32).at[:conv_w.shape[0]].set(conv_w))


def _cmul(xr, xi, yr, yi):
    return xr * yr - xi * yi, xr * yi + xi * yr


def _complex_powers(ar, ai, exponent, shape):
    wr, wi = jnp.ones(shape, F32), jnp.zeros(shape, F32)
    for bit in range(4):
        on = ((exponent >> bit) & 1) == 1
        wr, wi = _cmul(wr, wi, jnp.where(on, ar, 1.0), jnp.where(on, ai, 0.0))
        ar, ai = _cmul(ar, ai, ar, ai)
    return wr, wi, ar, ai


def _s5_chunk_matrices(bt_re, bt_im, ct_re, ct_im, ac_re, ac_im, arow_re, arow_im):
    t, hh, p = S5_CHUNK, S5_GROUP, S5_STATE
    hi = lax.Precision.HIGHEST
    lane_tau = lax.broadcasted_iota(jnp.int32, (p, t * hh), 1) // hh
    wr, wi, _, _ = _complex_powers(ac_re, ac_im, lane_tau, (p, t * hh))
    cw_re, cw_im = _cmul(ct_re, ct_im, wr, wi)
    strip = (jnp.dot(bt_re, cw_re, precision=hi, preferred_element_type=F32)
             - jnp.dot(bt_im, cw_im, precision=hi, preferred_element_type=F32))
    lane = lax.broadcasted_iota(jnp.int32, (hh, t * hh), 1)
    m_intra = jnp.concatenate(
        [strip if tp == 0 else jnp.where(lane >= tp * hh, pltpu.roll(strip, tp * hh, 1), 0.0)
         for tp in range(t)], axis=0)
    mo_re, mo_im = _cmul(cw_re, cw_im, ac_re, ac_im)
    both = lambda a: jnp.concatenate([a, a], axis=1)
    low = lax.broadcasted_iota(jnp.int32, (t, 2 * p), 1) < p
    row_tau = (t - 1) - lax.broadcasted_iota(jnp.int32, (t, 2 * p), 0)
    rr, ri, a16_re, a16_im = _complex_powers(both(arow_re), both(arow_im), row_tau, (t, 2 * p))
    r_same, r_cross = jnp.where(low, rr, ri), jnp.where(low, -ri, rr)
    b_re2, b_im2 = both(bt_re), both(bt_im)
    ms = jnp.concatenate([b_re2 * r_same[tp:tp + 1] + b_im2 * r_cross[tp:tp + 1] for tp in range(t)],
                         axis=0)
    mo = jnp.concatenate([mo_re, -mo_im], axis=0)
    return m_intra.astype(BF16), ms.astype(BF16), mo.astype(BF16), a16_re, a16_im


def _s5_kernel(u_ref, btr_ref, bti_ref, ctr_ref, cti_ref, acr_ref, aci_ref, arr_ref, ari_ref, d_ref,
               o_ref, *, n_chunks):
    for gi in range(u_ref.shape[0]):
        _s5_group(gi, u_ref, btr_ref, bti_ref, ctr_ref, cti_ref, acr_ref, aci_ref, arr_ref, ari_ref, d_ref,
                  o_ref, n_chunks)


def _s5_group(gi, u_ref, btr_ref, bti_ref, ctr_ref, cti_ref, acr_ref, aci_ref, arr_ref, ari_ref, d_ref,
              o_ref, n_chunks):
    u = u_ref[gi]
    m_intra, ms, mo, ar, ai = _s5_chunk_matrices(
        btr_ref[gi], bti_ref[gi], ctr_ref[gi], cti_ref[gi], acr_ref[gi], aci_ref[gi], arr_ref[gi],
        ari_ref[gi])
    s = jnp.dot(u, ms, preferred_element_type=F32)
    chunk = lax.broadcasted_iota(jnp.int32, s.shape, 0) % n_chunks
    low = lax.broadcasted_iota(jnp.int32, ar.shape, 1) < S5_STATE

    def shifted(a, rows):
        return jnp.where(chunk >= rows, pltpu.roll(a, rows, 0), 0.0)

    step = 1
    while step < n_chunks:
        prev = shifted(s, step)
        s = s + ar * prev + jnp.where(low, -ai, ai) * pltpu.roll(prev, S5_STATE, 1)
        ar, ai = ar * ar - ai * ai, 2.0 * ar * ai
        step *= 2
    y = jnp.dot(u, m_intra, preferred_element_type=F32)
    y += jnp.dot(shifted(s, 1).astype(BF16), mo, preferred_element_type=F32)
    y += d_ref[gi] * u.astype(F32)
    inner = math.sqrt(2.0 / math.pi) * (y + 0.044715 * (y * y * y))
    o_ref[gi] = (0.5 * y * (1.0 + jnp.tanh(inner))).astype(BF16)


def _s5_params(lam_re, lam_im, b_re, b_im, c_re, c_im, d_skip, log_dt):
    g, p = lam_re.shape
    hh = S5_GROUP
    t = S5_CHUNK
    dt = jnp.exp(log_dt)[:, None]
    mag = jnp.exp(lam_re * dt)
    ang = lam_im * dt
    ab_re, ab_im = mag * jnp.cos(ang), mag * jnp.sin(ang)
    den = lam_re * lam_re + lam_im * lam_im
    nr, ni = ab_re - 1.0, ab_im
    f_re = (nr * lam_re + ni * lam_im) / den
    f_im = (ni * lam_re - nr * lam_im) / den
    bb_re = f_re[..., None] * b_re - f_im[..., None] * b_im
    bb_im = f_re[..., None] * b_im + f_im[..., None] * b_re
    bt_re, bt_im = bb_re.transpose(0, 2, 1), bb_im.transpose(0, 2, 1)
    tile = lambda c: jnp.tile(c.transpose(0, 2, 1), (1, 1, t))
    d_row = jnp.tile(d_skip.reshape(g, 1, hh), (1, 1, t))
    return (bt_re, bt_im, tile(c_re), tile(c_im), ab_re[:, :, None], ab_im[:, :, None],
            ab_re[:, None, :], ab_im[:, None, :], d_row)


def _s5_gelu(proj, params, bsz, seq):
    g = params[0].shape[0]
    t, hh, p = S5_CHUNK, S5_GROUP, S5_STATE
    n_chunks = seq // t
    rows = n_chunks * bsz
    u = proj[:, OFF_S5_U:OFF_S5_U + MIX_W].reshape(rows * t * hh, g)
    u = u.T.reshape(g, rows, t * hh)
    grp = lambda shape: pl.BlockSpec((S5_GROUPS_PER_STEP,) + shape, lambda i: (i, 0, 0))
    y = pl.pallas_call(
        functools.partial(_s5_kernel, n_chunks=n_chunks),
        out_shape=jax.ShapeDtypeStruct((g, rows, t * hh), BF16),
        grid=(g // S5_GROUPS_PER_STEP,),
        in_specs=[grp((rows, t * hh)), grp((hh, p)), grp((hh, p)), grp((p, t * hh)), grp((p, t * hh)),
                  grp((p, 1)), grp((p, 1)), grp((1, p)), grp((1, p)), grp((1, t * hh))],
        out_specs=grp((rows, t * hh)),
        compiler_params=_cparams(("parallel",)),
        name="s5_chunked",
    )(u, *params)
    return y.reshape(g, rows * t * hh).T.reshape(bsz * seq, hh * g)


def _channel_major(w, axis):
    g = w.shape[axis] // S5_GROUP
    shape = w.shape[:axis] + (g, S5_GROUP) + w.shape[axis + 1:]
    return jnp.swapaxes(w.reshape(shape), axis, axis + 1).reshape(w.shape)


def _sb_kernel(q_ref, k_ref, v_ref, o_ref, *, tq, n_heads):
    i = pl.program_id(2)
    scale = SB_DH ** -0.5 * LOG2_E
    row = lax.broadcasted_iota(jnp.int32, (tq, tq), 0)
    col = lax.broadcasted_iota(jnp.int32, (tq, tq), 1)
    below = row > col
    tri = jnp.where(below, 1.0, 0.0).astype(BF16)

    def logits(cols, j, diag):
        rows = pl.ds(pl.multiple_of(j * tq, tq), tq)
        z = lax.dot_general(q_ref[:, cols], k_ref[rows, cols], (((1,), (1,)), ((), ())),
                            preferred_element_type=F32) * scale
        lp = jnp.log2(1.0 + jnp.exp2(-jnp.abs(z)))
        log_beta = jnp.minimum(z, 0.0) - lp
        log_1m = log_beta - z
        if diag:
            log_1m = jnp.where(below, log_1m, 0.0)
        local = jnp.dot(log_1m.astype(BF16), tri, preferred_element_type=F32)
        return log_beta, local, jnp.sum(log_1m, axis=-1, keepdims=True), v_ref[rows, cols]

    def weigh(parts, carry, diag):
        log_beta, local, total, vj = parts
        w = jnp.exp2(log_beta + local + carry)
        if diag:
            w = jnp.where(below, w, 0.0)
        return jnp.dot(w.astype(BF16), vj, preferred_element_type=F32), carry + total

    has_prev = i > 0
    heads = [slice(hd * SB_DH, (hd + 1) * SB_DH) for hd in range(n_heads)]
    started = []
    for cols in heads:
        d_parts = logits(cols, i, True)
        p_parts = logits(cols, jnp.maximum(i - 1, 0), False)
        acc, carry = weigh(d_parts, jnp.zeros((tq, 1), F32), True)
        p_acc, p_carry = weigh(p_parts, carry, False)
        started.append((jnp.where(has_prev, p_carry, carry), jnp.where(has_prev, acc + p_acc, acc)))

    def cond(state):
        j, top = state[0], state[1]
        return jnp.logical_and(j >= 0, top > -SB_SKIP * LOG2_E)

    for cols, (carry, acc) in zip(heads, started):
        def body(state, cols=cols):
            j, _, carry, acc = state
            d_acc, carry = weigh(logits(cols, j, False), carry, False)
            return j - 1, jnp.max(carry), carry, acc + d_acc

        state = lax.while_loop(cond, body, (i - 2, jnp.max(carry), carry, acc))
        o_ref[:, cols] = state[3].astype(BF16)


def _stick_breaking(proj, bsz, seq, tq):
    n_tok = proj.shape[0]
    nq = seq // tq
    nh = SB_HEADS_PER_STEP
    width = nh * SB_DH
    return pl.pallas_call(
        functools.partial(_sb_kernel, tq=tq, n_heads=nh),
        out_shape=jax.ShapeDtypeStruct((n_tok, MIX_W), BF16),
        grid=(bsz, SB_HEADS // nh, nq),
        in_specs=[
            pl.BlockSpec((tq, width), lambda b, h, i: (b * nq + i, OFF_SB_Q // width + h)),
            pl.BlockSpec((seq, width), lambda b, h, i: (b, OFF_SB_K // width + h)),
            pl.BlockSpec((seq, width), lambda b, h, i: (b, OFF_SB_V // width + h)),
        ],
        out_specs=pl.BlockSpec((tq, width), lambda b, h, i: (b * nq + i, h)),
        compiler_params=_cparams(("parallel", "parallel", "arbitrary")),
        name="stick_breaking",
    )(proj, proj, proj)


def _split3(x):
    hi = x.astype(BF16)
    r1 = x - hi.astype(F32)
    mid = r1.astype(BF16)
    lo = (r1 - mid.astype(F32)).astype(BF16)
    return hi, mid, lo


def _gla_kernel(q_ref, k_ref, v_ref, r_ref, lr_ref, wg_ref, bg_ref, g_ref, o_ref, st_ref, sc_ref, kd_ref,
                *, n_chunks, n_heads):
    c_len, sub = GLA_CHUNK, GLA_SUB
    st_ref[...] = jnp.zeros_like(st_ref)
    row = lax.broadcasted_iota(jnp.int32, (c_len, c_len), 0)
    col = lax.broadcasted_iota(jnp.int32, (c_len, c_len), 1)
    tri_incl = jnp.where(col <= row, 1.0, 0.0).astype(BF16)
    col_s = lax.broadcasted_iota(jnp.int32, (sub, c_len), 1)
    row_s = lax.broadcasted_iota(jnp.int32, (sub, GLA_DK), 0)
    nt = (((1,), (1,)), ((), ()))

    def head_chunk(hd, rows):
        kcols = slice(hd * GLA_DK, (hd + 1) * GLA_DK)
        vcols = slice(hd * GLA_DV, (hd + 1) * GLA_DV)
        q = q_ref[rows, kcols].astype(F32) * (GLA_DK ** -0.5)
        k = k_ref[rows, kcols].astype(F32)
        v = v_ref[rows, vcols]
        pre = jnp.dot(lr_ref[rows, :], wg_ref[:, kcols], preferred_element_type=F32) + bg_ref[:, kcols]
        la = _log_sigmoid(pre) * (1.0 / GLA_TAU)
        b = sum(jnp.dot(tri_incl, part, preferred_element_type=F32) for part in _split3(la))
        b = b * LOG2_E
        bx = b - la * LOG2_E
        b_end = b[c_len - 1:c_len]
        st = st_ref[hd]
        qe = (q * jnp.exp2(b)).astype(BF16)
        o = lax.dot_general(qe, st.astype(BF16), nt, preferred_element_type=F32)
        kd_ref[hd] = jnp.zeros((c_len, GLA_DK), F32)
        for i in range(c_len // sub):
            lo_row = i * sub
            sl = slice(lo_row, lo_row + sub)
            bi, qi, ki = b[sl], q[sl], k[sl]
            ref_i = bx[lo_row:lo_row + 1]
            if i == 0:
                sc = jnp.zeros((sub, c_len), F32)
            else:
                pl_row = lo_row - sub
                ps = slice(pl_row, lo_row)
                if pl_row:
                    kd_ref[hd, :pl_row, :] = kd_ref[hd, :pl_row, :] * jnp.exp2(ref_i - bx[pl_row:pl_row + 1])
                kd_ref[hd, ps, :] = k[ps] * jnp.exp2(ref_i - b[ps])
                qd = (qi * jnp.exp2(bi - ref_i)).astype(BF16)
                sc = lax.dot_general(qd, kd_ref[hd].astype(BF16), nt, preferred_element_type=F32)
            for s in range(sub):
                dec = jnp.exp2(jnp.where(row_s >= s, bi - bi[s:s + 1], -1e30))
                cs = jnp.sum(qi * (ki[s:s + 1] * dec), axis=-1, keepdims=True)
                sc = jnp.where(col_s == lo_row + s, cs, sc)
            sc_ref[hd, sl, :] = sc
        o = o + jnp.dot(sc_ref[hd].astype(BF16), v, preferred_element_type=F32)
        ke = (k * jnp.exp2(b_end - b)).astype(BF16)
        upd = lax.dot_general(v, ke, (((0,), (0,)), ((), ())), preferred_element_type=F32)
        st_ref[hd] = st * jnp.exp2(b_end) + upd
        on = _layer_norm_rows(o) * g_ref[:, vcols]
        r = r_ref[rows, vcols].astype(F32)
        o_ref[rows, vcols] = (on * (r * _sigmoid(r))).astype(BF16)

    def chunk(c, _):
        rows = pl.ds(pl.multiple_of(c * c_len, c_len), c_len)
        for hd in range(n_heads):
            head_chunk(hd, rows)
        return 0

    lax.fori_loop(0, n_chunks, chunk, 0)


def _gla(proj, lr, w_gate, b_gate, norm_g, bsz, seq):
    n_tok = proj.shape[0]
    n_chunks = seq // GLA_CHUNK
    nh = GLA_HEADS_PER_STEP
    wk, wv = nh * GLA_DK, nh * GLA_DV
    wg = jnp.zeros((LANES, GLA_HEADS * GLA_DK), BF16).at[:GLA_RANK].set(w_gate.astype(BF16))
    return pl.pallas_call(
        functools.partial(_gla_kernel, n_chunks=n_chunks, n_heads=nh),
        out_shape=jax.ShapeDtypeStruct((n_tok, MIX_W), BF16),
        grid=(bsz, GLA_HEADS // nh),
        in_specs=[
            pl.BlockSpec((seq, wk), lambda b, h: (b, OFF_GL_Q // wk + h)),
            pl.BlockSpec((seq, wk), lambda b, h: (b, OFF_GL_K // wk + h)),
            pl.BlockSpec((seq, wv), lambda b, h: (b, OFF_GL_V // wv + h)),
            pl.BlockSpec((seq, wv), lambda b, h: (b, OFF_GL_R // wv + h)),
            pl.BlockSpec((seq, LANES), lambda b, h: (b, 0)),
            pl.BlockSpec((LANES, wk), lambda b, h: (0, h)),
            pl.BlockSpec((1, wk), lambda b, h: (0, h)),
            pl.BlockSpec((1, wv), lambda b, h: (0, h)),
        ],
        out_specs=pl.BlockSpec((seq, wv), lambda b, h: (b, h)),
        scratch_shapes=[pltpu.VMEM((nh, GLA_DV, GLA_DK), F32),
                        pltpu.VMEM((nh, GLA_CHUNK, GLA_CHUNK), F32),
                        pltpu.VMEM((nh, GLA_CHUNK, GLA_DK), F32)],
        compiler_params=_cparams(("parallel", "parallel")),
        name="gla",
    )(proj, proj, proj, proj, lr, wg, b_gate.reshape(1, -1), norm_g.reshape(1, -1))


def _merge_kernel(pc_ref, ps_ref, pb_ref, pg_ref, gc_ref, gs_ref, gb_ref, gg_ref,
                  wc_ref, wv_ref, wt_ref, wb_ref, wl_ref, o_ref):
    def mm(a_ref, w_ref):
        return jnp.dot(a_ref[...], w_ref[0], preferred_element_type=F32)

    def gate(ref):
        return _sigmoid(ref[...].astype(F32))

    y_s5 = mm(ps_ref, wv_ref) * _sigmoid(mm(ps_ref, wt_ref))
    merged = (gate(gc_ref) * mm(pc_ref, wc_ref) + gate(gs_ref) * y_s5
              + gate(gb_ref) * mm(pb_ref, wb_ref) + gate(gg_ref) * mm(pg_ref, wl_ref))
    o_ref[...] = merged.astype(BF16)


def _merge(pre_conv, pre_s5, pre_sb, pre_gla, proj, w_conv, w_val, w_gate, w_sb, w_gla, layer, tm, tn):
    n_tok = proj.shape[0]
    d = w_conv.shape[2]
    pre = pl.BlockSpec((tm, MIX_W), lambda i, j: (i, 0))
    gcol = lambda off: pl.BlockSpec((tm, tn), lambda i, j: (i, off // tn + j))
    wcol = pl.BlockSpec((1, MIX_W, tn), lambda i, j: (layer, 0, j))
    return pl.pallas_call(
        _merge_kernel,
        out_shape=jax.ShapeDtypeStruct((n_tok, d), BF16),
        grid=(n_tok // tm, d // tn),
        in_specs=[pre, pre, pre, pre,
                  gcol(OFF_G_CONV), gcol(OFF_G_S5), gcol(OFF_G_SB), gcol(OFF_G_GLA),
                  wcol, wcol, wcol, wcol, wcol],
        out_specs=pl.BlockSpec((tm, tn), lambda i, j: (i, j)),
        compiler_params=_cparams(("parallel", "arbitrary")),
        name="branch_merge",
    )(pre_conv, pre_s5, pre_sb, pre_gla, proj, proj, proj, proj, w_conv, w_val, w_gate, w_sb, w_gla)


def _out_proj_kernel(m_ref, x_ref, gt_ref, wo_ref, lg_ref, lb_ref, o_ref, *, alpha):
    for r in range(m_ref.shape[0] // OUT_ROW_CHUNK):
        sl = slice(r * OUT_ROW_CHUNK, (r + 1) * OUT_ROW_CHUNK)
        y = jnp.dot(m_ref[sl, :], wo_ref[0], preferred_element_type=F32)
        z = alpha * x_ref[sl, :] + gt_ref[0] * y
        o_ref[sl, :] = _layer_norm_rows(z) * lg_ref[0] + lb_ref[0]


def _out_proj_ln(merged, x, gt, w_o, ln_g, ln_b, layer, seq, alpha, tm):
    n_tok, d = x.shape
    per_seq = seq // tm
    vec = pl.BlockSpec((1, 1, d), lambda i: (layer, 0, 0))
    return pl.pallas_call(
        functools.partial(_out_proj_kernel, alpha=alpha),
        out_shape=jax.ShapeDtypeStruct((n_tok, d), F32),
        grid=(n_tok // tm,),
        in_specs=[pl.BlockSpec((tm, d), lambda i: (i, 0)),
                  pl.BlockSpec((tm, d), lambda i: (i, 0)),
                  pl.BlockSpec((1, 1, d), lambda i: (i // per_seq, 0, 0)),
                  pl.BlockSpec((1, d, d), lambda i: (layer, 0, 0)),
                  vec, vec],
        out_specs=pl.BlockSpec((tm, d), lambda i: (i, 0)),
        compiler_params=_cparams(("parallel",)),
        name="out_proj_ln",
    )(merged, x, gt, w_o, ln_g.reshape(-1, 1, d), ln_b.reshape(-1, 1, d))


def _ffn_kernel(x_ref, sc_ref, sh_ref, gt_ref, wg_ref, wu_ref, wd_ref, lg_ref, lb_ref, o_ref,
                h_ref, acc_ref, *, alpha):
    j = pl.program_id(1)

    @pl.when(j == 0)
    def _():
        _ln_modulate_into(h_ref, x_ref, sc_ref, sh_ref)
        acc_ref[...] = jnp.zeros_like(acc_ref)

    h = h_ref[...]
    gate = jnp.dot(h, wg_ref[0], preferred_element_type=F32)
    up = jnp.dot(h, wu_ref[0], preferred_element_type=F32)
    act = (gate * _sigmoid(gate) * up).astype(BF16)
    acc_ref[...] += jnp.dot(act, wd_ref[0], preferred_element_type=F32)

    @pl.when(j == pl.num_programs(1) - 1)
    def _():
        _residual_ln_into(o_ref, x_ref, acc_ref, gt_ref, lg_ref, lb_ref, alpha)


def _ffn(x, sc, sh, gt, w_gate, w_up, w_down, ln_g, ln_b, layer, seq, alpha, tm, tf):
    n_tok, d = x.shape
    d_ff = w_gate.shape[2]
    per_seq = seq // tm
    mod = pl.BlockSpec((1, 1, d), lambda i, j: (i // per_seq, 0, 0))
    vec = pl.BlockSpec((1, 1, d), lambda i, j: (layer, 0, 0))
    return pl.pallas_call(
        functools.partial(_ffn_kernel, alpha=alpha),
        out_shape=jax.ShapeDtypeStruct((n_tok, d), F32),
        grid=(n_tok // tm, d_ff // tf),
        in_specs=[pl.BlockSpec((tm, d), lambda i, j: (i, 0)), mod, mod, mod,
                  pl.BlockSpec((1, d, tf), lambda i, j: (layer, 0, j)),
                  pl.BlockSpec((1, d, tf), lambda i, j: (layer, 0, j)),
                  pl.BlockSpec((1, tf, d), lambda i, j: (layer, j, 0)),
                  vec, vec],
        out_specs=pl.BlockSpec((tm, d), lambda i, j: (i, 0)),
        scratch_shapes=[pltpu.VMEM((tm, d), BF16), pltpu.VMEM((tm, d), F32)],
        compiler_params=_cparams(("parallel", "arbitrary")),
        name="ffn_swiglu_ln",
    )(x, sc, sh, gt, w_gate, w_up, w_down, ln_g.reshape(-1, 1, d), ln_b.reshape(-1, 1, d))


def kernel(x, c, ada_w, ada_b, w_in, conv_w, w_conv_out, s5_lam_re, s5_lam_im, s5_b_re, s5_b_im, s5_c_re, s5_c_im, s5_d, s5_log_dt, w_s5_val, w_s5_gate, w_sb_out, gla_w_gate, gla_b_gate, gla_norm_g, w_gla_out, w_o, ln1_g, ln1_b, ffn_w_gate, ffn_w_up, ffn_w_down, ln2_g, ln2_b):
    bsz, seq, d = x.shape
    depth = ada_w.shape[0]
    alpha = (2.0 * depth) ** 0.25
    mod = _ada_mod(c, ada_w, ada_b)
    xf = x.reshape(bsz * seq, d)
    bf = lambda a: a.astype(BF16)
    t = _tiles(seq)
    w_in_parts = _w_in_prep(jnp.swapaxes(w_in, 1, 2), t["prep_tn"])
    merge_w = (bf(w_conv_out), bf(_channel_major(w_s5_val, 1)), bf(_channel_major(w_s5_gate, 1)),
               bf(w_sb_out), bf(w_gla_out))
    w_o_bf = bf(w_o)
    ffn_w = (bf(ffn_w_gate), bf(ffn_w_up), bf(ffn_w_down))
    s5_params = jax.vmap(_s5_params)(s5_lam_re, s5_lam_im, s5_b_re, s5_b_im, s5_c_re, s5_c_im, s5_d,
                                     s5_log_dt)
    for l in range(depth):
        sh1, sc1, gt1, sh2, sc2, gt2 = [m.reshape(bsz, 1, d) for m in jnp.split(mod[l], 6, axis=-1)]
        proj, lr = _inproj(xf, sc1, sh1, *w_in_parts, l, seq, t["inproj_tm"], t["inproj_tn"])
        pre_conv = _short_conv(proj, conv_w[l], seq, t["conv_tc"])
        pre_s5 = _s5_gelu(proj, [p[l] for p in s5_params], bsz, seq)
        pre_sb = _stick_breaking(proj, bsz, seq, t["sb_tq"])
        pre_gla = _gla(proj, lr, gla_w_gate[l], gla_b_gate[l], gla_norm_g[l], bsz, seq)
        merged = _merge(pre_conv, pre_s5, pre_sb, pre_gla, proj, *merge_w, l, t["merge_tm"], t["merge_tn"])
        xf = _out_proj_ln(merged, xf, gt1, w_o_bf, ln1_g, ln1_b, l, seq, alpha, t["out_tm"])
        xf = _ffn(xf, sc2, sh2, gt2, *ffn_w, ln2_g, ln2_b, l, seq, alpha, t["ffn_tm"], t["ffn_tf"])
    return xf.reshape(bsz, seq, d)
```

```python
import functools
import math

import jax
import jax.numpy as jnp
from jax import lax
from jax.experimental import pallas as pl
from jax.experimental.pallas import tpu as pltpu

F32 = jnp.float32
BF16 = jnp.bfloat16

LN_EPS = 1e-5
LOG2_E = math.log2(math.e)
LANES = 128
SUBLANES = 8
BF16_ROWS = 16
MIX_W = 1024
S5_GROUP = 16
S5_STATE = 64
S5_CHUNK = 16
S5_GROUPS_PER_STEP = 4
SB_HEADS = 8
SB_DH = 128
SB_HEADS_PER_STEP = 4
GLA_HEADS = 4
GLA_DK = 128
GLA_DV = 256
GLA_RANK = 16
GLA_TAU = 16.0
GLA_CHUNK = 128
GLA_SUB = 16
GLA_HEADS_PER_STEP = 4
VMEM_LIMIT = 52 * 1024 * 1024
SB_SKIP = 90.0

OFF_CV_B, OFF_CV_C, OFF_CV_X = 0, 1024, 2048
OFF_SB_Q, OFF_SB_K, OFF_SB_V = 3072, 4096, 5120
OFF_GL_Q, OFF_GL_K, OFF_GL_V, OFF_GL_R = 6144, 6656, 7168, 8192
OFF_S5_U = 9216
MAIN_W = 10240
OFF_G_CONV, OFF_G_S5, OFF_G_SB, OFF_G_GLA = 10240, 12288, 14336, 16384
PROJ_W = 18432
SRC_S5_U = 3072
SRC_GL_LR = 10240
SRC_GATES = 10256


def _tiles(seq):
    return dict(ada_tn=1024, prep_tn=1024, inproj_tm=min(1024, seq), inproj_tn=2048,
                conv_tc=min(1024, seq), sb_tq=min(256, seq),
                merge_tm=min(1024, seq), merge_tn=512, out_tm=min(512, seq),
                ffn_tm=min(512, seq), ffn_tf=512)


def _cparams(sem):
    return pltpu.CompilerParams(dimension_semantics=sem, vmem_limit_bytes=VMEM_LIMIT)


def _layer_norm_rows(x):
    mu = jnp.mean(x, axis=-1, keepdims=True)
    xc = x - mu
    var = jnp.mean(xc * xc, axis=-1, keepdims=True)
    return xc * lax.rsqrt(var + LN_EPS)


OUT_ROW_CHUNK = 256
ROW_CHUNK = 256


def _for_row_chunks(n_rows, fn):
    chunk = min(ROW_CHUNK, n_rows)

    def body(r, _):
        fn(pl.ds(pl.multiple_of(r * chunk, chunk), chunk))
        return 0

    lax.fori_loop(0, n_rows // chunk, body, 0)


def _ln_modulate_into(h_ref, x_ref, sc_ref, sh_ref):
    def rows(sl):
        h = _layer_norm_rows(x_ref[sl, :]) * (1.0 + sc_ref[0]) + sh_ref[0]
        h_ref[sl, :] = h.astype(BF16)

    _for_row_chunks(x_ref.shape[0], rows)


def _residual_ln_into(o_ref, x_ref, y_ref, gt_ref, g_ref, b_ref, alpha):
    def rows(sl):
        z = alpha * x_ref[sl, :] + gt_ref[0] * y_ref[sl, :]
        o_ref[sl, :] = _layer_norm_rows(z) * g_ref[0] + b_ref[0]

    _for_row_chunks(x_ref.shape[0], rows)


def _sigmoid(x):
    return 1.0 / (1.0 + jnp.exp(-x))


def _log_sigmoid(x):
    return jnp.minimum(x, 0.0) - jnp.log(1.0 + jnp.exp(-jnp.abs(x)))


def _ada_kernel(c_ref, w_ref, b_ref, o_ref):
    c = c_ref[...]
    s = (c * _sigmoid(c)).astype(BF16)
    o_ref[0] = jnp.dot(s, w_ref[0].astype(BF16), preferred_element_type=F32) + b_ref[0]


def _ada_mod(c, ada_w, ada_b, tn):
    depth, d, n = ada_w.shape
    bsz = c.shape[0]
    rows = -(-bsz // SUBLANES) * SUBLANES
    c_pad = jnp.zeros((rows, d), F32).at[:bsz].set(c)
    out = pl.pallas_call(
        _ada_kernel,
        out_shape=jax.ShapeDtypeStruct((depth, rows, n), F32),
        grid=(depth, n // tn),
        in_specs=[
            pl.BlockSpec((rows, d), lambda l, j: (0, 0)),
            pl.BlockSpec((1, d, tn), lambda l, j: (l, 0, j)),
            pl.BlockSpec((1, 1, tn), lambda l, j: (l, 0, j)),
        ],
        out_specs=pl.BlockSpec((1, rows, tn), lambda l, j: (l, 0, j)),
        compiler_params=_cparams(("parallel", "parallel")),
        name="ada_mod",
    )(c_pad, ada_w, ada_b.reshape(depth, 1, n))
    return out[:, :bsz]


def _w_in_prep_kernel(a_ref, l_ref, p_ref, o_ref, lr_ref, *, n_direct):
    j = pl.program_id(1)

    @pl.when(j == 0)
    def _():
        row = lax.broadcasted_iota(jnp.int32, l_ref.shape[1:], 0)
        lr_ref[0] = jnp.where(row < GLA_RANK, l_ref[0], 0.0).T.astype(BF16)

    @pl.when(j != n_direct)
    def _():
        o_ref[0] = a_ref[0].T.astype(BF16)

    @pl.when(j == n_direct)
    def _():
        rows = jnp.dot(p_ref[...], a_ref[0].astype(BF16), preferred_element_type=F32)
        o_ref[0] = rows.T.astype(BF16)


def _w_in_prep(w_in_t, tn):
    depth, _, d = w_in_t.shape
    n_a = SRC_S5_U // tn
    n_direct = (MAIN_W - MIX_W) // tn
    n_gate = (PROJ_W - MAIN_W) // tn
    dst = jnp.arange(MIX_W)
    perm = (dst[:, None] == (dst[None, :] % S5_GROUP) * (MIX_W // S5_GROUP) + dst[None, :] // S5_GROUP
            ).astype(BF16)

    def src_row(j):
        direct = jnp.where(j < n_a, j * tn, j * tn + MIX_W)
        row = jnp.where(j < n_direct, direct,
                        jnp.where(j == n_direct, SRC_S5_U, SRC_GATES + (j - n_direct - 1) * tn))
        return pl.multiple_of(row, math.gcd(SRC_GATES, tn))

    return pl.pallas_call(
        functools.partial(_w_in_prep_kernel, n_direct=n_direct),
        out_shape=(jax.ShapeDtypeStruct((depth, d, PROJ_W), BF16),
                   jax.ShapeDtypeStruct((depth, d, LANES), BF16)),
        grid=(depth, n_direct + 1 + n_gate),
        in_specs=[
            pl.BlockSpec((pl.Element(1), pl.Element(tn), pl.Element(d)), lambda l, j: (l, src_row(j), 0)),
            pl.BlockSpec((pl.Element(1), pl.Element(LANES), pl.Element(d)), lambda l, j: (l, SRC_GL_LR, 0)),
            pl.BlockSpec((MIX_W, MIX_W), lambda l, j: (0, 0)),
        ],
        out_specs=(pl.BlockSpec((1, d, tn), lambda l, j: (l, 0, j)),
                   pl.BlockSpec((1, d, LANES), lambda l, j: (l, 0, 0))),
        compiler_params=_cparams(("parallel", "arbitrary")),
        name="w_in_prep",
    )(w_in_t, w_in_t, perm)


def _inproj_kernel(x_ref, sc_ref, sh_ref, w_ref, wl_ref, o_ref, lr_ref, h_ref):
    @pl.when(pl.program_id(1) == 0)
    def _():
        _ln_modulate_into(h_ref, x_ref, sc_ref, sh_ref)
        lr_ref[...] = jnp.dot(h_ref[...], wl_ref[0], preferred_element_type=F32).astype(BF16)

    o_ref[...] = jnp.dot(h_ref[...], w_ref[0], preferred_element_type=F32).astype(BF16)


def _inproj(x, sc, sh, w_all, w_lr, layer, seq, tm, tn):
    n_tok, d = x.shape
    width = w_all.shape[2]
    per_seq = seq // tm
    mod = pl.BlockSpec((1, 1, d), lambda i, j: (i // per_seq, 0, 0))
    return pl.pallas_call(
        _inproj_kernel,
        out_shape=(jax.ShapeDtypeStruct((n_tok, width), BF16),
                   jax.ShapeDtypeStruct((n_tok, LANES), BF16)),
        grid=(n_tok // tm, width // tn),
        in_specs=[
            pl.BlockSpec((tm, d), lambda i, j: (i, 0)),
            mod, mod,
            pl.BlockSpec((1, d, tn), lambda i, j: (layer, 0, j)),
            pl.BlockSpec((1, d, LANES), lambda i, j: (layer, 0, 0)),
        ],
        out_specs=(pl.BlockSpec((tm, tn), lambda i, j: (i, j)),
                   pl.BlockSpec((tm, LANES), lambda i, j: (i, 0))),
        scratch_shapes=[pltpu.VMEM((tm, d), BF16)],
        compiler_params=_cparams(("parallel", "arbitrary")),
        name="ln_mod_inproj",
    )(x, sc, sh, w_all, w_lr)


def _conv_kernel(b_ref, c_ref, x_ref, cp_ref, xp_ref, w_ref, o_ref, *, per_seq):
    h = c_ref[...].astype(F32) * x_ref[...].astype(F32)
    hp = cp_ref[...].astype(F32) * xp_ref[...].astype(F32)
    first = (pl.program_id(0) % per_seq) == 0
    hp = jnp.where(first, 0.0, hp)
    row = lax.broadcasted_iota(jnp.int32, h.shape, 0)
    rows_p = hp.shape[0]
    h1 = jnp.where(row == 0, hp[rows_p - 1:rows_p], pltpu.roll(h, 1, 0))
    h2 = pltpu.roll(h, 2, 0)
    h2 = jnp.where(row == 0, hp[rows_p - 2:rows_p - 1], h2)
    h2 = jnp.where(row == 1, hp[rows_p - 1:rows_p], h2)
    w = w_ref[...]
    y = w[0:1] * h2 + w[1:2] * h1 + w[2:3] * h
    o_ref[...] = (b_ref[...].astype(F32) * y).astype(BF16)


def _short_conv(proj, conv_w, seq, tc):
    n_tok = proj.shape[0]
    halo = BF16_ROWS
    per_seq = seq // tc
    ratio = tc // halo

    def prev(i):
        return jnp.maximum(i * ratio - 1, 0)

    return pl.pallas_call(
        functools.partial(_conv_kernel, per_seq=per_seq),
        out_shape=jax.ShapeDtypeStruct((n_tok, MIX_W), BF16),
        grid=(n_tok // tc,),
        in_specs=[
            pl.BlockSpec((tc, MIX_W), lambda i: (i, OFF_CV_B // MIX_W)),
            pl.BlockSpec((tc, MIX_W), lambda i: (i, OFF_CV_C // MIX_W)),
            pl.BlockSpec((tc, MIX_W), lambda i: (i, OFF_CV_X // MIX_W)),
            pl.BlockSpec((halo, MIX_W), lambda i: (prev(i), OFF_CV_C // MIX_W)),
            pl.BlockSpec((halo, MIX_W), lambda i: (prev(i), OFF_CV_X // MIX_W)),
            pl.BlockSpec((SUBLANES, MIX_W), lambda i: (0, 0)),
        ],
        out_specs=pl.BlockSpec((tc, MIX_W), lambda i: (i, 0)),
        compiler_params=_cparams(("parallel",)),
        name="short_conv",
    )(proj, proj, proj, proj, proj, jnp.zeros((SUBLANES, MIX_W), F32).at[:conv_w.shape[0]].set(conv_w))


def _cmul(xr, xi, yr, yi):
    return xr * yr - xi * yi, xr * yi + xi * yr


def _complex_powers(ar, ai, exponent, shape):
    wr, wi = jnp.ones(shape, F32), jnp.zeros(shape, F32)
    for bit in range(4):
        on = ((exponent >> bit) & 1) == 1
        wr, wi = _cmul(wr, wi, jnp.where(on, ar, 1.0), jnp.where(on, ai, 0.0))
        ar, ai = _cmul(ar, ai, ar, ai)
    return wr, wi, ar, ai


def _s5_chunk_matrices(bt_re, bt_im, ct_re, ct_im, ac_re, ac_im, arow_re, arow_im):
    t, hh, p = S5_CHUNK, S5_GROUP, S5_STATE
    hi = lax.Precision.HIGHEST
    lane_tau = lax.broadcasted_iota(jnp.int32, (p, t * hh), 1) // hh
    wr, wi, _, _ = _complex_powers(ac_re, ac_im, lane_tau, (p, t * hh))
    cw_re, cw_im = _cmul(ct_re, ct_im, wr, wi)
    strip = (jnp.dot(bt_re, cw_re, precision=hi, preferred_element_type=F32)
             - jnp.dot(bt_im, cw_im, precision=hi, preferred_element_type=F32))
    lane = lax.broadcasted_iota(jnp.int32, (hh, t * hh), 1)
    m_intra = jnp.concatenate(
        [strip if tp == 0 else jnp.where(lane >= tp * hh, pltpu.roll(strip, tp * hh, 1), 0.0)
         for tp in range(t)], axis=0)
    mo_re, mo_im = _cmul(cw_re, cw_im, ac_re, ac_im)
    both = lambda a: jnp.concatenate([a, a], axis=1)
    low = lax.broadcasted_iota(jnp.int32, (t, 2 * p), 1) < p
    row_tau = (t - 1) - lax.broadcasted_iota(jnp.int32, (t, 2 * p), 0)
    rr, ri, a16_re, a16_im = _complex_powers(both(arow_re), both(arow_im), row_tau, (t, 2 * p))
    r_same, r_cross = jnp.where(low, rr, ri), jnp.where(low, -ri, rr)
    b_re2, b_im2 = both(bt_re), both(bt_im)
    ms = jnp.concatenate([b_re2 * r_same[tp:tp + 1] + b_im2 * r_cross[tp:tp + 1] for tp in range(t)],
                         axis=0)
    mo = jnp.concatenate([mo_re, -mo_im], axis=0)
    return m_intra.astype(BF16), ms.astype(BF16), mo.astype(BF16), a16_re, a16_im


def _s5_kernel(u_ref, btr_ref, bti_ref, ctr_ref, cti_ref, acr_ref, aci_ref, arr_ref, ari_ref, d_ref,
               o_ref, *, n_chunks):
    for gi in range(u_ref.shape[0]):
        _s5_group(gi, u_ref, btr_ref, bti_ref, ctr_ref, cti_ref, acr_ref, aci_ref, arr_ref, ari_ref, d_ref,
                  o_ref, n_chunks)


def _s5_group(gi, u_ref, btr_ref, bti_ref, ctr_ref, cti_ref, acr_ref, aci_ref, arr_ref, ari_ref, d_ref,
              o_ref, n_chunks):
    u = u_ref[gi]
    m_intra, ms, mo, ar, ai = _s5_chunk_matrices(
        btr_ref[gi], bti_ref[gi], ctr_ref[gi], cti_ref[gi], acr_ref[gi], aci_ref[gi], arr_ref[gi],
        ari_ref[gi])
    s = jnp.dot(u, ms, preferred_element_type=F32)
    chunk = lax.broadcasted_iota(jnp.int32, s.shape, 0) % n_chunks
    low = lax.broadcasted_iota(jnp.int32, ar.shape, 1) < S5_STATE

    def shifted(a, rows):
        return jnp.where(chunk >= rows, pltpu.roll(a, rows, 0), 0.0)

    step = 1
    while step < n_chunks:
        prev = shifted(s, step)
        s = s + ar * prev + jnp.where(low, -ai, ai) * pltpu.roll(prev, S5_STATE, 1)
        ar, ai = ar * ar - ai * ai, 2.0 * ar * ai
        step *= 2
    y = jnp.dot(u, m_intra, preferred_element_type=F32)
    y += jnp.dot(shifted(s, 1).astype(BF16), mo, preferred_element_type=F32)
    y += d_ref[gi] * u.astype(F32)
    inner = math.sqrt(2.0 / math.pi) * (y + 0.044715 * (y * y * y))
    o_ref[gi] = (0.5 * y * (1.0 + jnp.tanh(inner))).astype(BF16)


def _s5_params(lam_re, lam_im, b_re, b_im, c_re, c_im, d_skip, log_dt):
    g, p = lam_re.shape
    hh = S5_GROUP
    t = S5_CHUNK
    dt = jnp.exp(log_dt)[:, None]
    mag = jnp.exp(lam_re * dt)
    ang = lam_im * dt
    ab_re, ab_im = mag * jnp.cos(ang), mag * jnp.sin(ang)
    den = lam_re * lam_re + lam_im * lam_im
    nr, ni = ab_re - 1.0, ab_im
    f_re = (nr * lam_re + ni * lam_im) / den
    f_im = (ni * lam_re - nr * lam_im) / den
    bb_re = f_re[..., None] * b_re - f_im[..., None] * b_im
    bb_im = f_re[..., None] * b_im + f_im[..., None] * b_re
    bt_re, bt_im = bb_re.transpose(0, 2, 1), bb_im.transpose(0, 2, 1)
    tile = lambda c: jnp.tile(c.transpose(0, 2, 1), (1, 1, t))
    d_row = jnp.tile(d_skip.reshape(g, 1, hh), (1, 1, t))
    return (bt_re, bt_im, tile(c_re), tile(c_im), ab_re[:, :, None], ab_im[:, :, None],
            ab_re[:, None, :], ab_im[:, None, :], d_row)


def _s5_gelu(proj, params, bsz, seq):
    g = params[0].shape[0]
    t, hh, p = S5_CHUNK, S5_GROUP, S5_STATE
    n_chunks = seq // t
    rows = n_chunks * bsz
    u = proj[:, OFF_S5_U:OFF_S5_U + MIX_W].reshape(rows * t * hh, g)
    u = u.T.reshape(g, rows, t * hh)
    grp = lambda shape: pl.BlockSpec((S5_GROUPS_PER_STEP,) + shape, lambda i: (i, 0, 0))
    y = pl.pallas_call(
        functools.partial(_s5_kernel, n_chunks=n_chunks),
        out_shape=jax.ShapeDtypeStruct((g, rows, t * hh), BF16),
        grid=(g // S5_GROUPS_PER_STEP,),
        in_specs=[grp((rows, t * hh)), grp((hh, p)), grp((hh, p)), grp((p, t * hh)), grp((p, t * hh)),
                  grp((p, 1)), grp((p, 1)), grp((1, p)), grp((1, p)), grp((1, t * hh))],
        out_specs=grp((rows, t * hh)),
        compiler_params=_cparams(("parallel",)),
        name="s5_chunked",
    )(u, *params)
    return y.reshape(g, rows * t * hh).T.reshape(bsz * seq, hh * g)


def _channel_major(w, axis):
    g = w.shape[axis] // S5_GROUP
    shape = w.shape[:axis] + (g, S5_GROUP) + w.shape[axis + 1:]
    return jnp.swapaxes(w.reshape(shape), axis, axis + 1).reshape(w.shape)


def _sb_kernel(q_ref, k_ref, v_ref, o_ref, *, tq, n_heads):
    i = pl.program_id(2)
    scale = SB_DH ** -0.5 * LOG2_E
    row = lax.broadcasted_iota(jnp.int32, (tq, tq), 0)
    col = lax.broadcasted_iota(jnp.int32, (tq, tq), 1)
    below = row > col
    tri = jnp.where(below, 1.0, 0.0).astype(BF16)

    def logits(cols, j, diag):
        rows = pl.ds(pl.multiple_of(j * tq, tq), tq)
        z = lax.dot_general(q_ref[:, cols], k_ref[rows, cols], (((1,), (1,)), ((), ())),
                            preferred_element_type=F32) * scale
        lp = jnp.log2(1.0 + jnp.exp2(-jnp.abs(z)))
        log_beta = jnp.minimum(z, 0.0) - lp
        log_1m = log_beta - z
        if diag:
            log_1m = jnp.where(below, log_1m, 0.0)
        local = jnp.dot(log_1m.astype(BF16), tri, preferred_element_type=F32)
        return log_beta, local, jnp.sum(log_1m, axis=-1, keepdims=True), v_ref[rows, cols]

    def weigh(parts, carry, diag):
        log_beta, local, total, vj = parts
        w = jnp.exp2(log_beta + local + carry)
        if diag:
            w = jnp.where(below, w, 0.0)
        return jnp.dot(w.astype(BF16), vj, preferred_element_type=F32), carry + total

    has_prev = i > 0
    heads = [slice(hd * SB_DH, (hd + 1) * SB_DH) for hd in range(n_heads)]
    started = []
    for cols in heads:
        d_parts = logits(cols, i, True)
        p_parts = logits(cols, jnp.maximum(i - 1, 0), False)
        acc, carry = weigh(d_parts, jnp.zeros((tq, 1), F32), True)
        p_acc, p_carry = weigh(p_parts, carry, False)
        started.append((jnp.where(has_prev, p_carry, carry), jnp.where(has_prev, acc + p_acc, acc)))

    def cond(state):
        j, top = state[0], state[1]
        return jnp.logical_and(j >= 0, top > -SB_SKIP * LOG2_E)

    for cols, (carry, acc) in zip(heads, started):
        def body(state, cols=cols):
            j, _, carry, acc = state
            d_acc, carry = weigh(logits(cols, j, False), carry, False)
            return j - 1, jnp.max(carry), carry, acc + d_acc

        state = lax.while_loop(cond, body, (i - 2, jnp.max(carry), carry, acc))
        o_ref[:, cols] = state[3].astype(BF16)


def _stick_breaking(proj, bsz, seq, tq):
    n_tok = proj.shape[0]
    nq = seq // tq
    nh = SB_HEADS_PER_STEP
    width = nh * SB_DH
    return pl.pallas_call(
        functools.partial(_sb_kernel, tq=tq, n_heads=nh),
        out_shape=jax.ShapeDtypeStruct((n_tok, MIX_W), BF16),
        grid=(bsz, SB_HEADS // nh, nq),
        in_specs=[
            pl.BlockSpec((tq, width), lambda b, h, i: (b * nq + i, OFF_SB_Q // width + h)),
            pl.BlockSpec((seq, width), lambda b, h, i: (b, OFF_SB_K // width + h)),
            pl.BlockSpec((seq, width), lambda b, h, i: (b, OFF_SB_V // width + h)),
        ],
        out_specs=pl.BlockSpec((tq, width), lambda b, h, i: (b * nq + i, h)),
        compiler_params=_cparams(("parallel", "parallel", "arbitrary")),
        name="stick_breaking",
    )(proj, proj, proj)


def _split3(x):
    hi = x.astype(BF16)
    r1 = x - hi.astype(F32)
    mid = r1.astype(BF16)
    lo = (r1 - mid.astype(F32)).astype(BF16)
    return hi, mid, lo


def _gla_kernel(q_ref, k_ref, v_ref, r_ref, lr_ref, wg_ref, bg_ref, g_ref, o_ref, st_ref, sc_ref, kd_ref,
                *, n_chunks, n_heads):
    c_len, sub = GLA_CHUNK, GLA_SUB
    st_ref[...] = jnp.zeros_like(st_ref)
    row = lax.broadcasted_iota(jnp.int32, (c_len, c_len), 0)
    col = lax.broadcasted_iota(jnp.int32, (c_len, c_len), 1)
    tri_incl = jnp.where(col <= row, 1.0, 0.0).astype(BF16)
    col_s = lax.broadcasted_iota(jnp.int32, (sub, c_len), 1)
    row_s = lax.broadcasted_iota(jnp.int32, (sub, GLA_DK), 0)
    nt = (((1,), (1,)), ((), ()))

    def head_chunk(hd, rows):
        kcols = slice(hd * GLA_DK, (hd + 1) * GLA_DK)
        vcols = slice(hd * GLA_DV, (hd + 1) * GLA_DV)
        q = q_ref[rows, kcols].astype(F32) * (GLA_DK ** -0.5)
        k = k_ref[rows, kcols].astype(F32)
        v = v_ref[rows, vcols]
        pre = jnp.dot(lr_ref[rows, :], wg_ref[:, kcols], preferred_element_type=F32) + bg_ref[:, kcols]
        la = _log_sigmoid(pre) * (1.0 / GLA_TAU)
        b = sum(jnp.dot(tri_incl, part, preferred_element_type=F32) for part in _split3(la))
        b = b * LOG2_E
        bx = b - la * LOG2_E
        b_end = b[c_len - 1:c_len]
        st = st_ref[hd]
        qe = (q * jnp.exp2(b)).astype(BF16)
        o = lax.dot_general(qe, st.astype(BF16), nt, preferred_element_type=F32)
        kd_ref[hd] = jnp.zeros((c_len, GLA_DK), F32)
        for i in range(c_len // sub):
            lo_row = i * sub
            sl = slice(lo_row, lo_row + sub)
            bi, qi, ki = b[sl], q[sl], k[sl]
            ref_i = bx[lo_row:lo_row + 1]
            if i == 0:
                sc = jnp.zeros((sub, c_len), F32)
            else:
                pl_row = lo_row - sub
                ps = slice(pl_row, lo_row)
                if pl_row:
                    kd_ref[hd, :pl_row, :] = kd_ref[hd, :pl_row, :] * jnp.exp2(ref_i - bx[pl_row:pl_row + 1])
                kd_ref[hd, ps, :] = k[ps] * jnp.exp2(ref_i - b[ps])
                qd = (qi * jnp.exp2(bi - ref_i)).astype(BF16)
                sc = lax.dot_general(qd, kd_ref[hd].astype(BF16), nt, preferred_element_type=F32)
            for s in range(sub):
                dec = jnp.exp2(jnp.where(row_s >= s, bi - bi[s:s + 1], -1e30))
                cs = jnp.sum(qi * (ki[s:s + 1] * dec), axis=-1, keepdims=True)
                sc = jnp.where(col_s == lo_row + s, cs, sc)
            sc_ref[hd, sl, :] = sc
        o = o + jnp.dot(sc_ref[hd].astype(BF16), v, preferred_element_type=F32)
        ke = (k * jnp.exp2(b_end - b)).astype(BF16)
        upd = lax.dot_general(v, ke, (((0,), (0,)), ((), ())), preferred_element_type=F32)
        st_ref[hd] = st * jnp.exp2(b_end) + upd
        on = _layer_norm_rows(o) * g_ref[:, vcols]
        r = r_ref[rows, vcols].astype(F32)
        o_ref[rows, vcols] = (on * (r * _sigmoid(r))).astype(BF16)

    def chunk(c, _):
        rows = pl.ds(pl.multiple_of(c * c_len, c_len), c_len)
        for hd in range(n_heads):
            head_chunk(hd, rows)
        return 0

    lax.fori_loop(0, n_chunks, chunk, 0)


def _gla(proj, lr, w_gate, b_gate, norm_g, bsz, seq):
    n_tok = proj.shape[0]
    n_chunks = seq // GLA_CHUNK
    nh = GLA_HEADS_PER_STEP
    wk, wv = nh * GLA_DK, nh * GLA_DV
    wg = jnp.zeros((LANES, GLA_HEADS * GLA_DK), BF16).at[:GLA_RANK].set(w_gate.astype(BF16))
    return pl.pallas_call(
        functools.partial(_gla_kernel, n_chunks=n_chunks, n_heads=nh),
        out_shape=jax.ShapeDtypeStruct((n_tok, MIX_W), BF16),
        grid=(bsz, GLA_HEADS // nh),
        in_specs=[
            pl.BlockSpec((seq, wk), lambda b, h: (b, OFF_GL_Q // wk + h)),
            pl.BlockSpec((seq, wk), lambda b, h: (b, OFF_GL_K // wk + h)),
            pl.BlockSpec((seq, wv), lambda b, h: (b, OFF_GL_V // wv + h)),
            pl.BlockSpec((seq, wv), lambda b, h: (b, OFF_GL_R // wv + h)),
            pl.BlockSpec((seq, LANES), lambda b, h: (b, 0)),
            pl.BlockSpec((LANES, wk), lambda b, h: (0, h)),
            pl.BlockSpec((1, wk), lambda b, h: (0, h)),
            pl.BlockSpec((1, wv), lambda b, h: (0, h)),
        ],
        out_specs=pl.BlockSpec((seq, wv), lambda b, h: (b, h)),
        scratch_shapes=[pltpu.VMEM((nh, GLA_DV, GLA_DK), F32),
                        pltpu.VMEM((nh, GLA_CHUNK, GLA_CHUNK), F32),
                        pltpu.VMEM((nh, GLA_CHUNK, GLA_DK), F32)],
        compiler_params=_cparams(("parallel", "parallel")),
        name="gla",
    )(proj, proj, proj, proj, lr, wg, b_gate.reshape(1, -1), norm_g.reshape(1, -1))


def _merge_kernel(pc_ref, ps_ref, pb_ref, pg_ref, gc_ref, gs_ref, gb_ref, gg_ref,
                  wc_ref, wv_ref, wt_ref, wb_ref, wl_ref, o_ref):
    def mm(a_ref, w_ref):
        return jnp.dot(a_ref[...], w_ref[0], preferred_element_type=F32)

    def gate(ref):
        return _sigmoid(ref[...].astype(F32))

    y_s5 = mm(ps_ref, wv_ref) * _sigmoid(mm(ps_ref, wt_ref))
    merged = (gate(gc_ref) * mm(pc_ref, wc_ref) + gate(gs_ref) * y_s5
              + gate(gb_ref) * mm(pb_ref, wb_ref) + gate(gg_ref) * mm(pg_ref, wl_ref))
    o_ref[...] = merged.astype(BF16)


def _merge(pre_conv, pre_s5, pre_sb, pre_gla, proj, w_conv, w_val, w_gate, w_sb, w_gla, layer, tm, tn):
    n_tok = proj.shape[0]
    d = w_conv.shape[2]
    pre = pl.BlockSpec((tm, MIX_W), lambda i, j: (i, 0))
    gcol = lambda off: pl.BlockSpec((tm, tn), lambda i, j: (i, off // tn + j))
    wcol = pl.BlockSpec((1, MIX_W, tn), lambda i, j: (layer, 0, j))
    return pl.pallas_call(
        _merge_kernel,
        out_shape=jax.ShapeDtypeStruct((n_tok, d), BF16),
        grid=(n_tok // tm, d // tn),
        in_specs=[pre, pre, pre, pre,
                  gcol(OFF_G_CONV), gcol(OFF_G_S5), gcol(OFF_G_SB), gcol(OFF_G_GLA),
                  wcol, wcol, wcol, wcol, wcol],
        out_specs=pl.BlockSpec((tm, tn), lambda i, j: (i, j)),
        compiler_params=_cparams(("parallel", "arbitrary")),
        name="branch_merge",
    )(pre_conv, pre_s5, pre_sb, pre_gla, proj, proj, proj, proj, w_conv, w_val, w_gate, w_sb, w_gla)


def _out_proj_kernel(m_ref, x_ref, gt_ref, wo_ref, lg_ref, lb_ref, o_ref, *, alpha):
    for r in range(m_ref.shape[0] // OUT_ROW_CHUNK):
        sl = slice(r * OUT_ROW_CHUNK, (r + 1) * OUT_ROW_CHUNK)
        y = jnp.dot(m_ref[sl, :], wo_ref[0], preferred_element_type=F32)
        z = alpha * x_ref[sl, :] + gt_ref[0] * y
        o_ref[sl, :] = _layer_norm_rows(z) * lg_ref[0] + lb_ref[0]


def _out_proj_ln(merged, x, gt, w_o, ln_g, ln_b, layer, seq, alpha, tm):
    n_tok, d = x.shape
    per_seq = seq // tm
    vec = pl.BlockSpec((1, 1, d), lambda i: (layer, 0, 0))
    return pl.pallas_call(
        functools.partial(_out_proj_kernel, alpha=alpha),
        out_shape=jax.ShapeDtypeStruct((n_tok, d), F32),
        grid=(n_tok // tm,),
        in_specs=[pl.BlockSpec((tm, d), lambda i: (i, 0)),
                  pl.BlockSpec((tm, d), lambda i: (i, 0)),
                  pl.BlockSpec((1, 1, d), lambda i: (i // per_seq, 0, 0)),
                  pl.BlockSpec((1, d, d), lambda i: (layer, 0, 0)),
                  vec, vec],
        out_specs=pl.BlockSpec((tm, d), lambda i: (i, 0)),
        compiler_params=_cparams(("parallel",)),
        name="out_proj_ln",
    )(merged, x, gt, w_o, ln_g.reshape(-1, 1, d), ln_b.reshape(-1, 1, d))


def _ffn_kernel(x_ref, sc_ref, sh_ref, gt_ref, wg_ref, wu_ref, wd_ref, lg_ref, lb_ref, o_ref,
                h_ref, acc_ref, *, alpha):
    j = pl.program_id(1)

    @pl.when(j == 0)
    def _():
        _ln_modulate_into(h_ref, x_ref, sc_ref, sh_ref)
        acc_ref[...] = jnp.zeros_like(acc_ref)

    h = h_ref[...]
    gate = jnp.dot(h, wg_ref[0], preferred_element_type=F32)
    up = jnp.dot(h, wu_ref[0], preferred_element_type=F32)
    act = (gate * _sigmoid(gate) * up).astype(BF16)
    acc_ref[...] += jnp.dot(act, wd_ref[0], preferred_element_type=F32)

    @pl.when(j == pl.num_programs(1) - 1)
    def _():
        _residual_ln_into(o_ref, x_ref, acc_ref, gt_ref, lg_ref, lb_ref, alpha)


def _ffn(x, sc, sh, gt, w_gate, w_up, w_down, ln_g, ln_b, layer, seq, alpha, tm, tf):
    n_tok, d = x.shape
    d_ff = w_gate.shape[2]
    per_seq = seq // tm
    mod = pl.BlockSpec((1, 1, d), lambda i, j: (i // per_seq, 0, 0))
    vec = pl.BlockSpec((1, 1, d), lambda i, j: (layer, 0, 0))
    return pl.pallas_call(
        functools.partial(_ffn_kernel, alpha=alpha),
        out_shape=jax.ShapeDtypeStruct((n_tok, d), F32),
        grid=(n_tok // tm, d_ff // tf),
        in_specs=[pl.BlockSpec((tm, d), lambda i, j: (i, 0)), mod, mod, mod,
                  pl.BlockSpec((1, d, tf), lambda i, j: (layer, 0, j)),
                  pl.BlockSpec((1, d, tf), lambda i, j: (layer, 0, j)),
                  pl.BlockSpec((1, tf, d), lambda i, j: (layer, j, 0)),
                  vec, vec],
        out_specs=pl.BlockSpec((tm, d), lambda i, j: (i, 0)),
        scratch_shapes=[pltpu.VMEM((tm, d), BF16), pltpu.VMEM((tm, d), F32)],
        compiler_params=_cparams(("parallel", "arbitrary")),
        name="ffn_swiglu_ln",
    )(x, sc, sh, gt, w_gate, w_up, w_down, ln_g.reshape(-1, 1, d), ln_b.reshape(-1, 1, d))


def kernel(x, c, ada_w, ada_b, w_in, conv_w, w_conv_out, s5_lam_re, s5_lam_im, s5_b_re, s5_b_im, s5_c_re, s5_c_im, s5_d, s5_log_dt, w_s5_val, w_s5_gate, w_sb_out, gla_w_gate, gla_b_gate, gla_norm_g, w_gla_out, w_o, ln1_g, ln1_b, ffn_w_gate, ffn_w_up, ffn_w_down, ln2_g, ln2_b):
    bsz, seq, d = x.shape
    depth = ada_w.shape[0]
    alpha = (2.0 * depth) ** 0.25
    mod = _ada_mod(c, ada_w, ada_b, _tiles(seq)["ada_tn"])
    xf = x.reshape(bsz * seq, d)
    bf = lambda a: a.astype(BF16)
    t = _tiles(seq)
    w_in_parts = _w_in_prep(jnp.swapaxes(w_in, 1, 2), t["prep_tn"])
    merge_w = (bf(w_conv_out), bf(_channel_major(w_s5_val, 1)), bf(_channel_major(w_s5_gate, 1)),
               bf(w_sb_out), bf(w_gla_out))
    w_o_bf = bf(w_o)
    ffn_w = (bf(ffn_w_gate), bf(ffn_w_up), bf(ffn_w_down))
    s5_params = jax.vmap(_s5_params)(s5_lam_re, s5_lam_im, s5_b_re, s5_b_im, s5_c_re, s5_c_im, s5_d,
                                     s5_log_dt)
    for l in range(depth):
        sh1, sc1, gt1, sh2, sc2, gt2 = [m.reshape(bsz, 1, d) for m in jnp.split(mod[l], 6, axis=-1)]
        proj, lr = _inproj(xf, sc1, sh1, *w_in_parts, l, seq, t["inproj_tm"], t["inproj_tn"])
        pre_conv = _short_conv(proj, conv_w[l], seq, t["conv_tc"])
        pre_s5 = _s5_gelu(proj, [p[l] for p in s5_params], bsz, seq)
        pre_sb = _stick_breaking(proj, bsz, seq, t["sb_tq"])
        pre_gla = _gla(proj, lr, gla_w_gate[l], gla_b_gate[l], gla_norm_g[l], bsz, seq)
        merged = _merge(pre_conv, pre_s5, pre_sb, pre_gla, proj, *merge_w, l, t["merge_tm"], t["merge_tn"])
        xf = _out_proj_ln(merged, xf, gt1, w_o_bf, ln1_g, ln1_b, l, seq, alpha, t["out_tm"])
        xf = _ffn(xf, sc2, sh2, gt2, *ffn_w, ln2_g, ln2_b, l, seq, alpha, t["ffn_tm"], t["ffn_tf"])
    return xf.reshape(bsz, seq, d)
```

```python
import functools
import math

import jax
import jax.numpy as jnp
from jax import lax
from jax.experimental import pallas as pl
from jax.experimental.pallas import tpu as pltpu

F32 = jnp.float32
BF16 = jnp.bfloat16

LN_EPS = 1e-5
LOG2_E = math.log2(math.e)
LANES = 128
SUBLANES = 8
BF16_ROWS = 16
MIX_W = 1024
S5_GROUP = 16
S5_STATE = 64
S5_CHUNK = 16
S5_GROUPS_PER_STEP = 4
SB_HEADS = 8
SB_DH = 128
SB_HEADS_PER_STEP = 4
GLA_HEADS = 4
GLA_DK = 128
GLA_DV = 256
GLA_RANK = 16
GLA_TAU = 16.0
GLA_CHUNK = 128
GLA_SUB = 16
GLA_HEADS_PER_STEP = 4
VMEM_LIMIT = 56 * 1024 * 1024
SB_SKIP = 90.0

OFF_CV_B, OFF_CV_C, OFF_CV_X = 0, 1024, 2048
OFF_SB_Q, OFF_SB_K, OFF_SB_V = 3072, 4096, 5120
OFF_GL_Q, OFF_GL_K, OFF_GL_V, OFF_GL_R = 6144, 6656, 7168, 8192
OFF_S5_U = 9216
MAIN_W = 10240
OFF_G_CONV, OFF_G_S5, OFF_G_SB, OFF_G_GLA = 10240, 12288, 14336, 16384
PROJ_W = 18432
SRC_S5_U = 3072
SRC_GL_LR = 10240
SRC_GATES = 10256


def _tiles(seq):
    return dict(ada_tn=1024, prep_tn=1024, inproj_tm=min(1024, seq), inproj_tn=2048,
                conv_tc=min(1024, seq), sb_tq=min(256, seq),
                merge_tm=min(1024, seq), merge_tn=512, out_tm=min(512, seq),
                ffn_tm=min(512, seq), ffn_tf=512)


def _cparams(sem):
    return pltpu.CompilerParams(dimension_semantics=sem, vmem_limit_bytes=VMEM_LIMIT)


def _layer_norm_rows(x):
    mu = jnp.mean(x, axis=-1, keepdims=True)
    xc = x - mu
    var = jnp.mean(xc * xc, axis=-1, keepdims=True)
    return xc * lax.rsqrt(var + LN_EPS)


OUT_ROW_CHUNK = 256
ROW_CHUNK = 256


def _for_row_chunks(n_rows, fn):
    chunk = min(ROW_CHUNK, n_rows)

    def body(r, _):
        fn(pl.ds(pl.multiple_of(r * chunk, chunk), chunk))
        return 0

    lax.fori_loop(0, n_rows // chunk, body, 0)


def _ln_modulate_into(h_ref, x_ref, sc_ref, sh_ref):
    def rows(sl):
        h = _layer_norm_rows(x_ref[sl, :]) * (1.0 + sc_ref[0]) + sh_ref[0]
        h_ref[sl, :] = h.astype(BF16)

    _for_row_chunks(x_ref.shape[0], rows)


def _residual_ln_into(o_ref, x_ref, y_ref, gt_ref, g_ref, b_ref, alpha):
    def rows(sl):
        z = alpha * x_ref[sl, :] + gt_ref[0] * y_ref[sl, :]
        o_ref[sl, :] = _layer_norm_rows(z) * g_ref[0] + b_ref[0]

    _for_row_chunks(x_ref.shape[0], rows)


def _sigmoid(x):
    return 1.0 / (1.0 + jnp.exp(-x))


def _log_sigmoid(x):
    return jnp.minimum(x, 0.0) - jnp.log(1.0 + jnp.exp(-jnp.abs(x)))


def _ada_kernel(c_ref, w_ref, b_ref, o_ref):
    c = c_ref[...]
    s = (c * _sigmoid(c)).astype(BF16)
    o_ref[0] = jnp.dot(s, w_ref[0].astype(BF16), preferred_element_type=F32) + b_ref[0]


def _ada_mod(c, ada_w, ada_b, tn):
    depth, d, n = ada_w.shape
    bsz = c.shape[0]
    rows = -(-bsz // SUBLANES) * SUBLANES
    c_pad = jnp.zeros((rows, d), F32).at[:bsz].set(c)
    out = pl.pallas_call(
        _ada_kernel,
        out_shape=jax.ShapeDtypeStruct((depth, rows, n), F32),
        grid=(depth, n // tn),
        in_specs=[
            pl.BlockSpec((rows, d), lambda l, j: (0, 0)),
            pl.BlockSpec((1, d, tn), lambda l, j: (l, 0, j)),
            pl.BlockSpec((1, 1, tn), lambda l, j: (l, 0, j)),
        ],
        out_specs=pl.BlockSpec((1, rows, tn), lambda l, j: (l, 0, j)),
        compiler_params=_cparams(("parallel", "parallel")),
        name="ada_mod",
    )(c_pad, ada_w, ada_b.reshape(depth, 1, n))
    return out[:, :bsz]


def _w_in_prep_kernel(a_ref, l_ref, p_ref, o_ref, lr_ref, *, n_direct):
    j = pl.program_id(1)

    @pl.when(j == 0)
    def _():
        row = lax.broadcasted_iota(jnp.int32, l_ref.shape[1:], 0)
        lr_ref[0] = jnp.where(row < GLA_RANK, l_ref[0], 0.0).T.astype(BF16)

    @pl.when(j != n_direct)
    def _():
        o_ref[0] = a_ref[0].T.astype(BF16)

    @pl.when(j == n_direct)
    def _():
        rows = jnp.dot(p_ref[...], a_ref[0].astype(BF16), preferred_element_type=F32)
        o_ref[0] = rows.T.astype(BF16)


def _w_in_prep(w_in_t, tn):
    depth, _, d = w_in_t.shape
    n_a = SRC_S5_U // tn
    n_direct = (MAIN_W - MIX_W) // tn
    n_gate = (PROJ_W - MAIN_W) // tn
    dst = jnp.arange(MIX_W)
    perm = (dst[:, None] == (dst[None, :] % S5_GROUP) * (MIX_W // S5_GROUP) + dst[None, :] // S5_GROUP
            ).astype(BF16)

    def src_row(j):
        direct = jnp.where(j < n_a, j * tn, j * tn + MIX_W)
        row = jnp.where(j < n_direct, direct,
                        jnp.where(j == n_direct, SRC_S5_U, SRC_GATES + (j - n_direct - 1) * tn))
        return pl.multiple_of(row, math.gcd(SRC_GATES, tn))

    return pl.pallas_call(
        functools.partial(_w_in_prep_kernel, n_direct=n_direct),
        out_shape=(jax.ShapeDtypeStruct((depth, d, PROJ_W), BF16),
                   jax.ShapeDtypeStruct((depth, d, LANES), BF16)),
        grid=(depth, n_direct + 1 + n_gate),
        in_specs=[
            pl.BlockSpec((pl.Element(1), pl.Element(tn), pl.Element(d)), lambda l, j: (l, src_row(j), 0)),
            pl.BlockSpec((pl.Element(1), pl.Element(LANES), pl.Element(d)), lambda l, j: (l, SRC_GL_LR, 0)),
            pl.BlockSpec((MIX_W, MIX_W), lambda l, j: (0, 0)),
        ],
        out_specs=(pl.BlockSpec((1, d, tn), lambda l, j: (l, 0, j)),
                   pl.BlockSpec((1, d, LANES), lambda l, j: (l, 0, 0))),
        compiler_params=_cparams(("parallel", "arbitrary")),
        name="w_in_prep",
    )(w_in_t, w_in_t, perm)


def _inproj_kernel(x_ref, sc_ref, sh_ref, w_ref, wl_ref, o_ref, lr_ref, u_ref, h_ref, *, s5_tile, s5_off):
    j = pl.program_id(1)

    @pl.when(j == 0)
    def _():
        _ln_modulate_into(h_ref, x_ref, sc_ref, sh_ref)
        lr_ref[...] = jnp.dot(h_ref[...], wl_ref[0], preferred_element_type=F32).astype(BF16)

    o_ref[...] = jnp.dot(h_ref[...], w_ref[0], preferred_element_type=F32).astype(BF16)

    @pl.when(j == s5_tile)
    def _():
        u_ref[...] = o_ref[:, s5_off:s5_off + MIX_W]


def _inproj(x, sc, sh, w_all, w_lr, layer, seq, tm, tn):
    n_tok, d = x.shape
    width = w_all.shape[2]
    per_seq = seq // tm
    s5_tile, s5_off = divmod(OFF_S5_U, tn)
    assert s5_off + MIX_W <= tn
    mod = pl.BlockSpec((1, 1, d), lambda i, j: (i // per_seq, 0, 0))
    return pl.pallas_call(
        functools.partial(_inproj_kernel, s5_tile=s5_tile, s5_off=s5_off),
        out_shape=(jax.ShapeDtypeStruct((n_tok, width), BF16),
                   jax.ShapeDtypeStruct((n_tok, LANES), BF16),
                   jax.ShapeDtypeStruct((n_tok, MIX_W), BF16)),
        grid=(n_tok // tm, width // tn),
        in_specs=[
            pl.BlockSpec((tm, d), lambda i, j: (i, 0)),
            mod, mod,
            pl.BlockSpec((1, d, tn), lambda i, j: (layer, 0, j)),
            pl.BlockSpec((1, d, LANES), lambda i, j: (layer, 0, 0)),
        ],
        out_specs=(pl.BlockSpec((tm, tn), lambda i, j: (i, j)),
                   pl.BlockSpec((tm, LANES), lambda i, j: (i, 0)),
                   pl.BlockSpec((tm, MIX_W), lambda i, j: (i, 0))),
        scratch_shapes=[pltpu.VMEM((tm, d), BF16)],
        compiler_params=_cparams(("parallel", "arbitrary")),
        name="ln_mod_inproj",
    )(x, sc, sh, w_all, w_lr)


def _conv_kernel(b_ref, c_ref, x_ref, cp_ref, xp_ref, w_ref, o_ref, *, per_seq):
    h = c_ref[...].astype(F32) * x_ref[...].astype(F32)
    hp = cp_ref[...].astype(F32) * xp_ref[...].astype(F32)
    first = (pl.program_id(0) % per_seq) == 0
    hp = jnp.where(first, 0.0, hp)
    row = lax.broadcasted_iota(jnp.int32, h.shape, 0)
    rows_p = hp.shape[0]
    h1 = jnp.where(row == 0, hp[rows_p - 1:rows_p], pltpu.roll(h, 1, 0))
    h2 = pltpu.roll(h, 2, 0)
    h2 = jnp.where(row == 0, hp[rows_p - 2:rows_p - 1], h2)
    h2 = jnp.where(row == 1, hp[rows_p - 1:rows_p], h2)
    w = w_ref[...]
    y = w[0:1] * h2 + w[1:2] * h1 + w[2:3] * h
    o_ref[...] = (b_ref[...].astype(F32) * y).astype(BF16)


def _short_conv(proj, conv_w, seq, tc):
    n_tok = proj.shape[0]
    halo = BF16_ROWS
    per_seq = seq // tc
    ratio = tc // halo

    def prev(i):
        return jnp.maximum(i * ratio - 1, 0)

    return pl.pallas_call(
        functools.partial(_conv_kernel, per_seq=per_seq),
        out_shape=jax.ShapeDtypeStruct((n_tok, MIX_W), BF16),
        grid=(n_tok // tc,),
        in_specs=[
            pl.BlockSpec((tc, MIX_W), lambda i: (i, OFF_CV_B // MIX_W)),
            pl.BlockSpec((tc, MIX_W), lambda i: (i, OFF_CV_C // MIX_W)),
            pl.BlockSpec((tc, MIX_W), lambda i: (i, OFF_CV_X // MIX_W)),
            pl.BlockSpec((halo, MIX_W), lambda i: (prev(i), OFF_CV_C // MIX_W)),
            pl.BlockSpec((halo, MIX_W), lambda i: (prev(i), OFF_CV_X // MIX_W)),
            pl.BlockSpec((SUBLANES, MIX_W), lambda i: (0, 0)),
        ],
        out_specs=pl.BlockSpec((tc, MIX_W), lambda i: (i, 0)),
        compiler_params=_cparams(("parallel",)),
        name="short_conv",
    )(proj, proj, proj, proj, proj, jnp.zeros((SUBLANES, MIX_W), F32).at[:conv_w.shape[0]].set(conv_w))


def _cmul(xr, xi, yr, yi):
    return xr * yr - xi * yi, xr * yi + xi * yr


def _complex_powers(ar, ai, exponent, shape):
    wr, wi = jnp.ones(shape, F32), jnp.zeros(shape, F32)
    for bit in range(4):
        on = ((exponent >> bit) & 1) == 1
        wr, wi = _cmul(wr, wi, jnp.where(on, ar, 1.0), jnp.where(on, ai, 0.0))
        ar, ai = _cmul(ar, ai, ar, ai)
    return wr, wi, ar, ai


def _s5_chunk_matrices(bt_re, bt_im, ct_re, ct_im, ac_re, ac_im, arow_re, arow_im):
    t, hh, p = S5_CHUNK, S5_GROUP, S5_STATE
    hi = lax.Precision.HIGHEST
    lane_tau = lax.broadcasted_iota(jnp.int32, (p, t * hh), 1) // hh
    wr, wi, _, _ = _complex_powers(ac_re, ac_im, lane_tau, (p, t * hh))
    cw_re, cw_im = _cmul(ct_re, ct_im, wr, wi)
    strip = (jnp.dot(bt_re, cw_re, precision=hi, preferred_element_type=F32)
             - jnp.dot(bt_im, cw_im, precision=hi, preferred_element_type=F32))
    lane = lax.broadcasted_iota(jnp.int32, (hh, t * hh), 1)
    m_intra = jnp.concatenate(
        [strip if tp == 0 else jnp.where(lane >= tp * hh, pltpu.roll(strip, tp * hh, 1), 0.0)
         for tp in range(t)], axis=0)
    mo_re, mo_im = _cmul(cw_re, cw_im, ac_re, ac_im)
    both = lambda a: jnp.concatenate([a, a], axis=1)
    low = lax.broadcasted_iota(jnp.int32, (t, 2 * p), 1) < p
    row_tau = (t - 1) - lax.broadcasted_iota(jnp.int32, (t, 2 * p), 0)
    rr, ri, a16_re, a16_im = _complex_powers(both(arow_re), both(arow_im), row_tau, (t, 2 * p))
    r_same, r_cross = jnp.where(low, rr, ri), jnp.where(low, -ri, rr)
    b_re2, b_im2 = both(bt_re), both(bt_im)
    ms = jnp.concatenate([b_re2 * r_same[tp:tp + 1] + b_im2 * r_cross[tp:tp + 1] for tp in range(t)],
                         axis=0)
    mo = jnp.concatenate([mo_re, -mo_im], axis=0)
    return m_intra.astype(BF16), ms.astype(BF16), mo.astype(BF16), a16_re, a16_im


def _s5_kernel(u_ref, btr_ref, bti_ref, ctr_ref, cti_ref, acr_ref, aci_ref, arr_ref, ari_ref, d_ref,
               o_ref, *, n_chunks):
    for gi in range(u_ref.shape[0]):
        _s5_group(gi, u_ref, btr_ref, bti_ref, ctr_ref, cti_ref, acr_ref, aci_ref, arr_ref, ari_ref, d_ref,
                  o_ref, n_chunks)


def _s5_group(gi, u_ref, btr_ref, bti_ref, ctr_ref, cti_ref, acr_ref, aci_ref, arr_ref, ari_ref, d_ref,
              o_ref, n_chunks):
    u = u_ref[gi]
    m_intra, ms, mo, ar, ai = _s5_chunk_matrices(
        btr_ref[gi], bti_ref[gi], ctr_ref[gi], cti_ref[gi], acr_ref[gi], aci_ref[gi], arr_ref[gi],
        ari_ref[gi])
    s = jnp.dot(u, ms, preferred_element_type=F32)
    chunk = lax.broadcasted_iota(jnp.int32, s.shape, 0) % n_chunks
    low = lax.broadcasted_iota(jnp.int32, ar.shape, 1) < S5_STATE

    def shifted(a, rows):
        return jnp.where(chunk >= rows, pltpu.roll(a, rows, 0), 0.0)

    step = 1
    while step < n_chunks:
        prev = shifted(s, step)
        s = s + ar * prev + jnp.where(low, -ai, ai) * pltpu.roll(prev, S5_STATE, 1)
        ar, ai = ar * ar - ai * ai, 2.0 * ar * ai
        step *= 2
    y = jnp.dot(u, m_intra, preferred_element_type=F32)
    y += jnp.dot(shifted(s, 1).astype(BF16), mo, preferred_element_type=F32)
    y += d_ref[gi] * u.astype(F32)
    inner = math.sqrt(2.0 / math.pi) * (y + 0.044715 * (y * y * y))
    o_ref[gi] = (0.5 * y * (1.0 + jnp.tanh(inner))).astype(BF16)


def _s5_params(lam_re, lam_im, b_re, b_im, c_re, c_im, d_skip, log_dt):
    g, p = lam_re.shape
    hh = S5_GROUP
    t = S5_CHUNK
    dt = jnp.exp(log_dt)[:, None]
    mag = jnp.exp(lam_re * dt)
    ang = lam_im * dt
    ab_re, ab_im = mag * jnp.cos(ang), mag * jnp.sin(ang)
    den = lam_re * lam_re + lam_im * lam_im
    nr, ni = ab_re - 1.0, ab_im
    f_re = (nr * lam_re + ni * lam_im) / den
    f_im = (ni * lam_re - nr * lam_im) / den
    bb_re = f_re[..., None] * b_re - f_im[..., None] * b_im
    bb_im = f_re[..., None] * b_im + f_im[..., None] * b_re
    bt_re, bt_im = bb_re.transpose(0, 2, 1), bb_im.transpose(0, 2, 1)
    tile = lambda c: jnp.tile(c.transpose(0, 2, 1), (1, 1, t))
    d_row = jnp.tile(d_skip.reshape(g, 1, hh), (1, 1, t))
    return (bt_re, bt_im, tile(c_re), tile(c_im), ab_re[:, :, None], ab_im[:, :, None],
            ab_re[:, None, :], ab_im[:, None, :], d_row)


def _s5_gelu(u_cols, params, bsz, seq):
    g = params[0].shape[0]
    t, hh, p = S5_CHUNK, S5_GROUP, S5_STATE
    n_chunks = seq // t
    rows = n_chunks * bsz
    u = u_cols.reshape(rows * t * hh, g).T.reshape(g, rows, t * hh)
    grp = lambda shape: pl.BlockSpec((S5_GROUPS_PER_STEP,) + shape, lambda i: (i, 0, 0))
    y = pl.pallas_call(
        functools.partial(_s5_kernel, n_chunks=n_chunks),
        out_shape=jax.ShapeDtypeStruct((g, rows, t * hh), BF16),
        grid=(g // S5_GROUPS_PER_STEP,),
        in_specs=[grp((rows, t * hh)), grp((hh, p)), grp((hh, p)), grp((p, t * hh)), grp((p, t * hh)),
                  grp((p, 1)), grp((p, 1)), grp((1, p)), grp((1, p)), grp((1, t * hh))],
        out_specs=grp((rows, t * hh)),
        compiler_params=_cparams(("parallel",)),
        name="s5_chunked",
    )(u, *params)
    return y.reshape(g, rows * t * hh).T.reshape(bsz * seq, hh * g)


def _channel_major(w, axis):
    g = w.shape[axis] // S5_GROUP
    shape = w.shape[:axis] + (g, S5_GROUP) + w.shape[axis + 1:]
    return jnp.swapaxes(w.reshape(shape), axis, axis + 1).reshape(w.shape)


def _sb_kernel(q_ref, k_ref, v_ref, o_ref, *, tq, n_heads):
    i = pl.program_id(2)
    scale = SB_DH ** -0.5 * LOG2_E
    row = lax.broadcasted_iota(jnp.int32, (tq, tq), 0)
    col = lax.broadcasted_iota(jnp.int32, (tq, tq), 1)
    below = row > col
    tri = jnp.where(below, 1.0, 0.0).astype(BF16)

    def logits(cols, j, diag):
        rows = pl.ds(pl.multiple_of(j * tq, tq), tq)
        z = lax.dot_general(q_ref[:, cols], k_ref[rows, cols], (((1,), (1,)), ((), ())),
                            preferred_element_type=F32) * scale
        lp = jnp.log2(1.0 + jnp.exp2(-jnp.abs(z)))
        log_beta = jnp.minimum(z, 0.0) - lp
        log_1m = log_beta - z
        if diag:
            log_1m = jnp.where(below, log_1m, 0.0)
        local = jnp.dot(log_1m.astype(BF16), tri, preferred_element_type=F32)
        return log_beta, local, jnp.sum(log_1m, axis=-1, keepdims=True), v_ref[rows, cols]

    def weigh(parts, carry, diag):
        log_beta, local, total, vj = parts
        w = jnp.exp2(log_beta + local + carry)
        if diag:
            w = jnp.where(below, w, 0.0)
        return jnp.dot(w.astype(BF16), vj, preferred_element_type=F32), carry + total

    has_prev = i > 0
    heads = [slice(hd * SB_DH, (hd + 1) * SB_DH) for hd in range(n_heads)]
    started = []
    for cols in heads:
        d_parts = logits(cols, i, True)
        p_parts = logits(cols, jnp.maximum(i - 1, 0), False)
        acc, carry = weigh(d_parts, jnp.zeros((tq, 1), F32), True)
        p_acc, p_carry = weigh(p_parts, carry, False)
        started.append((jnp.where(has_prev, p_carry, carry), jnp.where(has_prev, acc + p_acc, acc)))

    def cond(state):
        j, top = state[0], state[1]
        return jnp.logical_and(j >= 0, top > -SB_SKIP * LOG2_E)

    for cols, (carry, acc) in zip(heads, started):
        def body(state, cols=cols):
            j, _, carry, acc = state
            d_acc, carry = weigh(logits(cols, j, False), carry, False)
            return j - 1, jnp.max(carry), carry, acc + d_acc

        state = lax.while_loop(cond, body, (i - 2, jnp.max(carry), carry, acc))
        o_ref[:, cols] = state[3].astype(BF16)


def _stick_breaking(proj, bsz, seq, tq):
    n_tok = proj.shape[0]
    nq = seq // tq
    nh = SB_HEADS_PER_STEP
    width = nh * SB_DH
    return pl.pallas_call(
        functools.partial(_sb_kernel, tq=tq, n_heads=nh),
        out_shape=jax.ShapeDtypeStruct((n_tok, MIX_W), BF16),
        grid=(bsz, SB_HEADS // nh, nq),
        in_specs=[
            pl.BlockSpec((tq, width), lambda b, h, i: (b * nq + i, OFF_SB_Q // width + h)),
            pl.BlockSpec((seq, width), lambda b, h, i: (b, OFF_SB_K // width + h)),
            pl.BlockSpec((seq, width), lambda b, h, i: (b, OFF_SB_V // width + h)),
        ],
        out_specs=pl.BlockSpec((tq, width), lambda b, h, i: (b * nq + i, h)),
        compiler_params=_cparams(("parallel", "parallel", "arbitrary")),
        name="stick_breaking",
    )(proj, proj, proj)


def _split3(x):
    hi = x.astype(BF16)
    r1 = x - hi.astype(F32)
    mid = r1.astype(BF16)
    lo = (r1 - mid.astype(F32)).astype(BF16)
    return hi, mid, lo


def _gla_kernel(q_ref, k_ref, v_ref, r_ref, lr_ref, wg_ref, bg_ref, g_ref, o_ref, st_ref, sc_ref, kd_ref,
                *, n_chunks, n_heads):
    c_len, sub = GLA_CHUNK, GLA_SUB
    st_ref[...] = jnp.zeros_like(st_ref)
    row = lax.broadcasted_iota(jnp.int32, (c_len, c_len), 0)
    col = lax.broadcasted_iota(jnp.int32, (c_len, c_len), 1)
    tri_incl = jnp.where(col <= row, 1.0, 0.0).astype(BF16)
    col_s = lax.broadcasted_iota(jnp.int32, (sub, c_len), 1)
    row_s = lax.broadcasted_iota(jnp.int32, (sub, GLA_DK), 0)
    nt = (((1,), (1,)), ((), ()))

    def head_chunk(hd, rows):
        kcols = slice(hd * GLA_DK, (hd + 1) * GLA_DK)
        vcols = slice(hd * GLA_DV, (hd + 1) * GLA_DV)
        q = q_ref[rows, kcols].astype(F32) * (GLA_DK ** -0.5)
        k = k_ref[rows, kcols].astype(F32)
        v = v_ref[rows, vcols]
        pre = jnp.dot(lr_ref[rows, :], wg_ref[:, kcols], preferred_element_type=F32) + bg_ref[:, kcols]
        la = _log_sigmoid(pre) * (1.0 / GLA_TAU)
        b = sum(jnp.dot(tri_incl, part, preferred_element_type=F32) for part in _split3(la))
        b = b * LOG2_E
        bx = b - la * LOG2_E
        b_end = b[c_len - 1:c_len]
        st = st_ref[hd]
        qe = (q * jnp.exp2(b)).astype(BF16)
        o = lax.dot_general(qe, st.astype(BF16), nt, preferred_element_type=F32)
        kd_ref[hd] = jnp.zeros((c_len, GLA_DK), F32)
        for i in range(c_len // sub):
            lo_row = i * sub
            sl = slice(lo_row, lo_row + sub)
            bi, qi, ki = b[sl], q[sl], k[sl]
            ref_i = bx[lo_row:lo_row + 1]
            if i == 0:
                sc = jnp.zeros((sub, c_len), F32)
            else:
                pl_row = lo_row - sub
                ps = slice(pl_row, lo_row)
                if pl_row:
                    kd_ref[hd, :pl_row, :] = kd_ref[hd, :pl_row, :] * jnp.exp2(ref_i - bx[pl_row:pl_row + 1])
                kd_ref[hd, ps, :] = k[ps] * jnp.exp2(ref_i - b[ps])
                qd = (qi * jnp.exp2(bi - ref_i)).astype(BF16)
                sc = lax.dot_general(qd, kd_ref[hd].astype(BF16), nt, preferred_element_type=F32)
            for s in range(sub):
                dec = jnp.exp2(jnp.where(row_s >= s, bi - bi[s:s + 1], -1e30))
                cs = jnp.sum(qi * (ki[s:s + 1] * dec), axis=-1, keepdims=True)
                sc = jnp.where(col_s == lo_row + s, cs, sc)
            sc_ref[hd, sl, :] = sc
        o = o + jnp.dot(sc_ref[hd].astype(BF16), v, preferred_element_type=F32)
        ke = (k * jnp.exp2(b_end - b)).astype(BF16)
        upd = lax.dot_general(v, ke, (((0,), (0,)), ((), ())), preferred_element_type=F32)
        st_ref[hd] = st * jnp.exp2(b_end) + upd
        on = _layer_norm_rows(o) * g_ref[:, vcols]
        r = r_ref[rows, vcols].astype(F32)
        o_ref[rows, vcols] = (on * (r * _sigmoid(r))).astype(BF16)

    def chunk(c, _):
        rows = pl.ds(pl.multiple_of(c * c_len, c_len), c_len)
        for hd in range(n_heads):
            head_chunk(hd, rows)
        return 0

    lax.fori_loop(0, n_chunks, chunk, 0)


def _gla(proj, lr, w_gate, b_gate, norm_g, bsz, seq):
    n_tok = proj.shape[0]
    n_chunks = seq // GLA_CHUNK
    nh = GLA_HEADS_PER_STEP
    wk, wv = nh * GLA_DK, nh * GLA_DV
    wg = jnp.zeros((LANES, GLA_HEADS * GLA_DK), BF16).at[:GLA_RANK].set(w_gate.astype(BF16))
    return pl.pallas_call(
        functools.partial(_gla_kernel, n_chunks=n_chunks, n_heads=nh),
        out_shape=jax.ShapeDtypeStruct((n_tok, MIX_W), BF16),
        grid=(bsz, GLA_HEADS // nh),
        in_specs=[
            pl.BlockSpec((seq, wk), lambda b, h: (b, OFF_GL_Q // wk + h)),
            pl.BlockSpec((seq, wk), lambda b, h: (b, OFF_GL_K // wk + h)),
            pl.BlockSpec((seq, wv), lambda b, h: (b, OFF_GL_V // wv + h)),
            pl.BlockSpec((seq, wv), lambda b, h: (b, OFF_GL_R // wv + h)),
            pl.BlockSpec((seq, LANES), lambda b, h: (b, 0)),
            pl.BlockSpec((LANES, wk), lambda b, h: (0, h)),
            pl.BlockSpec((1, wk), lambda b, h: (0, h)),
            pl.BlockSpec((1, wv), lambda b, h: (0, h)),
        ],
        out_specs=pl.BlockSpec((seq, wv), lambda b, h: (b, h)),
        scratch_shapes=[pltpu.VMEM((nh, GLA_DV, GLA_DK), F32),
                        pltpu.VMEM((nh, GLA_CHUNK, GLA_CHUNK), F32),
                        pltpu.VMEM((nh, GLA_CHUNK, GLA_DK), F32)],
        compiler_params=_cparams(("parallel", "parallel")),
        name="gla",
    )(proj, proj, proj, proj, lr, wg, b_gate.reshape(1, -1), norm_g.reshape(1, -1))


def _merge_kernel(pc_ref, ps_ref, pb_ref, pg_ref, gc_ref, gs_ref, gb_ref, gg_ref,
                  wc_ref, wv_ref, wt_ref, wb_ref, wl_ref, o_ref):
    def mm(a_ref, w_ref):
        return jnp.dot(a_ref[...], w_ref[0], preferred_element_type=F32)

    def gate(ref):
        return _sigmoid(ref[...].astype(F32))

    y_s5 = mm(ps_ref, wv_ref) * _sigmoid(mm(ps_ref, wt_ref))
    merged = (gate(gc_ref) * mm(pc_ref, wc_ref) + gate(gs_ref) * y_s5
              + gate(gb_ref) * mm(pb_ref, wb_ref) + gate(gg_ref) * mm(pg_ref, wl_ref))
    o_ref[...] = merged.astype(BF16)


def _merge(pre_conv, pre_s5, pre_sb, pre_gla, proj, w_conv, w_val, w_gate, w_sb, w_gla, layer, tm, tn):
    n_tok = proj.shape[0]
    d = w_conv.shape[2]
    pre = pl.BlockSpec((tm, MIX_W), lambda i, j: (i, 0))
    gcol = lambda off: pl.BlockSpec((tm, tn), lambda i, j: (i, off // tn + j))
    wcol = pl.BlockSpec((1, MIX_W, tn), lambda i, j: (layer, 0, j))
    return pl.pallas_call(
        _merge_kernel,
        out_shape=jax.ShapeDtypeStruct((n_tok, d), BF16),
        grid=(n_tok // tm, d // tn),
        in_specs=[pre, pre, pre, pre,
                  gcol(OFF_G_CONV), gcol(OFF_G_S5), gcol(OFF_G_SB), gcol(OFF_G_GLA),
                  wcol, wcol, wcol, wcol, wcol],
        out_specs=pl.BlockSpec((tm, tn), lambda i, j: (i, j)),
        compiler_params=_cparams(("parallel", "arbitrary")),
        name="branch_merge",
    )(pre_conv, pre_s5, pre_sb, pre_gla, proj, proj, proj, proj, w_conv, w_val, w_gate, w_sb, w_gla)


def _out_proj_kernel(m_ref, x_ref, gt_ref, wo_ref, lg_ref, lb_ref, o_ref, *, alpha):
    for r in range(m_ref.shape[0] // OUT_ROW_CHUNK):
        sl = slice(r * OUT_ROW_CHUNK, (r + 1) * OUT_ROW_CHUNK)
        y = jnp.dot(m_ref[sl, :], wo_ref[0], preferred_element_type=F32)
        z = alpha * x_ref[sl, :] + gt_ref[0] * y
        o_ref[sl, :] = _layer_norm_rows(z) * lg_ref[0] + lb_ref[0]


def _out_proj_ln(merged, x, gt, w_o, ln_g, ln_b, layer, seq, alpha, tm):
    n_tok, d = x.shape
    per_seq = seq // tm
    vec = pl.BlockSpec((1, 1, d), lambda i: (layer, 0, 0))
    return pl.pallas_call(
        functools.partial(_out_proj_kernel, alpha=alpha),
        out_shape=jax.ShapeDtypeStruct((n_tok, d), F32),
        grid=(n_tok // tm,),
        in_specs=[pl.BlockSpec((tm, d), lambda i: (i, 0)),
                  pl.BlockSpec((tm, d), lambda i: (i, 0)),
                  pl.BlockSpec((1, 1, d), lambda i: (i // per_seq, 0, 0)),
                  pl.BlockSpec((1, d, d), lambda i: (layer, 0, 0)),
                  vec, vec],
        out_specs=pl.BlockSpec((tm, d), lambda i: (i, 0)),
        compiler_params=_cparams(("parallel",)),
        name="out_proj_ln",
    )(merged, x, gt, w_o, ln_g.reshape(-1, 1, d), ln_b.reshape(-1, 1, d))


def _ffn_kernel(x_ref, sc_ref, sh_ref, gt_ref, wg_ref, wu_ref, wd_ref, lg_ref, lb_ref, o_ref,
                h_ref, acc_ref, *, alpha):
    j = pl.program_id(1)

    @pl.when(j == 0)
    def _():
        _ln_modulate_into(h_ref, x_ref, sc_ref, sh_ref)
        acc_ref[...] = jnp.zeros_like(acc_ref)

    h = h_ref[...]
    gate = jnp.dot(h, wg_ref[0], preferred_element_type=F32)
    up = jnp.dot(h, wu_ref[0], preferred_element_type=F32)
    act = (gate * _sigmoid(gate) * up).astype(BF16)
    acc_ref[...] += jnp.dot(act, wd_ref[0], preferred_element_type=F32)

    @pl.when(j == pl.num_programs(1) - 1)
    def _():
        _residual_ln_into(o_ref, x_ref, acc_ref, gt_ref, lg_ref, lb_ref, alpha)


def _ffn(x, sc, sh, gt, w_gate, w_up, w_down, ln_g, ln_b, layer, seq, alpha, tm, tf):
    n_tok, d = x.shape
    d_ff = w_gate.shape[2]
    per_seq = seq // tm
    mod = pl.BlockSpec((1, 1, d), lambda i, j: (i // per_seq, 0, 0))
    vec = pl.BlockSpec((1, 1, d), lambda i, j: (layer, 0, 0))
    return pl.pallas_call(
        functools.partial(_ffn_kernel, alpha=alpha),
        out_shape=jax.ShapeDtypeStruct((n_tok, d), F32),
        grid=(n_tok // tm, d_ff // tf),
        in_specs=[pl.BlockSpec((tm, d), lambda i, j: (i, 0)), mod, mod, mod,
                  pl.BlockSpec((1, d, tf), lambda i, j: (layer, 0, j)),
                  pl.BlockSpec((1, d, tf), lambda i, j: (layer, 0, j)),
                  pl.BlockSpec((1, tf, d), lambda i, j: (layer, j, 0)),
                  vec, vec],
        out_specs=pl.BlockSpec((tm, d), lambda i, j: (i, 0)),
        scratch_shapes=[pltpu.VMEM((tm, d), BF16), pltpu.VMEM((tm, d), F32)],
        compiler_params=_cparams(("parallel", "arbitrary")),
        name="ffn_swiglu_ln",
    )(x, sc, sh, gt, w_gate, w_up, w_down, ln_g.reshape(-1, 1, d), ln_b.reshape(-1, 1, d))


def kernel(x, c, ada_w, ada_b, w_in, conv_w, w_conv_out, s5_lam_re, s5_lam_im, s5_b_re, s5_b_im, s5_c_re, s5_c_im, s5_d, s5_log_dt, w_s5_val, w_s5_gate, w_sb_out, gla_w_gate, gla_b_gate, gla_norm_g, w_gla_out, w_o, ln1_g, ln1_b, ffn_w_gate, ffn_w_up, ffn_w_down, ln2_g, ln2_b):
    bsz, seq, d = x.shape
    depth = ada_w.shape[0]
    alpha = (2.0 * depth) ** 0.25
    mod = _ada_mod(c, ada_w, ada_b, _tiles(seq)["ada_tn"])
    xf = x.reshape(bsz * seq, d)
    bf = lambda a: a.astype(BF16)
    t = _tiles(seq)
    w_in_parts = _w_in_prep(jnp.swapaxes(w_in, 1, 2), t["prep_tn"])
    merge_w = (bf(w_conv_out), bf(_channel_major(w_s5_val, 1)), bf(_channel_major(w_s5_gate, 1)),
               bf(w_sb_out), bf(w_gla_out))
    w_o_bf = bf(w_o)
    ffn_w = (bf(ffn_w_gate), bf(ffn_w_up), bf(ffn_w_down))
    s5_params = jax.vmap(_s5_params)(s5_lam_re, s5_lam_im, s5_b_re, s5_b_im, s5_c_re, s5_c_im, s5_d,
                                     s5_log_dt)
    for l in range(depth):
        sh1, sc1, gt1, sh2, sc2, gt2 = [m.reshape(bsz, 1, d) for m in jnp.split(mod[l], 6, axis=-1)]
        proj, lr, s5_u = _inproj(xf, sc1, sh1, *w_in_parts, l, seq, t["inproj_tm"], t["inproj_tn"])
        pre_conv = _short_conv(proj, conv_w[l], seq, t["conv_tc"])
        pre_s5 = _s5_gelu(s5_u, [p[l] for p in s5_params], bsz, seq)
        pre_sb = _stick_breaking(proj, bsz, seq, t["sb_tq"])
        pre_gla = _gla(proj, lr, gla_w_gate[l], gla_b_gate[l], gla_norm_g[l], bsz, seq)
        merged = _merge(pre_conv, pre_s5, pre_sb, pre_gla, proj, *merge_w, l, t["merge_tm"], t["merge_tn"])
        xf = _out_proj_ln(merged, xf, gt1, w_o_bf, ln1_g, ln1_b, l, seq, alpha, t["out_tm"])
        xf = _ffn(xf, sc2, sh2, gt2, *ffn_w, ln2_g, ln2_b, l, seq, alpha, t["ffn_tm"], t["ffn_tf"])
    return xf.reshape(bsz, seq, d)
```

```python
import functools
import math

import jax
import jax.numpy as jnp
from jax import lax
from jax.experimental import pallas as pl
from jax.experimental.pallas import tpu as pltpu

F32 = jnp.float32
BF16 = jnp.bfloat16

LN_EPS = 1e-5
LOG2_E = math.log2(math.e)
LANES = 128
SUBLANES = 8
BF16_ROWS = 16
MIX_W = 1024
S5_GROUP = 16
S5_STATE = 64
S5_CHUNK = 16
S5_GROUPS_PER_STEP = 4
SB_HEADS = 8
SB_DH = 128
SB_HEADS_PER_STEP = 8
GLA_HEADS = 4
GLA_DK = 128
GLA_DV = 256
GLA_RANK = 16
GLA_TAU = 16.0
GLA_CHUNK = 128
GLA_SUB = 16
GLA_HEADS_PER_STEP = 4
VMEM_LIMIT = 56 * 1024 * 1024
SB_SKIP = 90.0

OFF_CV_B, OFF_CV_C, OFF_CV_X = 0, 1024, 2048
OFF_SB_Q, OFF_SB_K, OFF_SB_V = 3072, 4096, 5120
OFF_GL_Q, OFF_GL_K, OFF_GL_V, OFF_GL_R = 6144, 6656, 7168, 8192
OFF_S5_U = 9216
MAIN_W = 10240
OFF_G_CONV, OFF_G_S5, OFF_G_SB, OFF_G_GLA = 10240, 12288, 14336, 16384
PROJ_W = 18432
SRC_S5_U = 3072
SRC_GL_LR = 10240
SRC_GATES = 10256


def _tiles(seq):
    return dict(ada_tn=1024, prep_tn=1024, inproj_tm=min(1024, seq), inproj_tn=2048,
                conv_tc=min(1024, seq), sb_tq=min(256, seq),
                merge_tm=min(1024, seq), merge_tn=512, out_tm=min(512, seq),
                ffn_tm=min(512, seq), ffn_tf=512)


def _cparams(sem):
    return pltpu.CompilerParams(dimension_semantics=sem, vmem_limit_bytes=VMEM_LIMIT)


def _layer_norm_rows(x):
    mu = jnp.mean(x, axis=-1, keepdims=True)
    xc = x - mu
    var = jnp.mean(xc * xc, axis=-1, keepdims=True)
    return xc * lax.rsqrt(var + LN_EPS)


OUT_ROW_CHUNK = 256
ROW_CHUNK = 256


def _for_row_chunks(n_rows, fn):
    chunk = min(ROW_CHUNK, n_rows)

    def body(r, _):
        fn(pl.ds(pl.multiple_of(r * chunk, chunk), chunk))
        return 0

    lax.fori_loop(0, n_rows // chunk, body, 0)


def _ln_modulate_into(h_ref, x_ref, sc_ref, sh_ref):
    def rows(sl):
        h = _layer_norm_rows(x_ref[sl, :]) * (1.0 + sc_ref[0]) + sh_ref[0]
        h_ref[sl, :] = h.astype(BF16)

    _for_row_chunks(x_ref.shape[0], rows)


def _residual_ln_into(o_ref, x_ref, y_ref, gt_ref, g_ref, b_ref, alpha):
    def rows(sl):
        z = alpha * x_ref[sl, :] + gt_ref[0] * y_ref[sl, :]
        o_ref[sl, :] = _layer_norm_rows(z) * g_ref[0] + b_ref[0]

    _for_row_chunks(x_ref.shape[0], rows)


def _sigmoid(x):
    return 1.0 / (1.0 + jnp.exp(-x))


def _log_sigmoid(x):
    return jnp.minimum(x, 0.0) - jnp.log(1.0 + jnp.exp(-jnp.abs(x)))


def _ada_kernel(c_ref, w_ref, b_ref, o_ref):
    c = c_ref[...]
    s = (c * _sigmoid(c)).astype(BF16)
    o_ref[0] = jnp.dot(s, w_ref[0].astype(BF16), preferred_element_type=F32) + b_ref[0]


def _ada_mod(c, ada_w, ada_b, tn):
    depth, d, n = ada_w.shape
    bsz = c.shape[0]
    rows = -(-bsz // SUBLANES) * SUBLANES
    c_pad = jnp.zeros((rows, d), F32).at[:bsz].set(c)
    out = pl.pallas_call(
        _ada_kernel,
        out_shape=jax.ShapeDtypeStruct((depth, rows, n), F32),
        grid=(depth, n // tn),
        in_specs=[
            pl.BlockSpec((rows, d), lambda l, j: (0, 0)),
            pl.BlockSpec((1, d, tn), lambda l, j: (l, 0, j)),
            pl.BlockSpec((1, 1, tn), lambda l, j: (l, 0, j)),
        ],
        out_specs=pl.BlockSpec((1, rows, tn), lambda l, j: (l, 0, j)),
        compiler_params=_cparams(("parallel", "parallel")),
        name="ada_mod",
    )(c_pad, ada_w, ada_b.reshape(depth, 1, n))
    return out[:, :bsz]


def _w_in_prep_kernel(a_ref, l_ref, p_ref, o_ref, lr_ref, *, n_direct):
    j = pl.program_id(1)

    @pl.when(j == 0)
    def _():
        row = lax.broadcasted_iota(jnp.int32, l_ref.shape[1:], 0)
        lr_ref[0] = jnp.where(row < GLA_RANK, l_ref[0], 0.0).T.astype(BF16)

    @pl.when(j != n_direct)
    def _():
        o_ref[0] = a_ref[0].T.astype(BF16)

    @pl.when(j == n_direct)
    def _():
        rows = jnp.dot(p_ref[...], a_ref[0].astype(BF16), preferred_element_type=F32)
        o_ref[0] = rows.T.astype(BF16)


def _w_in_prep(w_in_t, tn):
    depth, _, d = w_in_t.shape
    n_a = SRC_S5_U // tn
    n_direct = (MAIN_W - MIX_W) // tn
    n_gate = (PROJ_W - MAIN_W) // tn
    dst = jnp.arange(MIX_W)
    perm = (dst[:, None] == (dst[None, :] % S5_GROUP) * (MIX_W // S5_GROUP) + dst[None, :] // S5_GROUP
            ).astype(BF16)

    def src_row(j):
        direct = jnp.where(j < n_a, j * tn, j * tn + MIX_W)
        row = jnp.where(j < n_direct, direct,
                        jnp.where(j == n_direct, SRC_S5_U, SRC_GATES + (j - n_direct - 1) * tn))
        return pl.multiple_of(row, math.gcd(SRC_GATES, tn))

    return pl.pallas_call(
        functools.partial(_w_in_prep_kernel, n_direct=n_direct),
        out_shape=(jax.ShapeDtypeStruct((depth, d, PROJ_W), BF16),
                   jax.ShapeDtypeStruct((depth, d, LANES), BF16)),
        grid=(depth, n_direct + 1 + n_gate),
        in_specs=[
            pl.BlockSpec((pl.Element(1), pl.Element(tn), pl.Element(d)), lambda l, j: (l, src_row(j), 0)),
            pl.BlockSpec((pl.Element(1), pl.Element(LANES), pl.Element(d)), lambda l, j: (l, SRC_GL_LR, 0)),
            pl.BlockSpec((MIX_W, MIX_W), lambda l, j: (0, 0)),
        ],
        out_specs=(pl.BlockSpec((1, d, tn), lambda l, j: (l, 0, j)),
                   pl.BlockSpec((1, d, LANES), lambda l, j: (l, 0, 0))),
        compiler_params=_cparams(("parallel", "arbitrary")),
        name="w_in_prep",
    )(w_in_t, w_in_t, perm)


def _inproj_kernel(x_ref, sc_ref, sh_ref, w_ref, wl_ref, o_ref, lr_ref, u_ref, h_ref, *, s5_tile, s5_off):
    j = pl.program_id(1)

    @pl.when(j == 0)
    def _():
        _ln_modulate_into(h_ref, x_ref, sc_ref, sh_ref)
        lr_ref[...] = jnp.dot(h_ref[...], wl_ref[0], preferred_element_type=F32).astype(BF16)

    o_ref[...] = jnp.dot(h_ref[...], w_ref[0], preferred_element_type=F32).astype(BF16)

    @pl.when(j == s5_tile)
    def _():
        u_ref[...] = o_ref[:, s5_off:s5_off + MIX_W]


def _inproj(x, sc, sh, w_all, w_lr, layer, seq, tm, tn):
    n_tok, d = x.shape
    width = w_all.shape[2]
    per_seq = seq // tm
    s5_tile, s5_off = divmod(OFF_S5_U, tn)
    assert s5_off + MIX_W <= tn
    mod = pl.BlockSpec((1, 1, d), lambda i, j: (i // per_seq, 0, 0))
    return pl.pallas_call(
        functools.partial(_inproj_kernel, s5_tile=s5_tile, s5_off=s5_off),
        out_shape=(jax.ShapeDtypeStruct((n_tok, width), BF16),
                   jax.ShapeDtypeStruct((n_tok, LANES), BF16),
                   jax.ShapeDtypeStruct((n_tok, MIX_W), BF16)),
        grid=(n_tok // tm, width // tn),
        in_specs=[
            pl.BlockSpec((tm, d), lambda i, j: (i, 0)),
            mod, mod,
            pl.BlockSpec((1, d, tn), lambda i, j: (layer, 0, j)),
            pl.BlockSpec((1, d, LANES), lambda i, j: (layer, 0, 0)),
        ],
        out_specs=(pl.BlockSpec((tm, tn), lambda i, j: (i, j)),
                   pl.BlockSpec((tm, LANES), lambda i, j: (i, 0)),
                   pl.BlockSpec((tm, MIX_W), lambda i, j: (i, 0))),
        scratch_shapes=[pltpu.VMEM((tm, d), BF16)],
        compiler_params=_cparams(("parallel", "arbitrary")),
        name="ln_mod_inproj",
    )(x, sc, sh, w_all, w_lr)


def _conv_kernel(b_ref, c_ref, x_ref, cp_ref, xp_ref, w_ref, o_ref, *, per_seq):
    h = c_ref[...].astype(F32) * x_ref[...].astype(F32)
    hp = cp_ref[...].astype(F32) * xp_ref[...].astype(F32)
    first = (pl.program_id(0) % per_seq) == 0
    hp = jnp.where(first, 0.0, hp)
    row = lax.broadcasted_iota(jnp.int32, h.shape, 0)
    rows_p = hp.shape[0]
    h1 = jnp.where(row == 0, hp[rows_p - 1:rows_p], pltpu.roll(h, 1, 0))
    h2 = pltpu.roll(h, 2, 0)
    h2 = jnp.where(row == 0, hp[rows_p - 2:rows_p - 1], h2)
    h2 = jnp.where(row == 1, hp[rows_p - 1:rows_p], h2)
    w = w_ref[...]
    y = w[0:1] * h2 + w[1:2] * h1 + w[2:3] * h
    o_ref[...] = (b_ref[...].astype(F32) * y).astype(BF16)


def _short_conv(proj, conv_w, seq, tc):
    n_tok = proj.shape[0]
    halo = BF16_ROWS
    per_seq = seq // tc
    ratio = tc // halo

    def prev(i):
        return jnp.maximum(i * ratio - 1, 0)

    return pl.pallas_call(
        functools.partial(_conv_kernel, per_seq=per_seq),
        out_shape=jax.ShapeDtypeStruct((n_tok, MIX_W), BF16),
        grid=(n_tok // tc,),
        in_specs=[
            pl.BlockSpec((tc, MIX_W), lambda i: (i, OFF_CV_B // MIX_W)),
            pl.BlockSpec((tc, MIX_W), lambda i: (i, OFF_CV_C // MIX_W)),
            pl.BlockSpec((tc, MIX_W), lambda i: (i, OFF_CV_X // MIX_W)),
            pl.BlockSpec((halo, MIX_W), lambda i: (prev(i), OFF_CV_C // MIX_W)),
            pl.BlockSpec((halo, MIX_W), lambda i: (prev(i), OFF_CV_X // MIX_W)),
            pl.BlockSpec((SUBLANES, MIX_W), lambda i: (0, 0)),
        ],
        out_specs=pl.BlockSpec((tc, MIX_W), lambda i: (i, 0)),
        compiler_params=_cparams(("parallel",)),
        name="short_conv",
    )(proj, proj, proj, proj, proj, jnp.zeros((SUBLANES, MIX_W), F32).at[:conv_w.shape[0]].set(conv_w))


def _cmul(xr, xi, yr, yi):
    return xr * yr - xi * yi, xr * yi + xi * yr


def _complex_powers(ar, ai, exponent, shape):
    wr, wi = jnp.ones(shape, F32), jnp.zeros(shape, F32)
    for bit in range(4):
        on = ((exponent >> bit) & 1) == 1
        wr, wi = _cmul(wr, wi, jnp.where(on, ar, 1.0), jnp.where(on, ai, 0.0))
        ar, ai = _cmul(ar, ai, ar, ai)
    return wr, wi, ar, ai


def _s5_chunk_matrices(bt_re, bt_im, ct_re, ct_im, ac_re, ac_im, arow_re, arow_im):
    t, hh, p = S5_CHUNK, S5_GROUP, S5_STATE
    hi = lax.Precision.HIGHEST
    lane_tau = lax.broadcasted_iota(jnp.int32, (p, t * hh), 1) // hh
    wr, wi, _, _ = _complex_powers(ac_re, ac_im, lane_tau, (p, t * hh))
    cw_re, cw_im = _cmul(ct_re, ct_im, wr, wi)
    strip = (jnp.dot(bt_re, cw_re, precision=hi, preferred_element_type=F32)
             - jnp.dot(bt_im, cw_im, precision=hi, preferred_element_type=F32))
    lane = lax.broadcasted_iota(jnp.int32, (hh, t * hh), 1)
    m_intra = jnp.concatenate(
        [strip if tp == 0 else jnp.where(lane >= tp * hh, pltpu.roll(strip, tp * hh, 1), 0.0)
         for tp in range(t)], axis=0)
    mo_re, mo_im = _cmul(cw_re, cw_im, ac_re, ac_im)
    both = lambda a: jnp.concatenate([a, a], axis=1)
    low = lax.broadcasted_iota(jnp.int32, (t, 2 * p), 1) < p
    row_tau = (t - 1) - lax.broadcasted_iota(jnp.int32, (t, 2 * p), 0)
    rr, ri, a16_re, a16_im = _complex_powers(both(arow_re), both(arow_im), row_tau, (t, 2 * p))
    r_same, r_cross = jnp.where(low, rr, ri), jnp.where(low, -ri, rr)
    b_re2, b_im2 = both(bt_re), both(bt_im)
    ms = jnp.concatenate([b_re2 * r_same[tp:tp + 1] + b_im2 * r_cross[tp:tp + 1] for tp in range(t)],
                         axis=0)
    mo = jnp.concatenate([mo_re, -mo_im], axis=0)
    return m_intra.astype(BF16), ms.astype(BF16), mo.astype(BF16), a16_re, a16_im


def _s5_kernel(u_ref, btr_ref, bti_ref, ctr_ref, cti_ref, acr_ref, aci_ref, arr_ref, ari_ref, d_ref,
               o_ref, *, n_chunks):
    for gi in range(u_ref.shape[0]):
        _s5_group(gi, u_ref, btr_ref, bti_ref, ctr_ref, cti_ref, acr_ref, aci_ref, arr_ref, ari_ref, d_ref,
                  o_ref, n_chunks)


def _s5_group(gi, u_ref, btr_ref, bti_ref, ctr_ref, cti_ref, acr_ref, aci_ref, arr_ref, ari_ref, d_ref,
              o_ref, n_chunks):
    u = u_ref[gi]
    m_intra, ms, mo, ar, ai = _s5_chunk_matrices(
        btr_ref[gi], bti_ref[gi], ctr_ref[gi], cti_ref[gi], acr_ref[gi], aci_ref[gi], arr_ref[gi],
        ari_ref[gi])
    s = jnp.dot(u, ms, preferred_element_type=F32)
    chunk = lax.broadcasted_iota(jnp.int32, s.shape, 0) % n_chunks
    low = lax.broadcasted_iota(jnp.int32, ar.shape, 1) < S5_STATE

    def shifted(a, rows):
        return jnp.where(chunk >= rows, pltpu.roll(a, rows, 0), 0.0)

    step = 1
    while step < n_chunks:
        prev = shifted(s, step)
        s = s + ar * prev + jnp.where(low, -ai, ai) * pltpu.roll(prev, S5_STATE, 1)
        ar, ai = ar * ar - ai * ai, 2.0 * ar * ai
        step *= 2
    y = jnp.dot(u, m_intra, preferred_element_type=F32)
    y += jnp.dot(shifted(s, 1).astype(BF16), mo, preferred_element_type=F32)
    y += d_ref[gi] * u.astype(F32)
    inner = math.sqrt(2.0 / math.pi) * (y + 0.044715 * (y * y * y))
    o_ref[gi] = (0.5 * y * (1.0 + jnp.tanh(inner))).astype(BF16)


def _s5_params(lam_re, lam_im, b_re, b_im, c_re, c_im, d_skip, log_dt):
    g, p = lam_re.shape
    hh = S5_GROUP
    t = S5_CHUNK
    dt = jnp.exp(log_dt)[:, None]
    mag = jnp.exp(lam_re * dt)
    ang = lam_im * dt
    ab_re, ab_im = mag * jnp.cos(ang), mag * jnp.sin(ang)
    den = lam_re * lam_re + lam_im * lam_im
    nr, ni = ab_re - 1.0, ab_im
    f_re = (nr * lam_re + ni * lam_im) / den
    f_im = (ni * lam_re - nr * lam_im) / den
    bb_re = f_re[..., None] * b_re - f_im[..., None] * b_im
    bb_im = f_re[..., None] * b_im + f_im[..., None] * b_re
    bt_re, bt_im = bb_re.transpose(0, 2, 1), bb_im.transpose(0, 2, 1)
    tile = lambda c: jnp.tile(c.transpose(0, 2, 1), (1, 1, t))
    d_row = jnp.tile(d_skip.reshape(g, 1, hh), (1, 1, t))
    return (bt_re, bt_im, tile(c_re), tile(c_im), ab_re[:, :, None], ab_im[:, :, None],
            ab_re[:, None, :], ab_im[:, None, :], d_row)


def _s5_gelu(u_cols, params, bsz, seq):
    g = params[0].shape[0]
    t, hh, p = S5_CHUNK, S5_GROUP, S5_STATE
    n_chunks = seq // t
    rows = n_chunks * bsz
    u = u_cols.reshape(rows * t * hh, g).T.reshape(g, rows, t * hh)
    grp = lambda shape: pl.BlockSpec((S5_GROUPS_PER_STEP,) + shape, lambda i: (i, 0, 0))
    y = pl.pallas_call(
        functools.partial(_s5_kernel, n_chunks=n_chunks),
        out_shape=jax.ShapeDtypeStruct((g, rows, t * hh), BF16),
        grid=(g // S5_GROUPS_PER_STEP,),
        in_specs=[grp((rows, t * hh)), grp((hh, p)), grp((hh, p)), grp((p, t * hh)), grp((p, t * hh)),
                  grp((p, 1)), grp((p, 1)), grp((1, p)), grp((1, p)), grp((1, t * hh))],
        out_specs=grp((rows, t * hh)),
        compiler_params=_cparams(("parallel",)),
        name="s5_chunked",
    )(u, *params)
    return y.reshape(g, rows * t * hh).T.reshape(bsz * seq, hh * g)


def _channel_major(w, axis):
    g = w.shape[axis] // S5_GROUP
    shape = w.shape[:axis] + (g, S5_GROUP) + w.shape[axis + 1:]
    return jnp.swapaxes(w.reshape(shape), axis, axis + 1).reshape(w.shape)


def _sb_kernel(q_ref, k_ref, v_ref, o_ref, *, tq, n_heads):
    i = pl.program_id(2)
    scale = SB_DH ** -0.5 * LOG2_E
    row = lax.broadcasted_iota(jnp.int32, (tq, tq), 0)
    col = lax.broadcasted_iota(jnp.int32, (tq, tq), 1)
    below = row > col
    tri = jnp.where(below, 1.0, 0.0).astype(BF16)

    def logits(cols, j, diag):
        rows = pl.ds(pl.multiple_of(j * tq, tq), tq)
        z = lax.dot_general(q_ref[:, cols], k_ref[rows, cols], (((1,), (1,)), ((), ())),
                            preferred_element_type=F32) * scale
        lp = jnp.log2(1.0 + jnp.exp2(-jnp.abs(z)))
        log_beta = jnp.minimum(z, 0.0) - lp
        log_1m = log_beta - z
        if diag:
            log_1m = jnp.where(below, log_1m, 0.0)
        local = jnp.dot(log_1m.astype(BF16), tri, preferred_element_type=F32)
        return log_beta, local, jnp.sum(log_1m, axis=-1, keepdims=True), v_ref[rows, cols]

    def weigh(parts, carry, diag):
        log_beta, local, total, vj = parts
        w = jnp.exp2(log_beta + local + carry)
        if diag:
            w = jnp.where(below, w, 0.0)
        return jnp.dot(w.astype(BF16), vj, preferred_element_type=F32), carry + total

    has_prev = i > 0
    heads = [slice(hd * SB_DH, (hd + 1) * SB_DH) for hd in range(n_heads)]
    started = []
    for cols in heads:
        d_parts = logits(cols, i, True)
        p_parts = logits(cols, jnp.maximum(i - 1, 0), False)
        acc, carry = weigh(d_parts, jnp.zeros((tq, 1), F32), True)
        p_acc, p_carry = weigh(p_parts, carry, False)
        started.append((jnp.where(has_prev, p_carry, carry), jnp.where(has_prev, acc + p_acc, acc)))

    def cond(state):
        j, top = state[0], state[1]
        return jnp.logical_and(j >= 0, top > -SB_SKIP * LOG2_E)

    for cols, (carry, acc) in zip(heads, started):
        def body(state, cols=cols):
            j, _, carry, acc = state
            d_acc, carry = weigh(logits(cols, j, False), carry, False)
            return j - 1, jnp.max(carry), carry, acc + d_acc

        state = lax.while_loop(cond, body, (i - 2, jnp.max(carry), carry, acc))
        o_ref[:, cols] = state[3].astype(BF16)


def _stick_breaking(proj, bsz, seq, tq):
    n_tok = proj.shape[0]
    nq = seq // tq
    nh = SB_HEADS_PER_STEP
    width = nh * SB_DH
    return pl.pallas_call(
        functools.partial(_sb_kernel, tq=tq, n_heads=nh),
        out_shape=jax.ShapeDtypeStruct((n_tok, MIX_W), BF16),
        grid=(bsz, SB_HEADS // nh, nq),
        in_specs=[
            pl.BlockSpec((tq, width), lambda b, h, i: (b * nq + i, OFF_SB_Q // width + h)),
            pl.BlockSpec((seq, width), lambda b, h, i: (b, OFF_SB_K // width + h)),
            pl.BlockSpec((seq, width), lambda b, h, i: (b, OFF_SB_V // width + h)),
        ],
        out_specs=pl.BlockSpec((tq, width), lambda b, h, i: (b * nq + i, h)),
        compiler_params=_cparams(("parallel", "parallel", "arbitrary")),
        name="stick_breaking",
    )(proj, proj, proj)


def _split3(x):
    hi = x.astype(BF16)
    r1 = x - hi.astype(F32)
    mid = r1.astype(BF16)
    lo = (r1 - mid.astype(F32)).astype(BF16)
    return hi, mid, lo


def _gla_kernel(q_ref, k_ref, v_ref, r_ref, lr_ref, wg_ref, bg_ref, g_ref, o_ref, st_ref, sc_ref, kd_ref,
                *, n_chunks, n_heads):
    c_len, sub = GLA_CHUNK, GLA_SUB
    st_ref[...] = jnp.zeros_like(st_ref)
    row = lax.broadcasted_iota(jnp.int32, (c_len, c_len), 0)
    col = lax.broadcasted_iota(jnp.int32, (c_len, c_len), 1)
    tri_incl = jnp.where(col <= row, 1.0, 0.0).astype(BF16)
    col_s = lax.broadcasted_iota(jnp.int32, (sub, c_len), 1)
    row_s = lax.broadcasted_iota(jnp.int32, (sub, GLA_DK), 0)
    nt = (((1,), (1,)), ((), ()))

    def head_chunk(hd, rows):
        kcols = slice(hd * GLA_DK, (hd + 1) * GLA_DK)
        vcols = slice(hd * GLA_DV, (hd + 1) * GLA_DV)
        q = q_ref[rows, kcols].astype(F32) * (GLA_DK ** -0.5)
        k = k_ref[rows, kcols].astype(F32)
        v = v_ref[rows, vcols]
        pre = jnp.dot(lr_ref[rows, :], wg_ref[:, kcols], preferred_element_type=F32) + bg_ref[:, kcols]
        la = _log_sigmoid(pre) * (1.0 / GLA_TAU)
        b = sum(jnp.dot(tri_incl, part, preferred_element_type=F32) for part in _split3(la))
        b = b * LOG2_E
        bx = b - la * LOG2_E
        b_end = b[c_len - 1:c_len]
        st = st_ref[hd]
        qe = (q * jnp.exp2(b)).astype(BF16)
        o = lax.dot_general(qe, st.astype(BF16), nt, preferred_element_type=F32)
        kd_ref[hd] = jnp.zeros((c_len, GLA_DK), F32)
        for i in range(c_len // sub):
            lo_row = i * sub
            sl = slice(lo_row, lo_row + sub)
            bi, qi, ki = b[sl], q[sl], k[sl]
            ref_i = bx[lo_row:lo_row + 1]
            if i == 0:
                sc = jnp.zeros((sub, c_len), F32)
            else:
                pl_row = lo_row - sub
                ps = slice(pl_row, lo_row)
                if pl_row:
                    kd_ref[hd, :pl_row, :] = kd_ref[hd, :pl_row, :] * jnp.exp2(ref_i - bx[pl_row:pl_row + 1])
                kd_ref[hd, ps, :] = k[ps] * jnp.exp2(ref_i - b[ps])
                qd = (qi * jnp.exp2(bi - ref_i)).astype(BF16)
                sc = lax.dot_general(qd, kd_ref[hd].astype(BF16), nt, preferred_element_type=F32)
            for s in range(sub):
                dec = jnp.exp2(jnp.where(row_s >= s, bi - bi[s:s + 1], -1e30))
                cs = jnp.sum(qi * (ki[s:s + 1] * dec), axis=-1, keepdims=True)
                sc = jnp.where(col_s == lo_row + s, cs, sc)
            sc_ref[hd, sl, :] = sc
        o = o + jnp.dot(sc_ref[hd].astype(BF16), v, preferred_element_type=F32)
        ke = (k * jnp.exp2(b_end - b)).astype(BF16)
        upd = lax.dot_general(v, ke, (((0,), (0,)), ((), ())), preferred_element_type=F32)
        st_ref[hd] = st * jnp.exp2(b_end) + upd
        on = _layer_norm_rows(o) * g_ref[:, vcols]
        r = r_ref[rows, vcols].astype(F32)
        o_ref[rows, vcols] = (on * (r * _sigmoid(r))).astype(BF16)

    def chunk(c, _):
        rows = pl.ds(pl.multiple_of(c * c_len, c_len), c_len)
        for hd in range(n_heads):
            head_chunk(hd, rows)
        return 0

    lax.fori_loop(0, n_chunks, chunk, 0)


def _gla(proj, lr, w_gate, b_gate, norm_g, bsz, seq):
    n_tok = proj.shape[0]
    n_chunks = seq // GLA_CHUNK
    nh = GLA_HEADS_PER_STEP
    wk, wv = nh * GLA_DK, nh * GLA_DV
    wg = jnp.zeros((LANES, GLA_HEADS * GLA_DK), BF16).at[:GLA_RANK].set(w_gate.astype(BF16))
    return pl.pallas_call(
        functools.partial(_gla_kernel, n_chunks=n_chunks, n_heads=nh),
        out_shape=jax.ShapeDtypeStruct((n_tok, MIX_W), BF16),
        grid=(bsz, GLA_HEADS // nh),
        in_specs=[
            pl.BlockSpec((seq, wk), lambda b, h: (b, OFF_GL_Q // wk + h)),
            pl.BlockSpec((seq, wk), lambda b, h: (b, OFF_GL_K // wk + h)),
            pl.BlockSpec((seq, wv), lambda b, h: (b, OFF_GL_V // wv + h)),
            pl.BlockSpec((seq, wv), lambda b, h: (b, OFF_GL_R // wv + h)),
            pl.BlockSpec((seq, LANES), lambda b, h: (b, 0)),
            pl.BlockSpec((LANES, wk), lambda b, h: (0, h)),
            pl.BlockSpec((1, wk), lambda b, h: (0, h)),
            pl.BlockSpec((1, wv), lambda b, h: (0, h)),
        ],
        out_specs=pl.BlockSpec((seq, wv), lambda b, h: (b, h)),
        scratch_shapes=[pltpu.VMEM((nh, GLA_DV, GLA_DK), F32),
                        pltpu.VMEM((nh, GLA_CHUNK, GLA_CHUNK), F32),
                        pltpu.VMEM((nh, GLA_CHUNK, GLA_DK), F32)],
        compiler_params=_cparams(("parallel", "parallel")),
        name="gla",
    )(proj, proj, proj, proj, lr, wg, b_gate.reshape(1, -1), norm_g.reshape(1, -1))


def _merge_kernel(pc_ref, ps_ref, pb_ref, pg_ref, gc_ref, gs_ref, gb_ref, gg_ref,
                  wc_ref, wv_ref, wt_ref, wb_ref, wl_ref, o_ref):
    def mm(a_ref, w_ref):
        return jnp.dot(a_ref[...], w_ref[0], preferred_element_type=F32)

    def gate(ref):
        return _sigmoid(ref[...].astype(F32))

    y_s5 = mm(ps_ref, wv_ref) * _sigmoid(mm(ps_ref, wt_ref))
    merged = (gate(gc_ref) * mm(pc_ref, wc_ref) + gate(gs_ref) * y_s5
              + gate(gb_ref) * mm(pb_ref, wb_ref) + gate(gg_ref) * mm(pg_ref, wl_ref))
    o_ref[...] = merged.astype(BF16)


def _merge(pre_conv, pre_s5, pre_sb, pre_gla, proj, w_conv, w_val, w_gate, w_sb, w_gla, layer, tm, tn):
    n_tok = proj.shape[0]
    d = w_conv.shape[2]
    pre = pl.BlockSpec((tm, MIX_W), lambda i, j: (i, 0))
    gcol = lambda off: pl.BlockSpec((tm, tn), lambda i, j: (i, off // tn + j))
    wcol = pl.BlockSpec((1, MIX_W, tn), lambda i, j: (layer, 0, j))
    return pl.pallas_call(
        _merge_kernel,
        out_shape=jax.ShapeDtypeStruct((n_tok, d), BF16),
        grid=(n_tok // tm, d // tn),
        in_specs=[pre, pre, pre, pre,
                  gcol(OFF_G_CONV), gcol(OFF_G_S5), gcol(OFF_G_SB), gcol(OFF_G_GLA),
                  wcol, wcol, wcol, wcol, wcol],
        out_specs=pl.BlockSpec((tm, tn), lambda i, j: (i, j)),
        compiler_params=_cparams(("parallel", "arbitrary")),
        name="branch_merge",
    )(pre_conv, pre_s5, pre_sb, pre_gla, proj, proj, proj, proj, w_conv, w_val, w_gate, w_sb, w_gla)


def _out_proj_kernel(m_ref, x_ref, gt_ref, wo_ref, lg_ref, lb_ref, o_ref, *, alpha):
    for r in range(m_ref.shape[0] // OUT_ROW_CHUNK):
        sl = slice(r * OUT_ROW_CHUNK, (r + 1) * OUT_ROW_CHUNK)
        y = jnp.dot(m_ref[sl, :], wo_ref[0], preferred_element_type=F32)
        z = alpha * x_ref[sl, :] + gt_ref[0] * y
        o_ref[sl, :] = _layer_norm_rows(z) * lg_ref[0] + lb_ref[0]


def _out_proj_ln(merged, x, gt, w_o, ln_g, ln_b, layer, seq, alpha, tm):
    n_tok, d = x.shape
    per_seq = seq // tm
    vec = pl.BlockSpec((1, 1, d), lambda i: (layer, 0, 0))
    return pl.pallas_call(
        functools.partial(_out_proj_kernel, alpha=alpha),
        out_shape=jax.ShapeDtypeStruct((n_tok, d), F32),
        grid=(n_tok // tm,),
        in_specs=[pl.BlockSpec((tm, d), lambda i: (i, 0)),
                  pl.BlockSpec((tm, d), lambda i: (i, 0)),
                  pl.BlockSpec((1, 1, d), lambda i: (i // per_seq, 0, 0)),
                  pl.BlockSpec((1, d, d), lambda i: (layer, 0, 0)),
                  vec, vec],
        out_specs=pl.BlockSpec((tm, d), lambda i: (i, 0)),
        compiler_params=_cparams(("parallel",)),
        name="out_proj_ln",
    )(merged, x, gt, w_o, ln_g.reshape(-1, 1, d), ln_b.reshape(-1, 1, d))


def _ffn_kernel(x_ref, sc_ref, sh_ref, gt_ref, wg_ref, wu_ref, wd_ref, lg_ref, lb_ref, o_ref,
                h_ref, acc_ref, *, alpha):
    j = pl.program_id(1)

    @pl.when(j == 0)
    def _():
        _ln_modulate_into(h_ref, x_ref, sc_ref, sh_ref)
        acc_ref[...] = jnp.zeros_like(acc_ref)

    h = h_ref[...]
    gate = jnp.dot(h, wg_ref[0], preferred_element_type=F32)
    up = jnp.dot(h, wu_ref[0], preferred_element_type=F32)
    act = (gate * _sigmoid(gate) * up).astype(BF16)
    acc_ref[...] += jnp.dot(act, wd_ref[0], preferred_element_type=F32)

    @pl.when(j == pl.num_programs(1) - 1)
    def _():
        _residual_ln_into(o_ref, x_ref, acc_ref, gt_ref, lg_ref, lb_ref, alpha)


def _ffn(x, sc, sh, gt, w_gate, w_up, w_down, ln_g, ln_b, layer, seq, alpha, tm, tf):
    n_tok, d = x.shape
    d_ff = w_gate.shape[2]
    per_seq = seq // tm
    mod = pl.BlockSpec((1, 1, d), lambda i, j: (i // per_seq, 0, 0))
    vec = pl.BlockSpec((1, 1, d), lambda i, j: (layer, 0, 0))
    return pl.pallas_call(
        functools.partial(_ffn_kernel, alpha=alpha),
        out_shape=jax.ShapeDtypeStruct((n_tok, d), F32),
        grid=(n_tok // tm, d_ff // tf),
        in_specs=[pl.BlockSpec((tm, d), lambda i, j: (i, 0)), mod, mod, mod,
                  pl.BlockSpec((1, d, tf), lambda i, j: (layer, 0, j)),
                  pl.BlockSpec((1, d, tf), lambda i, j: (layer, 0, j)),
                  pl.BlockSpec((1, tf, d), lambda i, j: (layer, j, 0)),
                  vec, vec],
        out_specs=pl.BlockSpec((tm, d), lambda i, j: (i, 0)),
        scratch_shapes=[pltpu.VMEM((tm, d), BF16), pltpu.VMEM((tm, d), F32)],
        compiler_params=_cparams(("parallel", "arbitrary")),
        name="ffn_swiglu_ln",
    )(x, sc, sh, gt, w_gate, w_up, w_down, ln_g.reshape(-1, 1, d), ln_b.reshape(-1, 1, d))


def kernel(x, c, ada_w, ada_b, w_in, conv_w, w_conv_out, s5_lam_re, s5_lam_im, s5_b_re, s5_b_im, s5_c_re, s5_c_im, s5_d, s5_log_dt, w_s5_val, w_s5_gate, w_sb_out, gla_w_gate, gla_b_gate, gla_norm_g, w_gla_out, w_o, ln1_g, ln1_b, ffn_w_gate, ffn_w_up, ffn_w_down, ln2_g, ln2_b):
    bsz, seq, d = x.shape
    depth = ada_w.shape[0]
    alpha = (2.0 * depth) ** 0.25
    mod = _ada_mod(c, ada_w, ada_b, _tiles(seq)["ada_tn"])
    xf = x.reshape(bsz * seq, d)
    bf = lambda a: a.astype(BF16)
    t = _tiles(seq)
    w_in_parts = _w_in_prep(jnp.swapaxes(w_in, 1, 2), t["prep_tn"])
    merge_w = (bf(w_conv_out), bf(_channel_major(w_s5_val, 1)), bf(_channel_major(w_s5_gate, 1)),
               bf(w_sb_out), bf(w_gla_out))
    w_o_bf = bf(w_o)
    ffn_w = (bf(ffn_w_gate), bf(ffn_w_up), bf(ffn_w_down))
    s5_params = jax.vmap(_s5_params)(s5_lam_re, s5_lam_im, s5_b_re, s5_b_im, s5_c_re, s5_c_im, s5_d,
                                     s5_log_dt)
    for l in range(depth):
        sh1, sc1, gt1, sh2, sc2, gt2 = [m.reshape(bsz, 1, d) for m in jnp.split(mod[l], 6, axis=-1)]
        proj, lr, s5_u = _inproj(xf, sc1, sh1, *w_in_parts, l, seq, t["inproj_tm"], t["inproj_tn"])
        pre_conv = _short_conv(proj, conv_w[l], seq, t["conv_tc"])
        pre_s5 = _s5_gelu(s5_u, [p[l] for p in s5_params], bsz, seq)
        pre_sb = _stick_breaking(proj, bsz, seq, t["sb_tq"])
        pre_gla = _gla(proj, lr, gla_w_gate[l], gla_b_gate[l], gla_norm_g[l], bsz, seq)
        merged = _merge(pre_conv, pre_s5, pre_sb, pre_gla, proj, *merge_w, l, t["merge_tm"], t["merge_tn"])
        xf = _out_proj_ln(merged, xf, gt1, w_o_bf, ln1_g, ln1_b, l, seq, alpha, t["out_tm"])
        xf = _ffn(xf, sc2, sh2, gt2, *ffn_w, ln2_g, ln2_b, l, seq, alpha, t["ffn_tm"], t["ffn_tf"])
    return xf.reshape(bsz, seq, d)
```
